```python
import math
import jax, jax.numpy as jnp
from jax import lax
import numpy as np

D_MODEL = 1024
BATCH = 2
SEQ = 8192
DEPTH = 2

HEAD_DIM = 64
Q_BLOCK = 128
NSA_HEADS = 8
NSA_KV = 2
NSA_BLOCK = 64
NSA_TOPN = 16
NSA_LOCAL = 2
NSA_WINDOW = 512
NSA_CMP_HIDDEN = 256
SWA_HEADS = 8
SWA_KV = 2
SWA_WINDOW = 128
DIFF_HEADS = 4
N_EXPERTS = 64
TOP_K = 8
N_GROUPS = 8
TOPK_GROUPS = 4
EXPERT_DIM = 256
SHARED_DIM = 256
ROUTED_SCALE = 2.5
MOE_BLOCK = 128
PLE_DIM = 256
ALPHA = (2 * DEPTH) ** 0.25
BETA = (8 * DEPTH) ** -0.25
LN_EPS = 1e-5
NEG = -1e30
FORCE = 1e4

IN_SIZES = (NSA_HEADS * HEAD_DIM,) + (NSA_KV * HEAD_DIM,) * 6 + (NSA_HEADS * 3,) + \
    (SWA_HEADS * HEAD_DIM, SWA_KV * HEAD_DIM, SWA_KV * HEAD_DIM) + \
    (DIFF_HEADS * 2 * HEAD_DIM,) * 3 + (3 * D_MODEL,)
IN_COLS = sum(IN_SIZES)

kernel_name = "hybrid_nsa_swa_diff_moe_deepnorm"


def alibi_slopes(n):
    return jnp.asarray(2.0 ** (-8.0 * np.arange(1, n + 1) / n), dtype=jnp.float32)


def layer_norm(x, g, b):
    xf = x.astype(jnp.float32)
    mu = xf.mean(-1, keepdims=True)
    var = jnp.square(xf - mu).mean(-1, keepdims=True)
    return ((xf - mu) * lax.rsqrt(var + LN_EPS) * g + b).astype(x.dtype)


def masked_softmax(s, mask):
    s = jnp.where(mask, s, NEG)
    mx = s.max(-1, keepdims=True)
    e = jnp.where(mask, jnp.exp(s - mx), 0.0)
    return e / jnp.maximum(e.sum(-1, keepdims=True), 1e-30)


def banded_gqa(q, k, v, window, slopes, sinks=None):
    B, S, H, dh = q.shape
    HKV = k.shape[2]
    G = H // HKV
    nq = S // Q_BLOCK
    nb = window // Q_BLOCK + 1
    span = nb * Q_BLOCK
    pad = ((0, 0), (window, 0), (0, 0), (0, 0))
    kp = jnp.pad(k, pad).reshape(B, nq + window // Q_BLOCK, Q_BLOCK, HKV, dh)
    vp = jnp.pad(v, pad).reshape(B, nq + window // Q_BLOCK, Q_BLOCK, HKV, dh)
    kband = jnp.concatenate([kp[:, j:j + nq] for j in range(nb)], axis=2)
    vband = jnp.concatenate([vp[:, j:j + nq] for j in range(nb)], axis=2)
    qb = q.reshape(B, nq, Q_BLOCK, HKV, G, dh)
    s = jnp.einsum('bnqkgd,bnckd->bnkgqc', qb, kband).astype(jnp.float32) * (dh ** -0.5)
    blk0 = jnp.arange(nq)[:, None] * Q_BLOCK
    t = blk0 + jnp.arange(Q_BLOCK)[None, :]
    spos = blk0 - window + jnp.arange(span)[None, :]
    dist = t[:, :, None] - spos[:, None, :]
    mask = (dist >= 0) & (dist < window) & (spos >= 0)[:, None, :]
    slopes_kg = slopes.reshape(HKV, G)
    s = s - slopes_kg[None, None, :, :, None, None] * dist.astype(jnp.float32)[None, :, None, None, :, :]
    s = jnp.where(mask[None, :, None, None, :, :], s, NEG)
    if sinks is None:
        p = jax.nn.softmax(s, axis=-1)
    else:
        sk = jnp.broadcast_to(sinks.astype(jnp.float32).reshape(HKV, G)[None, None, :, :, None, None], s.shape[:-1] + (1,))
        p = jax.nn.softmax(jnp.concatenate([s, sk], axis=-1), axis=-1)[..., :-1]
    o = jnp.einsum('bnkgqc,bnckd->bnqkgd', p.astype(v.dtype), vband)
    return o.reshape(B, S, H, dh)


def nsa_attention(q, k_cmp, v_cmp, k_slc, v_slc, k_win, v_win, gates, pe_k, w1_k, w2_k, pe_v, w1_v, w2_v, slopes):
    B, S, H, dh = q.shape
    HKV = k_cmp.shape[2]
    G = H // HKV
    NB = S // NSA_BLOCK
    n_sel = min(NSA_TOPN, NB)
    scale = dh ** -0.5
    slopes_kg = slopes.reshape(HKV, G)

    def compress(kv, pe, w1, w2):
        kb = kv.reshape(B, NB, NSA_BLOCK, HKV, dh) + pe[None, None, :, None, :]
        kb = kb.transpose(0, 1, 3, 2, 4).reshape(B, NB, HKV, NSA_BLOCK * dh)
        return jax.nn.gelu(kb @ w1) @ w2

    kc = compress(k_cmp, pe_k, w1_k, w2_k)
    vc = compress(v_cmp, pe_v, w1_v, w2_v)
    qg = q.reshape(B, S, HKV, G, dh)
    t = jnp.arange(S)

    cpos = jnp.arange(NB) * NSA_BLOCK + NSA_BLOCK - 1
    dist_c = t[:, None] - cpos[None, :]
    s_c = jnp.einsum('bskgd,bnkd->bskgn', qg, kc).astype(jnp.float32) * scale
    s_c = s_c - slopes_kg[None, None, :, :, None] * dist_c.astype(jnp.float32)[None, :, None, None, :]
    p_c = masked_softmax(s_c, (dist_c >= 0)[None, :, None, None, :])
    o_c = jnp.einsum('bskgn,bnkd->bskgd', p_c.astype(vc.dtype), vc)

    imp = p_c.sum(axis=3)
    blk = jnp.arange(NB)
    rel = (t // NSA_BLOCK)[:, None] - blk[None, :]
    forced = (blk[None, :] == 0) | ((rel >= 0) & (rel < NSA_LOCAL))
    imp = jnp.where(forced[None, :, None, :], FORCE, jnp.where((rel >= 0)[None, :, None, :], imp, -1.0))
    top_val, top_idx = lax.top_k(imp, n_sel)
    sel_ok = top_val >= 0

    ksb = k_slc.reshape(B, NB, NSA_BLOCK, HKV, dh).transpose(0, 3, 1, 2, 4)
    vsb = v_slc.reshape(B, NB, NSA_BLOCK, HKV, dh).transpose(0, 3, 1, 2, 4)
    nq = S // Q_BLOCK
    bi = jnp.arange(B)[:, None, None, None]
    ki = jnp.arange(HKV)[None, None, :, None]

    def to_chunks(a):
        return a.reshape((B, nq, Q_BLOCK) + a.shape[2:]).swapaxes(0, 1)

    def sel_chunk(args):
        qi, idx, ok, n = args
        kg = ksb[bi, ki, idx]
        vg = vsb[bi, ki, idx]
        tq = n * Q_BLOCK + jnp.arange(Q_BLOCK)
        kpos = idx[..., None] * NSA_BLOCK + jnp.arange(NSA_BLOCK)
        dist = tq[None, :, None, None, None] - kpos
        m = ok[..., None] & (dist >= 0)
        L = n_sel * NSA_BLOCK
        kg = kg.reshape(B, Q_BLOCK, HKV, L, dh)
        vg = vg.reshape(B, Q_BLOCK, HKV, L, dh)
        dist = dist.reshape(B, Q_BLOCK, HKV, L).astype(jnp.float32)
        m = m.reshape(B, Q_BLOCK, HKV, L)
        s = jnp.einsum('bqkgd,bqkld->bqkgl', qi, kg).astype(jnp.float32) * scale
        s = s - slopes_kg[None, None, :, :, None] * dist[:, :, :, None, :]
        p = masked_softmax(s, m[:, :, :, None, :])
        return jnp.einsum('bqkgl,bqkld->bqkgd', p.astype(vg.dtype), vg)

    o_s = lax.map(sel_chunk, (to_chunks(qg), to_chunks(top_idx), to_chunks(sel_ok), jnp.arange(nq)))
    o_s = o_s.swapaxes(0, 1).reshape(B, S, H, dh)

    o_w = banded_gqa(q, k_win, v_win, NSA_WINDOW, slopes)

    g = jax.nn.sigmoid(gates)
    o = g[..., 0:1] * o_c.reshape(B, S, H, dh) + g[..., 1:2] * o_s + g[..., 2:3] * o_w
    return o.reshape(B, S, H * dh)


def diff_attention(q, k, v, lam, lambda_init, slopes, norm_g):
    B, S, HC, _, dh = q.shape
    nq = S // Q_BLOCK
    scale = dh ** -0.5
    spos = jnp.arange(S)
    qc = q.reshape(B, nq, Q_BLOCK, HC, 2, dh).swapaxes(0, 1)

    def chunk(args):
        qi, n = args
        tq = n * Q_BLOCK + jnp.arange(Q_BLOCK)
        dist = tq[:, None] - spos[None, :]
        s = jnp.einsum('bqhrd,bkhrd->bhrqk', qi, k).astype(jnp.float32) * scale
        s = s - slopes[None, :, None, None, None] * dist.astype(jnp.float32)[None, None, None, :, :]
        s = jnp.where((dist >= 0)[None, None, None, :, :], s, NEG)
        a = jax.nn.softmax(s, axis=-1)
        w = a[:, :, 0] - lam * a[:, :, 1]
        return jnp.einsum('bhqk,bkhe->bqhe', w.astype(v.dtype), v)

    o = lax.map(chunk, (qc, jnp.arange(nq))).swapaxes(0, 1).reshape(B, S, HC, 2 * dh)
    of = o.astype(jnp.float32)
    of = of * lax.rsqrt(jnp.mean(of * of, axis=-1, keepdims=True) + LN_EPS) * norm_g
    of = of * (1.0 - lambda_init)
    return of.astype(v.dtype).reshape(B, S, HC * 2 * dh)


def moe(x, router_w, router_bias, w_gate, w_up, w_down, sh_gate, sh_up, sh_down):
    B, S, D = x.shape
    T = B * S
    xt = x.reshape(T, D)
    scores = jax.nn.sigmoid((xt @ router_w).astype(jnp.float32))
    biased = scores + router_bias.astype(jnp.float32)
    grp = biased.reshape(T, N_GROUPS, N_EXPERTS // N_GROUPS)
    grp_score = lax.top_k(grp, 2)[0].sum(-1)
    _, grp_idx = lax.top_k(grp_score, TOPK_GROUPS)
    grp_mask = jax.nn.one_hot(grp_idx, N_GROUPS).sum(-2) > 0
    exp_mask = jnp.repeat(grp_mask, N_EXPERTS // N_GROUPS, axis=-1)
    _, eidx = lax.top_k(jnp.where(exp_mask, biased, NEG), TOP_K)
    w = jnp.take_along_axis(scores, eidx, axis=-1)
    w = w / w.sum(-1, keepdims=True) * ROUTED_SCALE

    A = T * TOP_K
    flat_e = eidx.reshape(A)
    flat_tok = jnp.arange(A) // TOP_K
    flat_w = w.reshape(A)
    order = jnp.argsort(flat_e)
    e_sorted = flat_e[order]
    counts = jnp.bincount(flat_e, length=N_EXPERTS)
    padded = (counts + MOE_BLOCK - 1) // MOE_BLOCK * MOE_BLOCK
    pad_end = jnp.cumsum(padded)
    pad_start = pad_end - padded
    start = jnp.cumsum(counts) - counts
    dest = pad_start[e_sorted] + jnp.arange(A) - start[e_sorted]
    P = A + N_EXPERTS * MOE_BLOCK
    NBP = P // MOE_BLOCK
    buf_tok = jnp.full((P,), T, dtype=jnp.int32).at[dest].set(flat_tok[order].astype(jnp.int32))
    buf_w = jnp.zeros((P,), jnp.float32).at[dest].set(flat_w[order])
    blk_exp = jnp.minimum(jnp.searchsorted(pad_end, jnp.arange(NBP) * MOE_BLOCK, side='right'), N_EXPERTS - 1)
    x_pad = jnp.concatenate([xt, jnp.zeros((1, D), xt.dtype)], axis=0)

    def expert_block(args):
        tok, wb, e = args
        xb = x_pad[tok]
        h = jax.nn.silu(xb @ w_gate[e]) * (xb @ w_up[e])
        return (h @ w_down[e]) * wb[:, None].astype(xb.dtype)

    yb = lax.map(expert_block, (buf_tok.reshape(NBP, MOE_BLOCK), buf_w.reshape(NBP, MOE_BLOCK), blk_exp))
    y = jnp.zeros((T + 1, D), x.dtype).at[buf_tok].add(yb.reshape(P, D))[:T]
    shared = (jax.nn.silu(xt @ sh_gate) * (xt @ sh_up)) @ sh_down
    return (y + shared).reshape(B, S, D)


def setup_inputs(seed: int = 0) -> dict:
    key = jax.random.key(seed)
    ks = iter(jax.random.split(key, 40))
    L, D = DEPTH, D_MODEL

    def nrm(shape, scale):
        return jax.random.normal(next(ks), shape, jnp.float32) * scale

    return {
        "x": nrm((BATCH, SEQ, D), 1.0),
        "p": nrm((DEPTH, BATCH, SEQ, PLE_DIM), 1.0),
        "w_in": nrm((L, D, IN_COLS), D ** -0.5),
        "nsa_pe_k": nrm((L, NSA_BLOCK, HEAD_DIM), 0.02),
        "nsa_w1_k": nrm((L, NSA_BLOCK * HEAD_DIM, NSA_CMP_HIDDEN), (NSA_BLOCK * HEAD_DIM) ** -0.5),
        "nsa_w2_k": nrm((L, NSA_CMP_HIDDEN, HEAD_DIM), NSA_CMP_HIDDEN ** -0.5),
        "nsa_pe_v": nrm((L, NSA_BLOCK, HEAD_DIM), 0.02),
        "nsa_w1_v": nrm((L, NSA_BLOCK * HEAD_DIM, NSA_CMP_HIDDEN), (NSA_BLOCK * HEAD_DIM) ** -0.5),
        "nsa_w2_v": nrm((L, NSA_CMP_HIDDEN, HEAD_DIM), NSA_CMP_HIDDEN ** -0.5),
        "swa_sinks": nrm((L, SWA_HEADS), 1.0),
        "diff_lq1": nrm((L, HEAD_DIM), 0.1),
        "diff_lk1": nrm((L, HEAD_DIM), 0.1),
        "diff_lq2": nrm((L, HEAD_DIM), 0.1),
        "diff_lk2": nrm((L, HEAD_DIM), 0.1),
        "diff_norm_g": 1.0 + nrm((L, 2 * HEAD_DIM), 0.02),
        "w_branch_a": nrm((L, NSA_HEADS * HEAD_DIM, D), (NSA_HEADS * HEAD_DIM) ** -0.5),
        "w_branch_b": nrm((L, SWA_HEADS * HEAD_DIM, D), (SWA_HEADS * HEAD_DIM) ** -0.5),
        "w_branch_c": nrm((L, DIFF_HEADS * 2 * HEAD_DIM, D), (DIFF_HEADS * 2 * HEAD_DIM) ** -0.5),
        "w_out": nrm((L, D, D), D ** -0.5 * BETA),
        "ln1_g": 1.0 + nrm((L, D), 0.02),
        "ln1_b": nrm((L, D), 0.02),
        "router_w": nrm((L, D, N_EXPERTS), D ** -0.5),
        "router_bias": nrm((L, N_EXPERTS), 0.01),
        "exp_w_gate": nrm((L, N_EXPERTS, D, EXPERT_DIM), D ** -0.5),
        "exp_w_up": nrm((L, N_EXPERTS, D, EXPERT_DIM), D ** -0.5),
        "exp_w_down": nrm((L, N_EXPERTS, EXPERT_DIM, D), EXPERT_DIM ** -0.5 * BETA),
        "sh_w_gate": nrm((L, D, SHARED_DIM), D ** -0.5),
        "sh_w_up": nrm((L, D, SHARED_DIM), D ** -0.5),
        "sh_w_down": nrm((L, SHARED_DIM, D), SHARED_DIM ** -0.5 * BETA),
        "ple_w_proj": nrm((L, PLE_DIM, D), PLE_DIM ** -0.5 * BETA),
        "ple_w_gate": nrm((L, D, D), D ** -0.5),
        "ln2_g": 1.0 + nrm((L, D), 0.02),
        "ln2_b": nrm((L, D), 0.02),
    }


def reference(x, p, w_in, nsa_pe_k, nsa_w1_k, nsa_w2_k, nsa_pe_v, nsa_w1_v, nsa_w2_v, swa_sinks,
              diff_lq1, diff_lk1, diff_lq2, diff_lk2, diff_norm_g, w_branch_a, w_branch_b, w_branch_c,
              w_out, ln1_g, ln1_b, router_w, router_bias, exp_w_gate, exp_w_up, exp_w_down,
              sh_w_gate, sh_w_up, sh_w_down, ple_w_proj, ple_w_gate, ln2_g, ln2_b):
    B, S, D = x.shape
    split_points = np.cumsum(IN_SIZES)[:-1].tolist()
    slopes_a = alibi_slopes(NSA_HEADS)
    slopes_b = alibi_slopes(SWA_HEADS)
    slopes_c = alibi_slopes(DIFF_HEADS)
    for i in range(DEPTH):
        proj = x @ w_in[i]
        (q_a, kc_a, vc_a, ks_a, vs_a, kw_a, vw_a, g_a,
         q_b, k_b, v_b, q_c, k_c, v_c, g_m) = jnp.split(proj, split_points, axis=-1)

        o_a = nsa_attention(
            q_a.reshape(B, S, NSA_HEADS, HEAD_DIM),
            kc_a.reshape(B, S, NSA_KV, HEAD_DIM), vc_a.reshape(B, S, NSA_KV, HEAD_DIM),
            ks_a.reshape(B, S, NSA_KV, HEAD_DIM), vs_a.reshape(B, S, NSA_KV, HEAD_DIM),
            kw_a.reshape(B, S, NSA_KV, HEAD_DIM), vw_a.reshape(B, S, NSA_KV, HEAD_DIM),
            g_a.reshape(B, S, NSA_HEADS, 3),
            nsa_pe_k[i], nsa_w1_k[i], nsa_w2_k[i], nsa_pe_v[i], nsa_w1_v[i], nsa_w2_v[i], slopes_a)

        o_b = banded_gqa(q_b.reshape(B, S, SWA_HEADS, HEAD_DIM), k_b.reshape(B, S, SWA_KV, HEAD_DIM),
                         v_b.reshape(B, S, SWA_KV, HEAD_DIM), SWA_WINDOW, slopes_b,
                         sinks=swa_sinks[i]).reshape(B, S, SWA_HEADS * HEAD_DIM)

        lambda_init = 0.8 - 0.6 * math.exp(-0.3 * i)
        lam = (jnp.exp(jnp.sum(diff_lq1[i].astype(jnp.float32) * diff_lk1[i].astype(jnp.float32)))
               - jnp.exp(jnp.sum(diff_lq2[i].astype(jnp.float32) * diff_lk2[i].astype(jnp.float32)))
               + lambda_init)
        o_c = diff_attention(q_c.reshape(B, S, DIFF_HEADS, 2, HEAD_DIM), k_c.reshape(B, S, DIFF_HEADS, 2, HEAD_DIM),
                             v_c.reshape(B, S, DIFF_HEADS, 2 * HEAD_DIM), lam, lambda_init, slopes_c, diff_norm_g[i])

        g = jax.nn.sigmoid(g_m).reshape(B, S, 3, D)
        merged = (g[:, :, 0] * (o_a @ w_branch_a[i]) + g[:, :, 1] * (o_b @ w_branch_b[i])
                  + g[:, :, 2] * (o_c @ w_branch_c[i]))
        x = layer_norm(ALPHA * x + merged @ w_out[i], ln1_g[i], ln1_b[i])

        ffn = moe(x, router_w[i], router_bias[i], exp_w_gate[i], exp_w_up[i], exp_w_down[i],
                  sh_w_gate[i], sh_w_up[i], sh_w_down[i])
        ple = jax.nn.sigmoid(x @ ple_w_gate[i]) * (p[i] @ ple_w_proj[i])
        x = layer_norm(ALPHA * x + ffn + ple, ln2_g[i], ln2_b[i])
    return x
```

```python
import functools
import math

import numpy as np
import jax
import jax.numpy as jnp
from jax import lax
from jax.experimental import pallas as pl
from jax.experimental.pallas import tpu as pltpu

F32 = jnp.float32
BF16 = jnp.bfloat16

HEAD_DIM = 64
NSA_HEADS = 8
NSA_KV = 2
NSA_BLOCK = 64
NSA_TOPN = 16
NSA_LOCAL = 2
NSA_WINDOW = 512
SWA_HEADS = 8
SWA_KV = 2
SWA_WINDOW = 128
DIFF_HEADS = 4
N_EXPERTS = 64
TOP_K = 8
N_GROUPS = 8
TOPK_GROUPS = 4
ROUTED_SCALE = 2.5
LN_EPS = 1e-5
NEG = -1e30
FORCE = 1e4
MASK_NEG = -(2.0 ** 100)

LANE = 128
GROUP = NSA_HEADS // NSA_KV
NBLK_PAD = 128
VMEM_LIMIT = 56 * 1024 * 1024

IN_SIZES = (NSA_HEADS * HEAD_DIM,) + (NSA_KV * HEAD_DIM,) * 6 + (NSA_HEADS * 3,) + \
    (SWA_HEADS * HEAD_DIM, SWA_KV * HEAD_DIM, SWA_KV * HEAD_DIM) + \
    (DIFF_HEADS * 2 * HEAD_DIM,) * 3
IN_OFFS = np.concatenate([[0], np.cumsum(IN_SIZES)]).tolist()
GM_OFF = IN_OFFS[-1]


def _slopes(n):
    return [2.0 ** (-8.0 * (h + 1) / n) for h in range(n)]


class _Layout:
    def __init__(self):
        src, scale, bias, pa, pb = [], [], [], [], []

        def slot(cols, sc=1.0, consts=(), pos=False, width=LANE):
            s = [-1] * width
            c = [0.0] * width
            a = [0.0] * width
            b = [0.0] * width
            s[:len(cols)] = cols
            for off, val in consts:
                c[off] = val
            if pos:
                a[HEAD_DIM] = 1.0
                b[HEAD_DIM + 1] = 1.0
            start = len(src)
            src.extend(s)
            scale.extend([sc] * width)
            bias.extend(c)
            pa.extend(a)
            pb.extend(b)
            return start

        def rng(base, n):
            return list(range(base, base + n))

        qs = HEAD_DIM ** -0.5
        o = IN_OFFS
        sl8 = _slopes(NSA_HEADS)
        sl4 = _slopes(DIFF_HEADS)
        self.qa = len(src)
        for h in range(NSA_HEADS):
            slot(rng(o[0] + h * HEAD_DIM, HEAD_DIM), qs,
                 [(HEAD_DIM, sl8[h] * LANE), (HEAD_DIM + 1, sl8[h])])
        self.qb = len(src)
        for h in range(SWA_HEADS):
            slot(rng(o[8] + h * HEAD_DIM, HEAD_DIM), qs,
                 [(HEAD_DIM, sl8[h] * LANE), (HEAD_DIM + 1, sl8[h])])
        self.qd = len(src)
        for h in range(DIFF_HEADS):
            for r in range(2):
                slot(rng(o[11] + (h * 2 + r) * HEAD_DIM, HEAD_DIM), qs,
                     [(HEAD_DIM, sl4[h] * LANE), (HEAD_DIM + 1, sl4[h])])
        self.kd = len(src)
        for h in range(DIFF_HEADS):
            for r in range(2):
                slot(rng(o[12] + (h * 2 + r) * HEAD_DIM, HEAD_DIM), pos=True)
        self.vd = len(src)
        for h in range(DIFF_HEADS):
            slot(rng(o[13] + h * 2 * HEAD_DIM, 2 * HEAD_DIM),
                 consts=[(2 * HEAD_DIM, 1.0)], width=2 * LANE)

        def kv_slots(k_src, v_src):
            k_off = len(src)
            for k in range(NSA_KV):
                slot(rng(k_src + k * HEAD_DIM, HEAD_DIM), pos=True)
            v_off = len(src)
            for k in range(NSA_KV):
                slot(rng(v_src + k * HEAD_DIM, HEAD_DIM), consts=[(HEAD_DIM, 1.0)])
            return k_off, v_off

        self.ks, self.vs = kv_slots(o[3], o[4])
        self.kw, self.vw = kv_slots(o[5], o[6])
        self.kb, self.vb = kv_slots(o[9], o[10])
        self.kc = slot(rng(o[1], NSA_KV * HEAD_DIM))
        self.vc = slot(rng(o[2], NSA_KV * HEAD_DIM))
        self.ga = slot([o[7] + h * 3 + c for c in range(3) for h in range(NSA_HEADS)])
        self.width = len(src)
        self.src = np.asarray(src, np.int32)
        self.scale = np.asarray(scale, np.float32)
        aux = np.zeros((8, self.width), np.float32)
        aux[0] = bias
        aux[1] = pa
        aux[2] = pb
        self.aux = aux


_LAYOUT = _Layout()


def _cparams(*sem):
    return pltpu.CompilerParams(dimension_semantics=sem, vmem_limit_bytes=VMEM_LIMIT)


def _tile(n, pref):
    t = min(n, pref)
    assert n % t == 0, (n, pref)
    return t


def _iota(shape, dim):
    return lax.broadcasted_iota(jnp.int32, shape, dim)


def _dot(a, b):
    return jnp.dot(a, b, preferred_element_type=F32)


def _dot_nt(a, b):
    return lax.dot_general(a, b, (((1,), (1,)), ((), ())), preferred_element_type=F32)


def _proj_kernel(x_ref, w_ref, aux_ref, o_ref, *, seq, tm):
    acc = _dot(x_ref[...], w_ref[...])
    pos = (pl.program_id(0) * tm) % seq + _iota((tm, 1), 0)
    a = (pos >> 7).astype(F32)
    b = (pos & (LANE - 1)).astype(F32)
    aux = aux_ref[...]
    o_ref[...] = (acc + aux[0:1] + a * aux[1:2] + b * aux[2:3]).astype(o_ref.dtype)


def _matmul_kernel(x_ref, w_ref, o_ref):
    o_ref[...] = _dot(x_ref[...], w_ref[...]).astype(o_ref.dtype)


def _project(xb, w, aux, seq, out_dtype, tn_pref):
    t, d = xb.shape
    n = w.shape[1]
    tm = _tile(t, 1024)
    tn = _tile(n, tn_pref)
    in_specs = [pl.BlockSpec((tm, d), lambda i, j: (i, 0)),
                pl.BlockSpec((d, tn), lambda i, j: (0, j))]
    args = [xb, w]
    if aux is None:
        body = _matmul_kernel
    else:
        body = functools.partial(_proj_kernel, seq=seq, tm=tm)
        in_specs.append(pl.BlockSpec((8, tn), lambda i, j: (0, j)))
        args.append(aux)
    return pl.pallas_call(
        body,
        grid=(t // tm, n // tn),
        in_specs=in_specs,
        out_specs=pl.BlockSpec((tm, tn), lambda i, j: (i, j)),
        out_shape=jax.ShapeDtypeStruct((t, n), out_dtype),
        compiler_params=_cparams("parallel", "parallel"),
        name="proj" if aux is not None else "gate_proj",
    )(*args)


def _compress_kernel(x_ref, pe_ref, w1_ref, w2_ref, o_ref):
    xb = (x_ref[0].astype(F32) + pe_ref[0]).astype(BF16)
    h = jax.nn.gelu(_dot(xb, w1_ref[0]))
    o_ref[0] = _dot(h.astype(BF16), w2_ref[0])


def _compress(xs, pes, w1s, w2s):
    _, r, kdim = xs.shape
    hid = w1s.shape[2]
    dh = w2s.shape[2]
    return pl.pallas_call(
        _compress_kernel,
        grid=(2,),
        in_specs=[pl.BlockSpec((1, r, kdim), lambda i: (i, 0, 0)),
                  pl.BlockSpec((1, 1, kdim), lambda i: (i, 0, 0)),
                  pl.BlockSpec((1, kdim, hid), lambda i: (i, 0, 0)),
                  pl.BlockSpec((1, hid, dh), lambda i: (i, 0, 0))],
        out_specs=pl.BlockSpec((1, r, dh), lambda i: (i, 0, 0)),
        out_shape=jax.ShapeDtypeStruct((2, r, dh), F32),
        compiler_params=_cparams("parallel"),
        name="nsa_compress",
    )(xs, pes, w1s, w2s)


def _stack_heads(q, n):
    return jnp.concatenate([q[:, g * LANE:(g + 1) * LANE] for g in range(n)], axis=0)


def _unstack_heads(o, n, tq, width):
    return jnp.concatenate([o[g * tq:(g + 1) * tq, :width] for g in range(n)], axis=1)


def _cmp_kernel(q_ref, kc_ref, vc_ref, oc_ref, sel_ref, *, tq, n_sel):
    i = pl.program_id(2)
    qs = _stack_heads(q_ref[0], GROUP)
    s = _dot_nt(qs, kc_ref[0, 0])
    t1 = i * tq + _iota((tq, 1), 0)
    t = jnp.concatenate([t1] * GROUP, axis=0)
    n = _iota((1, NBLK_PAD), 1)
    valid = (n * NSA_BLOCK + (NSA_BLOCK - 1)) <= t
    s = jnp.where(valid, s, NEG)
    mx = jnp.max(s, axis=-1, keepdims=True)
    e = jnp.where(valid, jnp.exp(s - mx), 0.0)
    p = e / jnp.maximum(jnp.sum(e, axis=-1, keepdims=True), 1e-30)
    o = _dot(p.astype(BF16), vc_ref[0, 0])
    oc_ref[0] = _unstack_heads(o, GROUP, tq, HEAD_DIM)

    imp = p[0:tq]
    for g in range(1, GROUP):
        imp = imp + p[g * tq:(g + 1) * tq]
    rel = (t1 >> 6) - n
    forced = (n == 0) | ((rel >= 0) & (rel < NSA_LOCAL))
    val = jnp.where(forced, FORCE, jnp.where(rel >= 0, imp, -1.0))
    sel = jnp.zeros((tq, NBLK_PAD), F32)
    for _ in range(n_sel):
        mx = jnp.max(val, axis=-1, keepdims=True)
        idx = jnp.min(jnp.where(val == mx, n, NBLK_PAD), axis=-1, keepdims=True)
        pick = n == idx
        sel = jnp.where(pick & (mx >= 0.0), 1.0, sel)
        val = jnp.where(pick, -jnp.inf, val)
    sel_ref[0, 0] = (1.0 - sel).astype(BF16)


def _nsa_compressed(proj, kca, vca, seq, n_sel):
    b = proj.shape[0]
    tq = _tile(seq, 256)
    lay = _LAYOUT
    gw = GROUP * LANE
    return pl.pallas_call(
        functools.partial(_cmp_kernel, tq=tq, n_sel=n_sel),
        grid=(b, NSA_KV, seq // tq),
        in_specs=[pl.BlockSpec((1, tq, gw), lambda b_, k, i: (b_, i, lay.qa // gw + k)),
                  pl.BlockSpec((1, 1, NBLK_PAD, LANE), lambda b_, k, i: (b_, k, 0, 0)),
                  pl.BlockSpec((1, 1, NBLK_PAD, LANE), lambda b_, k, i: (b_, k, 0, 0))],
        out_specs=[pl.BlockSpec((1, tq, GROUP * HEAD_DIM), lambda b_, k, i: (b_, i, k)),
                   pl.BlockSpec((1, 1, tq, NBLK_PAD), lambda b_, k, i: (b_, k, i, 0))],
        out_shape=[jax.ShapeDtypeStruct((b, seq, NSA_HEADS * HEAD_DIM), F32),
                   jax.ShapeDtypeStruct((b, NSA_KV, seq, NBLK_PAD), BF16)],
        compiler_params=_cparams("parallel", "parallel", "parallel"),
        name="nsa_compressed",
    )(proj, kca, vca)


def _flash(score_fn, v_ref, i, tq, rows, m_ref, acc_ref):
    m_ref[...] = jnp.full(m_ref.shape, NEG, F32)
    acc_ref[...] = jnp.zeros(acc_ref.shape, F32)

    def update(j, s):
        m_old = m_ref[...]
        m_new = jnp.maximum(m_old, jnp.max(s, axis=-1, keepdims=True))
        p = jnp.exp(s - m_new)
        start = pl.multiple_of(j * tq, tq)
        pv = _dot(p.astype(BF16), v_ref[pl.ds(start, tq), :])
        acc_ref[...] = jnp.exp(m_old - m_new) * acc_ref[...] + pv
        m_ref[...] = m_new

    def body(j, carry):
        update(j, score_fn(j))
        return carry

    lax.fori_loop(0, i, body, 0)
    reps = rows // tq
    tri = _iota((tq, tq), 0) >= _iota((tq, tq), 1)
    tri = jnp.concatenate([tri] * reps, axis=0)
    update(i, jnp.where(tri, score_fn(i), NEG))


def _sel_kernel(q_ref, sel_ref, k_ref, v_ref, o_ref, qa_ref, m_ref, acc_ref, *, tq):
    i = pl.program_id(2)
    q = q_ref[0]
    sc = sel_ref[0, 0]
    for g in range(GROUP):
        qa_ref[g * tq:(g + 1) * tq, 0:LANE] = q[:, g * LANE:(g + 1) * LANE]
        qa_ref[g * tq:(g + 1) * tq, LANE:2 * LANE] = sc
    kv = k_ref.at[0, 0]

    def scores(j):
        start = pl.multiple_of(j * tq, tq)
        return _dot_nt(qa_ref[...], kv[pl.ds(start, tq), :])

    _flash(scores, v_ref.at[0], i, tq, GROUP * tq, m_ref, acc_ref)
    acc = acc_ref[...]
    o = acc / acc[:, HEAD_DIM:HEAD_DIM + 1]
    o_ref[0] = _unstack_heads(o, GROUP, tq, HEAD_DIM)


def _nsa_selected(proj, selc, ksel, seq):
    b = proj.shape[0]
    tq = _tile(seq, 256)
    lay = _LAYOUT
    gw = GROUP * LANE
    return pl.pallas_call(
        functools.partial(_sel_kernel, tq=tq),
        grid=(b, NSA_KV, seq // tq),
        in_specs=[pl.BlockSpec((1, tq, gw), lambda b_, k, i: (b_, i, lay.qa // gw + k)),
                  pl.BlockSpec((1, 1, tq, NBLK_PAD), lambda b_, k, i: (b_, k, i, 0)),
                  pl.BlockSpec((1, 1, seq, 2 * LANE), lambda b_, k, i: (b_, k, 0, 0)),
                  pl.BlockSpec((1, seq, LANE), lambda b_, k, i: (b_, 0, lay.vs // LANE + k))],
        out_specs=pl.BlockSpec((1, tq, GROUP * HEAD_DIM), lambda b_, k, i: (b_, i, k)),
        out_shape=jax.ShapeDtypeStruct((b, seq, NSA_HEADS * HEAD_DIM), F32),
        scratch_shapes=[pltpu.VMEM((GROUP * tq, 2 * LANE), BF16),
                        pltpu.VMEM((GROUP * tq, 1), F32),
                        pltpu.VMEM((GROUP * tq, LANE), F32)],
        compiler_params=_cparams("parallel", "parallel", "arbitrary"),
        name="nsa_selected",
    )(proj, selc, ksel, proj)


def _diff_kernel(lam_ref, q_ref, k_ref, v_ref, g_ref, o_ref, m_ref, acc_ref, *, tq, lambda_init):
    i = pl.program_id(2)
    q = q_ref[0]
    kv = k_ref.at[0]

    def scores(j):
        start = pl.multiple_of(j * tq, tq)
        k = kv[pl.ds(start, tq), :]
        return jnp.concatenate([_dot_nt(q[:, 0:LANE], k[:, 0:LANE]),
                                _dot_nt(q[:, LANE:2 * LANE], k[:, LANE:2 * LANE])], axis=0)

    _flash(scores, v_ref.at[0], i, tq, 2 * tq, m_ref, acc_ref)
    acc = acc_ref[...]
    dv = 2 * HEAD_DIM
    a = acc[:, :dv] / acc[:, dv:dv + 1]
    lv = lam_ref[...]
    lam = (jnp.exp(jnp.sum(lv[0:1] * lv[1:2], axis=-1, keepdims=True))
           - jnp.exp(jnp.sum(lv[2:3] * lv[3:4], axis=-1, keepdims=True)) + lambda_init)
    o = a[0:tq] - lam * a[tq:2 * tq]
    o = o * lax.rsqrt(jnp.mean(o * o, axis=-1, keepdims=True) + LN_EPS) * g_ref[...]
    o_ref[0] = (o * (1.0 - lambda_init)).astype(o_ref.dtype)


def _diff_attention(proj, lam_vecs, norm_g, seq, lambda_init):
    b = proj.shape[0]
    tq = _tile(seq, 256)
    lay = _LAYOUT
    w2 = 2 * LANE
    return pl.pallas_call(
        functools.partial(_diff_kernel, tq=tq, lambda_init=lambda_init),
        grid=(b, DIFF_HEADS, seq // tq),
        in_specs=[pl.BlockSpec((4, HEAD_DIM), lambda b_, h, i: (0, 0)),
                  pl.BlockSpec((1, tq, w2), lambda b_, h, i: (b_, i, lay.qd // w2 + h)),
                  pl.BlockSpec((1, seq, w2), lambda b_, h, i: (b_, 0, lay.kd // w2 + h)),
                  pl.BlockSpec((1, seq, w2), lambda b_, h, i: (b_, 0, lay.vd // w2 + h)),
                  pl.BlockSpec((1, 2 * HEAD_DIM), lambda b_, h, i: (0, 0))],
        out_specs=pl.BlockSpec((1, tq, 2 * HEAD_DIM), lambda b_, h, i: (b_, i, h)),
        out_shape=jax.ShapeDtypeStruct((b, seq, DIFF_HEADS * 2 * HEAD_DIM), BF16),
        scratch_shapes=[pltpu.VMEM((2 * tq, 1), F32),
                        pltpu.VMEM((2 * tq, w2), F32)],
        compiler_params=_cparams("parallel", "parallel", "arbitrary"),
        name="diff_attention",
    )(lam_vecs, proj, proj, proj, norm_g)


def _band_kernel(hp_ref, q_ref, k_ref, v_ref, o_ref, *, tq, window, band, use_sinks):
    kvh = pl.program_id(1)
    i = pl.program_id(2)
    qs = _stack_heads(q_ref[0], GROUP)
    start = pl.multiple_of(jnp.maximum(i * tq - window, 0), LANE)
    s = _dot_nt(qs, k_ref[0, pl.ds(start, band), :])
    t1 = i * tq + _iota((tq, 1), 0)
    t = jnp.concatenate([t1] * GROUP, axis=0)
    dist = t - (start + _iota((1, band), 1))
    s = jnp.where((dist >= 0) & (dist < window), s, NEG)
    mx = jnp.max(s, axis=-1, keepdims=True)
    if use_sinks:
        tf = t1.astype(F32)
        sink = jnp.concatenate(
            [hp_ref[1, kvh * GROUP + g] + hp_ref[0, kvh * GROUP + g] * tf for g in range(GROUP)], axis=0)
        mx = jnp.maximum(mx, sink)
    p = jnp.exp(s - mx)
    acc = _dot(p.astype(BF16), v_ref[0, pl.ds(start, band), :])
    den = acc[:, HEAD_DIM:HEAD_DIM + 1]
    if use_sinks:
        den = den + jnp.exp(sink - mx)
    o_ref[0] = _unstack_heads(acc / den, GROUP, tq, HEAD_DIM).astype(o_ref.dtype)


def _banded(proj, head_params, seq, window, q_off, k_off, v_off, use_sinks, out_dtype, name):
    b = proj.shape[0]
    tq = _tile(seq, 256)
    band = min(window + tq, seq)
    gw = GROUP * LANE
    return pl.pallas_call(
        functools.partial(_band_kernel, tq=tq, window=window, band=band, use_sinks=use_sinks),
        grid=(b, NSA_KV, seq // tq),
        in_specs=[pl.BlockSpec(memory_space=pltpu.SMEM),
                  pl.BlockSpec((1, tq, gw), lambda b_, k, i: (b_, i, q_off // gw + k)),
                  pl.BlockSpec((1, seq, LANE), lambda b_, k, i: (b_, 0, k_off // LANE + k)),
                  pl.BlockSpec((1, seq, LANE), lambda b_, k, i: (b_, 0, v_off // LANE + k))],
        out_specs=pl.BlockSpec((1, tq, GROUP * HEAD_DIM), lambda b_, k, i: (b_, i, k)),
        out_shape=jax.ShapeDtypeStruct((b, seq, NSA_HEADS * HEAD_DIM), out_dtype),
        compiler_params=_cparams("parallel", "parallel", "parallel"),
        name=name,
    )(head_params, proj, proj, proj)


def _layer_norm(z, g, b):
    mu = jnp.mean(z, axis=-1, keepdims=True)
    zc = z - mu
    var = jnp.mean(zc * zc, axis=-1, keepdims=True)
    return zc * lax.rsqrt(var + LN_EPS) * g + b


def _merge_kernel(x_ref, oc_ref, os_ref, ow_ref, ga_ref, ob_ref, od_ref, gm_ref, ex_ref,
                  wa_ref, wb_ref, wc_ref, wo_ref, lng_ref, lnb_ref, xo_ref, xb_ref, *, alpha):
    d = x_ref.shape[1]
    ga = jax.nn.sigmoid(ga_ref[...].astype(F32)).astype(BF16)
    oa = (_dot(ga, ex_ref[0]) * oc_ref[...] + _dot(ga, ex_ref[1]) * os_ref[...]
          + _dot(ga, ex_ref[2]) * ow_ref[...])
    gm = jax.nn.sigmoid(gm_ref[...])
    merged = (gm[:, 0:d] * _dot(oa.astype(BF16), wa_ref[...])
              + gm[:, d:2 * d] * _dot(ob_ref[...], wb_ref[...])
              + gm[:, 2 * d:3 * d] * _dot(od_ref[...], wc_ref[...]))
    z = alpha * x_ref[...] + _dot(merged.astype(BF16), wo_ref[...])
    y = _layer_norm(z, lng_ref[...], lnb_ref[...])
    xo_ref[...] = y
    xb_ref[...] = y.astype(BF16)


def _merge(x, oc, os_, ow, proj2d, ob, od, gm, expand, wa, wb, wc, wo, lng, lnb, alpha):
    t, d = x.shape
    tm = _tile(t, 256)
    ha = NSA_HEADS * HEAD_DIM
    row = lambda w: pl.BlockSpec((tm, w), lambda i: (i, 0))
    full = lambda a: pl.BlockSpec(a.shape, lambda i: (0,) * a.ndim)
    return pl.pallas_call(
        functools.partial(_merge_kernel, alpha=alpha),
        grid=(t // tm,),
        in_specs=[row(d), row(ha), row(ha), row(ha),
                  pl.BlockSpec((tm, LANE), lambda i: (i, _LAYOUT.ga // LANE)),
                  row(ha), row(ha), row(3 * d), full(expand),
                  full(wa), full(wb), full(wc), full(wo), full(lng), full(lnb)],
        out_specs=[row(d), row(d)],
        out_shape=[jax.ShapeDtypeStruct((t, d), F32), jax.ShapeDtypeStruct((t, d), BF16)],
        compiler_params=_cparams("parallel"),
        name="merge_norm",
    )(x, oc, os_, ow, proj2d, ob, od, gm, expand, wa, wb, wc, wo, lng, lnb)


def _first_max(v, idx, n):
    mx = jnp.max(v, axis=0, keepdims=True)
    first = jnp.min(jnp.where(v == mx, idx, n), axis=0, keepdims=True)
    return mx, idx == first


def _router_kernel(x_ref, rw_ref, rb_ref, cw_ref):
    gsz = N_EXPERTS // N_GROUPS
    logits = lax.dot_general(rw_ref[...], x_ref[...], (((1,), (1,)), ((), ())),
                             precision=lax.Precision.HIGHEST,
                             preferred_element_type=F32)
    tm = logits.shape[1]
    scores = jax.nn.sigmoid(logits)
    biased = scores + rb_ref[...]
    iw = _iota((gsz, tm), 0)
    gs = []
    for g in range(N_GROUPS):
        blk = biased[g * gsz:(g + 1) * gsz]
        m1, pick = _first_max(blk, iw, gsz)
        m2 = jnp.max(jnp.where(pick, -jnp.inf, blk), axis=0, keepdims=True)
        gs.append(m1 + m2)
    gs = jnp.concatenate(gs, axis=0)
    ig = _iota((N_GROUPS, tm), 0)
    gmask = jnp.zeros((N_GROUPS, tm), F32)
    for _ in range(TOPK_GROUPS):
        _, pick = _first_max(gs, ig, N_GROUPS)
        gmask = jnp.where(pick, 1.0, gmask)
        gs = jnp.where(pick, -jnp.inf, gs)
    emask = jnp.concatenate(
        [jnp.broadcast_to(gmask[g:g + 1], (gsz, tm)) for g in range(N_GROUPS)], axis=0)
    cand = jnp.where(emask > 0.5, biased, NEG)
    ie = _iota((N_EXPERTS, tm), 0)
    chosen = jnp.zeros((N_EXPERTS, tm), F32)
    for _ in range(TOP_K):
        _, pick = _first_max(cand, ie, N_EXPERTS)
        chosen = jnp.where(pick, 1.0, chosen)
        cand = jnp.where(pick, -jnp.inf, cand)
    w = scores * chosen
    cw_ref[...] = w / jnp.sum(w, axis=0, keepdims=True) * ROUTED_SCALE


def _router(x, rw_t, rb):
    t, d = x.shape
    tm = _tile(t, 512)
    return pl.pallas_call(
        _router_kernel,
        grid=(t // tm,),
        in_specs=[pl.BlockSpec((tm, d), lambda i: (i, 0)),
                  pl.BlockSpec((N_EXPERTS, d), lambda i: (0, 0)),
                  pl.BlockSpec((N_EXPERTS, 1), lambda i: (0, 0))],
        out_specs=pl.BlockSpec((N_EXPERTS, tm), lambda i: (0, i)),
        out_shape=jax.ShapeDtypeStruct((N_EXPERTS, t), F32),
        compiler_params=_cparams("parallel"),
        name="router",
    )(x, rw_t, rb)


def _moe_kernel(x_ref, cw_ref, wg_ref, wu_ref, wd_ref, y_ref, *, ne):
    c = pl.program_id(1)
    x = x_ref[...]
    edim = wg_ref.shape[2]
    cw = cw_ref[...]
    cw_hi = cw.astype(BF16)
    cw_lo = (cw - cw_hi.astype(F32)).astype(BF16)
    expand = jnp.concatenate(
        [jnp.where(_iota((N_EXPERTS, edim), 0) == c * ne + e, 1.0, 0.0) for e in range(ne)],
        axis=1).astype(BF16)
    scale = _dot(cw_hi, expand) + _dot(cw_lo, expand)
    hs = []
    for e in range(ne):
        hs.append(jax.nn.silu(_dot(x, wg_ref[e])) * _dot(x, wu_ref[e]))
    h = (jnp.concatenate(hs, axis=1) * scale).astype(BF16)
    y = _dot(h, wd_ref[...].reshape(ne * edim, wd_ref.shape[2]))

    @pl.when(c == 0)
    def _():
        y_ref[...] = y

    @pl.when(c != 0)
    def _():
        y_ref[...] += y


def _moe_dense(xb, cw, wg, wu, wd):
    t, d = xb.shape
    edim = wg.shape[2]
    tm = _tile(t, 512)
    ne = 8
    return pl.pallas_call(
        functools.partial(_moe_kernel, ne=ne),
        grid=(t // tm, N_EXPERTS // ne),
        in_specs=[pl.BlockSpec((tm, d), lambda i, c: (i, 0)),
                  pl.BlockSpec((tm, N_EXPERTS), lambda i, c: (i, 0)),
                  pl.BlockSpec((ne, d, edim), lambda i, c: (c, 0, 0)),
                  pl.BlockSpec((ne, d, edim), lambda i, c: (c, 0, 0)),
                  pl.BlockSpec((ne, edim, d), lambda i, c: (c, 0, 0))],
        out_specs=pl.BlockSpec((tm, d), lambda i, c: (i, 0)),
        out_shape=jax.ShapeDtypeStruct((t, d), F32),
        compiler_params=_cparams("parallel", "arbitrary"),
        name="moe_experts",
    )(xb, cw, wg, wu, wd)


def _tail_kernel(x_ref, xb_ref, y_ref, p_ref, sg_ref, su_ref, sd_ref, pg_ref, pp_ref,
                 lng_ref, lnb_ref, xo_ref, xbo_ref, *, alpha):
    xb = xb_ref[...]
    h = jax.nn.silu(_dot(xb, sg_ref[...])) * _dot(xb, su_ref[...])
    shared = _dot(h.astype(BF16), sd_ref[...])
    ple = jax.nn.sigmoid(_dot(xb, pg_ref[...])) * _dot(p_ref[...].astype(BF16), pp_ref[...])
    z = alpha * x_ref[...] + y_ref[...] + shared + ple
    y = _layer_norm(z, lng_ref[...], lnb_ref[...])
    xo_ref[...] = y
    xbo_ref[...] = y.astype(BF16)


def _tail(x, xb, y, p, sg, su, sd, pg, pp, lng, lnb, alpha):
    t, d = x.shape
    tm = _tile(t, 256)
    row = lambda w: pl.BlockSpec((tm, w), lambda i: (i, 0))
    full = lambda a: pl.BlockSpec(a.shape, lambda i: (0,) * a.ndim)
    return pl.pallas_call(
        functools.partial(_tail_kernel, alpha=alpha),
        grid=(t // tm,),
        in_specs=[row(d), row(d), row(d), row(p.shape[1]), full(sg), full(su), full(sd),
                  full(pg), full(pp), full(lng), full(lnb)],
        out_specs=[row(d), row(d)],
        out_shape=[jax.ShapeDtypeStruct((t, d), F32), jax.ShapeDtypeStruct((t, d), BF16)],
        compiler_params=_cparams("parallel"),
        name="ffn_tail_norm",
    )(x, xb, y, p, sg, su, sd, pg, pp, lng, lnb)


def _compressed_slots(c, pos_cols):
    b, nb, hkv, dh = c.shape
    c = c.transpose(0, 2, 1, 3)
    extra = jnp.zeros((b, hkv, nb, LANE - dh), F32)
    if pos_cols:
        cpos = np.arange(nb) * NSA_BLOCK + NSA_BLOCK - 1
        cols = np.zeros((nb, LANE - dh), np.float32)
        cols[:, 0] = cpos // LANE
        cols[:, 1] = cpos % LANE
        extra = extra + jnp.asarray(cols)
    c = jnp.concatenate([c, extra], axis=-1)
    c = jnp.pad(c, ((0, 0), (0, 0), (0, NBLK_PAD - nb), (0, 0)))
    return c.astype(BF16)


def kernel(x, p, w_in, nsa_pe_k, nsa_w1_k, nsa_w2_k, nsa_pe_v, nsa_w1_v, nsa_w2_v, swa_sinks,
           diff_lq1, diff_lk1, diff_lq2, diff_lk2, diff_norm_g, w_branch_a, w_branch_b, w_branch_c,
           w_out, ln1_g, ln1_b, router_w, router_bias, exp_w_gate, exp_w_up, exp_w_down,
           sh_w_gate, sh_w_up, sh_w_down, ple_w_proj, ple_w_gate, ln2_g, ln2_b):
    bsz, seq, d = x.shape
    depth = w_in.shape[0]
    t = bsz * seq
    alpha = (2 * depth) ** 0.25
    lay = _LAYOUT
    nb = seq // NSA_BLOCK
    assert nb <= NBLK_PAD and seq % LANE == 0
    n_sel = min(NSA_TOPN, nb)

    aux = jnp.asarray(lay.aux)
    col_scale = jnp.asarray(lay.scale)
    src = jnp.asarray(np.where(lay.src < 0, GM_OFF + 3 * d, lay.src))
    blk_mask = jnp.asarray(
        (np.arange(seq)[:, None] // NSA_BLOCK == np.arange(NBLK_PAD)[None, :]) * MASK_NEG, BF16)
    ex = np.zeros((3, LANE, NSA_HEADS * HEAD_DIM), np.float32)
    for c in range(3):
        for h in range(NSA_HEADS):
            ex[c, c * NSA_HEADS + h, h * HEAD_DIM:(h + 1) * HEAD_DIM] = 1.0
    expand = jnp.asarray(ex, BF16)
    slopes8 = jnp.asarray(_slopes(NSA_HEADS), F32)

    xf = x.reshape(t, d)
    xb = xf.astype(BF16)
    for i in range(depth):
        w_ext = jnp.concatenate([w_in[i], jnp.zeros((d, 1), F32)], axis=1)
        w_slots = (jnp.take(w_ext, src, axis=1) * col_scale).astype(BF16)
        proj2d = _project(xb, w_slots, aux, seq, BF16, 1408)
        gm = _project(xb, w_in[i][:, GM_OFF:].astype(BF16), None, seq, F32, 1024)
        proj = proj2d.reshape(bsz, seq, lay.width)

        def blocks(off):
            c = proj[:, :, off:off + LANE].reshape(bsz, nb, NSA_BLOCK, NSA_KV, HEAD_DIM)
            return c.transpose(0, 1, 3, 2, 4).reshape(bsz * nb * NSA_KV, NSA_BLOCK * HEAD_DIM)
        cmp = _compress(
            jnp.stack([blocks(lay.kc), blocks(lay.vc)]),
            jnp.stack([nsa_pe_k[i].reshape(1, -1), nsa_pe_v[i].reshape(1, -1)]),
            jnp.stack([nsa_w1_k[i], nsa_w1_v[i]]).astype(BF16),
            jnp.stack([nsa_w2_k[i], nsa_w2_v[i]]).astype(BF16))
        cmp = cmp.reshape(2, bsz, nb, NSA_KV, HEAD_DIM)
        kca = _compressed_slots(cmp[0], True)
        vca = _compressed_slots(cmp[1], False)

        o_cmp, selc = _nsa_compressed(proj, kca, vca, seq, n_sel)
        ks = proj[:, :, lay.ks:lay.ks + NSA_KV * LANE].reshape(bsz, seq, NSA_KV, LANE).transpose(0, 2, 1, 3)
        ksel = jnp.concatenate(
            [ks, jnp.broadcast_to(blk_mask[None, None], (bsz, NSA_KV, seq, NBLK_PAD))], axis=-1)
        o_sel = _nsa_selected(proj, selc, ksel, seq)
        hp_a = jnp.stack([slopes8, jnp.zeros_like(slopes8)])
        o_win = _banded(proj, hp_a, seq, NSA_WINDOW, lay.qa, lay.kw, lay.vw, False, F32, "nsa_window")
        hp_b = jnp.stack([slopes8, swa_sinks[i].astype(F32)])
        o_b = _banded(proj, hp_b, seq, SWA_WINDOW, lay.qb, lay.kb, lay.vb, True, BF16, "swa_sinks")

        lambda_init = 0.8 - 0.6 * math.exp(-0.3 * i)
        lam_vecs = jnp.stack([diff_lq1[i], diff_lk1[i], diff_lq2[i], diff_lk2[i]]).astype(F32)
        o_d = _diff_attention(proj, lam_vecs, diff_norm_g[i].reshape(1, -1), seq, lambda_init)

        ha = NSA_HEADS * HEAD_DIM
        xf, xb = _merge(
            xf, o_cmp.reshape(t, ha), o_sel.reshape(t, ha), o_win.reshape(t, ha), proj2d,
            o_b.reshape(t, ha), o_d.reshape(t, ha), gm, expand,
            w_branch_a[i].astype(BF16), w_branch_b[i].astype(BF16), w_branch_c[i].astype(BF16),
            w_out[i].astype(BF16), ln1_g[i].reshape(1, d), ln1_b[i].reshape(1, d), alpha)

        cw_t = _router(xf, router_w[i].T, router_bias[i].reshape(-1, 1))
        y = _moe_dense(xb, cw_t.T, exp_w_gate[i].astype(BF16), exp_w_up[i].astype(BF16),
                       exp_w_down[i].astype(BF16))
        xf, xb = _tail(xf, xb, y, p[i].reshape(t, -1),
                       sh_w_gate[i].astype(BF16), sh_w_up[i].astype(BF16), sh_w_down[i].astype(BF16),
                       ple_w_gate[i].astype(BF16), ple_w_proj[i].astype(BF16),
                       ln2_g[i].reshape(1, d), ln2_b[i].reshape(1, d), alpha)
    return xf.reshape(bsz, seq, d)
```

```python
import functools
import math

import numpy as np
import jax
import jax.numpy as jnp
from jax import lax
from jax.experimental import pallas as pl
from jax.experimental.pallas import tpu as pltpu

F32 = jnp.float32
BF16 = jnp.bfloat16

HEAD_DIM = 64
NSA_HEADS = 8
NSA_KV = 2
NSA_BLOCK = 64
NSA_TOPN = 16
NSA_LOCAL = 2
NSA_WINDOW = 512
SWA_HEADS = 8
SWA_KV = 2
SWA_WINDOW = 128
DIFF_HEADS = 4
N_EXPERTS = 64
TOP_K = 8
N_GROUPS = 8
TOPK_GROUPS = 4
ROUTED_SCALE = 2.5
LN_EPS = 1e-5
NEG = -1e30
FORCE = 1e4
MASK_NEG = -(2.0 ** 100)

LANE = 128
GROUP = NSA_HEADS // NSA_KV
NBLK_PAD = 128
VMEM_LIMIT = 56 * 1024 * 1024

IN_SIZES = (NSA_HEADS * HEAD_DIM,) + (NSA_KV * HEAD_DIM,) * 6 + (NSA_HEADS * 3,) + \
    (SWA_HEADS * HEAD_DIM, SWA_KV * HEAD_DIM, SWA_KV * HEAD_DIM) + \
    (DIFF_HEADS * 2 * HEAD_DIM,) * 3
IN_OFFS = np.concatenate([[0], np.cumsum(IN_SIZES)]).tolist()
GM_OFF = IN_OFFS[-1]


def _slopes(n):
    return [2.0 ** (-8.0 * (h + 1) / n) for h in range(n)]


class _Layout:
    def __init__(self):
        src, scale, bias, pa, pb = [], [], [], [], []

        def slot(cols, sc=1.0, consts=(), pos=False, width=LANE):
            s = [-1] * width
            c = [0.0] * width
            a = [0.0] * width
            b = [0.0] * width
            s[:len(cols)] = cols
            for off, val in consts:
                c[off] = val
            if pos:
                a[HEAD_DIM] = 1.0
                b[HEAD_DIM + 1] = 1.0
            start = len(src)
            src.extend(s)
            scale.extend([sc] * width)
            bias.extend(c)
            pa.extend(a)
            pb.extend(b)
            return start

        def rng(base, n):
            return list(range(base, base + n))

        qs = HEAD_DIM ** -0.5
        o = IN_OFFS
        sl8 = _slopes(NSA_HEADS)
        sl4 = _slopes(DIFF_HEADS)
        self.qa = len(src)
        for h in range(NSA_HEADS):
            slot(rng(o[0] + h * HEAD_DIM, HEAD_DIM), qs,
                 [(HEAD_DIM, sl8[h] * LANE), (HEAD_DIM + 1, sl8[h])])
        self.qb = len(src)
        for h in range(SWA_HEADS):
            slot(rng(o[8] + h * HEAD_DIM, HEAD_DIM), qs,
                 [(HEAD_DIM, sl8[h] * LANE), (HEAD_DIM + 1, sl8[h])])
        self.qd = len(src)
        for h in range(DIFF_HEADS):
            for r in range(2):
                slot(rng(o[11] + (h * 2 + r) * HEAD_DIM, HEAD_DIM), qs,
                     [(HEAD_DIM, sl4[h] * LANE), (HEAD_DIM + 1, sl4[h])])
        self.kd = len(src)
        for h in range(DIFF_HEADS):
            for r in range(2):
                slot(rng(o[12] + (h * 2 + r) * HEAD_DIM, HEAD_DIM), pos=True)
        self.vd = len(src)
        for h in range(DIFF_HEADS):
            slot(rng(o[13] + h * 2 * HEAD_DIM, 2 * HEAD_DIM),
                 consts=[(2 * HEAD_DIM, 1.0)], width=2 * LANE)

        def kv_slots(k_src, v_src):
            k_off = len(src)
            for k in range(NSA_KV):
                slot(rng(k_src + k * HEAD_DIM, HEAD_DIM), pos=True)
            v_off = len(src)
            for k in range(NSA_KV):
                slot(rng(v_src + k * HEAD_DIM, HEAD_DIM), consts=[(HEAD_DIM, 1.0)])
            return k_off, v_off

        self.ks, self.vs = kv_slots(o[3], o[4])
        self.kw, self.vw = kv_slots(o[5], o[6])
        self.kb, self.vb = kv_slots(o[9], o[10])
        self.kc = slot(rng(o[1], NSA_KV * HEAD_DIM))
        self.vc = slot(rng(o[2], NSA_KV * HEAD_DIM))
        self.ga = slot([o[7] + h * 3 + c for c in range(3) for h in range(NSA_HEADS)])
        self.width = len(src)
        self.src = np.asarray(src, np.int32)
        self.scale = np.asarray(scale, np.float32)
        aux = np.zeros((8, self.width), np.float32)
        aux[0] = bias
        aux[1] = pa
        aux[2] = pb
        self.aux = aux


_LAYOUT = _Layout()


def _cparams(*sem):
    return pltpu.CompilerParams(dimension_semantics=sem, vmem_limit_bytes=VMEM_LIMIT)


def _tile(n, pref):
    t = min(n, pref)
    assert n % t == 0, (n, pref)
    return t


def _iota(shape, dim):
    return lax.broadcasted_iota(jnp.int32, shape, dim)


def _dot(a, b):
    return jnp.dot(a, b, preferred_element_type=F32)


def _dot_nt(a, b):
    return lax.dot_general(a, b, (((1,), (1,)), ((), ())), preferred_element_type=F32)


def _proj_kernel(x_ref, w_ref, aux_ref, o_ref, *, seq, tm):
    acc = _dot(x_ref[...], w_ref[...])
    pos = (pl.program_id(0) * tm) % seq + _iota((tm, 1), 0)
    a = (pos >> 7).astype(F32)
    b = (pos & (LANE - 1)).astype(F32)
    aux = aux_ref[...]
    o_ref[...] = (acc + aux[0:1] + a * aux[1:2] + b * aux[2:3]).astype(o_ref.dtype)


def _matmul_kernel(x_ref, w_ref, o_ref):
    o_ref[...] = _dot(x_ref[...], w_ref[...]).astype(o_ref.dtype)


def _project(xb, w, aux, seq, out_dtype, tn_pref):
    t, d = xb.shape
    n = w.shape[1]
    tm = _tile(t, 1024)
    tn = _tile(n, tn_pref)
    in_specs = [pl.BlockSpec((tm, d), lambda i, j: (i, 0)),
                pl.BlockSpec((d, tn), lambda i, j: (0, j))]
    args = [xb, w]
    if aux is None:
        body = _matmul_kernel
    else:
        body = functools.partial(_proj_kernel, seq=seq, tm=tm)
        in_specs.append(pl.BlockSpec((8, tn), lambda i, j: (0, j)))
        args.append(aux)
    return pl.pallas_call(
        body,
        grid=(t // tm, n // tn),
        in_specs=in_specs,
        out_specs=pl.BlockSpec((tm, tn), lambda i, j: (i, j)),
        out_shape=jax.ShapeDtypeStruct((t, n), out_dtype),
        compiler_params=_cparams("parallel", "parallel"),
        name="proj" if aux is not None else "gate_proj",
    )(*args)


def _compress_kernel(x_ref, pe_ref, w1_ref, w2_ref, o_ref):
    xb = (x_ref[0].astype(F32) + pe_ref[0]).astype(BF16)
    h = jax.nn.gelu(_dot(xb, w1_ref[0]))
    o_ref[0] = _dot(h.astype(BF16), w2_ref[0])


def _compress(xs, pes, w1s, w2s):
    _, r, kdim = xs.shape
    hid = w1s.shape[2]
    dh = w2s.shape[2]
    return pl.pallas_call(
        _compress_kernel,
        grid=(2,),
        in_specs=[pl.BlockSpec((1, r, kdim), lambda i: (i, 0, 0)),
                  pl.BlockSpec((1, 1, kdim), lambda i: (i, 0, 0)),
                  pl.BlockSpec((1, kdim, hid), lambda i: (i, 0, 0)),
                  pl.BlockSpec((1, hid, dh), lambda i: (i, 0, 0))],
        out_specs=pl.BlockSpec((1, r, dh), lambda i: (i, 0, 0)),
        out_shape=jax.ShapeDtypeStruct((2, r, dh), F32),
        compiler_params=_cparams("parallel"),
        name="nsa_compress",
    )(xs, pes, w1s, w2s)


def _stack_heads(q, n):
    return jnp.concatenate([q[:, g * LANE:(g + 1) * LANE] for g in range(n)], axis=0)


def _unstack_heads(o, n, tq, width):
    return jnp.concatenate([o[g * tq:(g + 1) * tq, :width] for g in range(n)], axis=1)


def _cmp_kernel(q_ref, kc_ref, vc_ref, oc_ref, sel_ref, *, tq, n_sel):
    i = pl.program_id(2)
    qs = _stack_heads(q_ref[0], GROUP)
    s = _dot_nt(qs, kc_ref[0, 0])
    t1 = i * tq + _iota((tq, 1), 0)
    t = jnp.concatenate([t1] * GROUP, axis=0)
    n = _iota((1, NBLK_PAD), 1)
    valid = (n * NSA_BLOCK + (NSA_BLOCK - 1)) <= t
    s = jnp.where(valid, s, NEG)
    mx = jnp.max(s, axis=-1, keepdims=True)
    e = jnp.where(valid, jnp.exp(s - mx), 0.0)
    p = e / jnp.maximum(jnp.sum(e, axis=-1, keepdims=True), 1e-30)
    o = _dot(p.astype(BF16), vc_ref[0, 0])
    oc_ref[0] = _unstack_heads(o, GROUP, tq, HEAD_DIM)

    imp = p[0:tq]
    for g in range(1, GROUP):
        imp = imp + p[g * tq:(g + 1) * tq]
    rel = (t1 >> 6) - n
    forced = (n == 0) | ((rel >= 0) & (rel < NSA_LOCAL))
    val = jnp.where(forced, FORCE, jnp.where(rel >= 0, imp, -1.0))
    sel = jnp.zeros((tq, NBLK_PAD), F32)
    for _ in range(n_sel):
        mx = jnp.max(val, axis=-1, keepdims=True)
        idx = jnp.min(jnp.where(val == mx, n, NBLK_PAD), axis=-1, keepdims=True)
        pick = n == idx
        sel = jnp.where(pick & (mx >= 0.0), 1.0, sel)
        val = jnp.where(pick, -jnp.inf, val)
    sel_ref[0, 0] = (1.0 - sel).astype(BF16)


def _nsa_compressed(proj, kca, vca, seq, n_sel):
    b = proj.shape[0]
    tq = _tile(seq, 256)
    lay = _LAYOUT
    gw = GROUP * LANE
    return pl.pallas_call(
        functools.partial(_cmp_kernel, tq=tq, n_sel=n_sel),
        grid=(b, NSA_KV, seq // tq),
        in_specs=[pl.BlockSpec((1, tq, gw), lambda b_, k, i: (b_, i, lay.qa // gw + k)),
                  pl.BlockSpec((1, 1, NBLK_PAD, LANE), lambda b_, k, i: (b_, k, 0, 0)),
                  pl.BlockSpec((1, 1, NBLK_PAD, LANE), lambda b_, k, i: (b_, k, 0, 0))],
        out_specs=[pl.BlockSpec((1, tq, GROUP * HEAD_DIM), lambda b_, k, i: (b_, i, k)),
                   pl.BlockSpec((1, 1, tq, NBLK_PAD), lambda b_, k, i: (b_, k, i, 0))],
        out_shape=[jax.ShapeDtypeStruct((b, seq, NSA_HEADS * HEAD_DIM), F32),
                   jax.ShapeDtypeStruct((b, NSA_KV, seq, NBLK_PAD), BF16)],
        compiler_params=_cparams("parallel", "parallel", "parallel"),
        name="nsa_compressed",
    )(proj, kca, vca)


def _flash(chunk_fns, v_ref, i, tq, tk, m_ref, acc_ref):
    m_ref[...] = jnp.full(m_ref.shape, NEG, F32)
    acc_ref[...] = jnp.zeros(acc_ref.shape, F32)
    acc_reps = acc_ref.shape[1] // LANE

    def update(j, visible):
        start = pl.multiple_of(j * tk, tk)
        vt = v_ref[pl.ds(start, tk), :]
        for c, fn in enumerate(chunk_fns):
            rows = slice(c * tq, (c + 1) * tq)
            s = fn(start)
            if visible is not None:
                s = jnp.where(visible, s, NEG)
            m_old = m_ref[rows, :]
            m_new = jnp.maximum(m_old, jnp.max(s, axis=-1, keepdims=True))
            p = jnp.exp(s - jnp.concatenate([m_new] * (tk // LANE), axis=1))
            alpha = jnp.concatenate([jnp.exp(m_old - m_new)] * acc_reps, axis=1)
            acc_ref[rows, :] = alpha * acc_ref[rows, :] + _dot(p.astype(BF16), vt)
            m_ref[rows, :] = m_new

    def body(j, carry):
        update(j, None)
        return carry

    n_full = (i * tq) // tk
    lax.fori_loop(0, n_full, body, 0)
    visible = (_iota((tq, tk), 1) - _iota((tq, tk), 0)) <= i * tq - n_full * tk
    update(n_full, visible)


def _sel_kernel(q_ref, sel_ref, k_ref, v_ref, o_ref, qa_ref, m_ref, acc_ref, *, tq, tk):
    i = pl.program_id(2)
    q = q_ref[0]
    sc = sel_ref[0, 0]
    for g in range(GROUP):
        qa_ref[g * tq:(g + 1) * tq, 0:LANE] = q[:, g * LANE:(g + 1) * LANE]
        qa_ref[g * tq:(g + 1) * tq, LANE:2 * LANE] = sc
    kv = k_ref.at[0, 0]

    def head_scores(g):
        return lambda start: _dot_nt(qa_ref[g * tq:(g + 1) * tq, :], kv[pl.ds(start, tk), :])

    _flash([head_scores(g) for g in range(GROUP)], v_ref.at[0], i, tq, tk, m_ref, acc_ref)
    acc = acc_ref[...]
    o = acc / acc[:, HEAD_DIM:HEAD_DIM + 1]
    o_ref[0] = _unstack_heads(o, GROUP, tq, HEAD_DIM)


def _nsa_selected(proj, selc, ksel, seq):
    b = proj.shape[0]
    tq = _tile(seq, 256)
    tk = _tile(seq, 512)
    lay = _LAYOUT
    gw = GROUP * LANE
    return pl.pallas_call(
        functools.partial(_sel_kernel, tq=tq, tk=tk),
        grid=(b, NSA_KV, seq // tq),
        in_specs=[pl.BlockSpec((1, tq, gw), lambda b_, k, i: (b_, i, lay.qa // gw + k)),
                  pl.BlockSpec((1, 1, tq, NBLK_PAD), lambda b_, k, i: (b_, k, i, 0)),
                  pl.BlockSpec((1, 1, seq, 2 * LANE), lambda b_, k, i: (b_, k, 0, 0)),
                  pl.BlockSpec((1, seq, LANE), lambda b_, k, i: (b_, 0, lay.vs // LANE + k))],
        out_specs=pl.BlockSpec((1, tq, GROUP * HEAD_DIM), lambda b_, k, i: (b_, i, k)),
        out_shape=jax.ShapeDtypeStruct((b, seq, NSA_HEADS * HEAD_DIM), F32),
        scratch_shapes=[pltpu.VMEM((GROUP * tq, 2 * LANE), BF16),
                        pltpu.VMEM((GROUP * tq, LANE), F32),
                        pltpu.VMEM((GROUP * tq, LANE), F32)],
        compiler_params=_cparams("parallel", "parallel", "arbitrary"),
        name="nsa_selected",
    )(proj, selc, ksel, proj)


def _diff_kernel(lam_ref, q_ref, k_ref, v_ref, g_ref, o_ref, m_ref, acc_ref, *, tq, tk, lambda_init):
    i = pl.program_id(2)
    kv = k_ref.at[0]

    def map_scores(r):
        cols = slice(r * LANE, (r + 1) * LANE)
        return lambda start: _dot_nt(q_ref[0, :, cols], kv[pl.ds(start, tk), cols])

    _flash([map_scores(0), map_scores(1)], v_ref.at[0], i, tq, tk, m_ref, acc_ref)
    acc = acc_ref[...]
    dv = 2 * HEAD_DIM
    a = acc[:, :dv] / acc[:, dv:dv + 1]
    lv = lam_ref[...]
    lam = (jnp.exp(jnp.sum(lv[0:1] * lv[1:2], axis=-1, keepdims=True))
           - jnp.exp(jnp.sum(lv[2:3] * lv[3:4], axis=-1, keepdims=True)) + lambda_init)
    o = a[0:tq] - lam * a[tq:2 * tq]
    o = o * lax.rsqrt(jnp.mean(o * o, axis=-1, keepdims=True) + LN_EPS) * g_ref[...]
    o_ref[0] = (o * (1.0 - lambda_init)).astype(o_ref.dtype)


def _diff_attention(proj, lam_vecs, norm_g, seq, lambda_init):
    b = proj.shape[0]
    tq = _tile(seq, 512)
    tk = _tile(seq, 512)
    lay = _LAYOUT
    w2 = 2 * LANE
    return pl.pallas_call(
        functools.partial(_diff_kernel, tq=tq, tk=tk, lambda_init=lambda_init),
        grid=(b, DIFF_HEADS, seq // tq),
        in_specs=[pl.BlockSpec((4, HEAD_DIM), lambda b_, h, i: (0, 0)),
                  pl.BlockSpec((1, tq, w2), lambda b_, h, i: (b_, i, lay.qd // w2 + h)),
                  pl.BlockSpec((1, seq, w2), lambda b_, h, i: (b_, 0, lay.kd // w2 + h)),
                  pl.BlockSpec((1, seq, w2), lambda b_, h, i: (b_, 0, lay.vd // w2 + h)),
                  pl.BlockSpec((1, 2 * HEAD_DIM), lambda b_, h, i: (0, 0))],
        out_specs=pl.BlockSpec((1, tq, 2 * HEAD_DIM), lambda b_, h, i: (b_, i, h)),
        out_shape=jax.ShapeDtypeStruct((b, seq, DIFF_HEADS * 2 * HEAD_DIM), BF16),
        scratch_shapes=[pltpu.VMEM((2 * tq, LANE), F32),
                        pltpu.VMEM((2 * tq, w2), F32)],
        compiler_params=_cparams("parallel", "parallel", "arbitrary"),
        name="diff_attention",
    )(lam_vecs, proj, proj, proj, norm_g)


def _band_kernel(hp_ref, q_ref, k_ref, v_ref, o_ref, *, tq, window, band, use_sinks):
    kvh = pl.program_id(1)
    i = pl.program_id(2)
    qs = _stack_heads(q_ref[0], GROUP)
    start = pl.multiple_of(jnp.maximum(i * tq - window, 0), LANE)
    s = _dot_nt(qs, k_ref[0, pl.ds(start, band), :])
    t1 = i * tq + _iota((tq, 1), 0)
    t = jnp.concatenate([t1] * GROUP, axis=0)
    dist = t - (start + _iota((1, band), 1))
    s = jnp.where((dist >= 0) & (dist < window), s, NEG)
    mx = jnp.max(s, axis=-1, keepdims=True)
    if use_sinks:
        tf = t1.astype(F32)
        sink = jnp.concatenate(
            [hp_ref[1, kvh * GROUP + g] + hp_ref[0, kvh * GROUP + g] * tf for g in range(GROUP)], axis=0)
        mx = jnp.maximum(mx, sink)
    p = jnp.exp(s - mx)
    acc = _dot(p.astype(BF16), v_ref[0, pl.ds(start, band), :])
    den = acc[:, HEAD_DIM:HEAD_DIM + 1]
    if use_sinks:
        den = den + jnp.exp(sink - mx)
    o_ref[0] = _unstack_heads(acc / den, GROUP, tq, HEAD_DIM).astype(o_ref.dtype)


def _banded(proj, head_params, seq, window, q_off, k_off, v_off, use_sinks, out_dtype, name):
    b = proj.shape[0]
    tq = _tile(seq, 256)
    band = min(window + tq, seq)
    gw = GROUP * LANE
    return pl.pallas_call(
        functools.partial(_band_kernel, tq=tq, window=window, band=band, use_sinks=use_sinks),
        grid=(b, NSA_KV, seq // tq),
        in_specs=[pl.BlockSpec(memory_space=pltpu.SMEM),
                  pl.BlockSpec((1, tq, gw), lambda b_, k, i: (b_, i, q_off // gw + k)),
                  pl.BlockSpec((1, seq, LANE), lambda b_, k, i: (b_, 0, k_off // LANE + k)),
                  pl.BlockSpec((1, seq, LANE), lambda b_, k, i: (b_, 0, v_off // LANE + k))],
        out_specs=pl.BlockSpec((1, tq, GROUP * HEAD_DIM), lambda b_, k, i: (b_, i, k)),
        out_shape=jax.ShapeDtypeStruct((b, seq, NSA_HEADS * HEAD_DIM), out_dtype),
        compiler_params=_cparams("parallel", "parallel", "parallel"),
        name=name,
    )(head_params, proj, proj, proj)


def _layer_norm(z, g, b):
    mu = jnp.mean(z, axis=-1, keepdims=True)
    zc = z - mu
    var = jnp.mean(zc * zc, axis=-1, keepdims=True)
    return zc * lax.rsqrt(var + LN_EPS) * g + b


def _merge_kernel(x_ref, oc_ref, os_ref, ow_ref, ga_ref, ob_ref, od_ref, gm_ref, ex_ref,
                  wa_ref, wb_ref, wc_ref, wo_ref, lng_ref, lnb_ref, xo_ref, xb_ref, *, alpha):
    d = x_ref.shape[1]
    ga = jax.nn.sigmoid(ga_ref[...].astype(F32)).astype(BF16)
    oa = (_dot(ga, ex_ref[0]) * oc_ref[...] + _dot(ga, ex_ref[1]) * os_ref[...]
          + _dot(ga, ex_ref[2]) * ow_ref[...])
    gm = jax.nn.sigmoid(gm_ref[...])
    merged = (gm[:, 0:d] * _dot(oa.astype(BF16), wa_ref[...])
              + gm[:, d:2 * d] * _dot(ob_ref[...], wb_ref[...])
              + gm[:, 2 * d:3 * d] * _dot(od_ref[...], wc_ref[...]))
    z = alpha * x_ref[...] + _dot(merged.astype(BF16), wo_ref[...])
    y = _layer_norm(z, lng_ref[...], lnb_ref[...])
    xo_ref[...] = y
    xb_ref[...] = y.astype(BF16)


def _merge(x, oc, os_, ow, proj2d, ob, od, gm, expand, wa, wb, wc, wo, lng, lnb, alpha):
    t, d = x.shape
    tm = _tile(t, 256)
    ha = NSA_HEADS * HEAD_DIM
    row = lambda w: pl.BlockSpec((tm, w), lambda i: (i, 0))
    full = lambda a: pl.BlockSpec(a.shape, lambda i: (0,) * a.ndim)
    return pl.pallas_call(
        functools.partial(_merge_kernel, alpha=alpha),
        grid=(t // tm,),
        in_specs=[row(d), row(ha), row(ha), row(ha),
                  pl.BlockSpec((tm, LANE), lambda i: (i, _LAYOUT.ga // LANE)),
                  row(ha), row(ha), row(3 * d), full(expand),
                  full(wa), full(wb), full(wc), full(wo), full(lng), full(lnb)],
        out_specs=[row(d), row(d)],
        out_shape=[jax.ShapeDtypeStruct((t, d), F32), jax.ShapeDtypeStruct((t, d), BF16)],
        compiler_params=_cparams("parallel"),
        name="merge_norm",
    )(x, oc, os_, ow, proj2d, ob, od, gm, expand, wa, wb, wc, wo, lng, lnb)


def _first_max(v, idx, n):
    mx = jnp.max(v, axis=0, keepdims=True)
    first = jnp.min(jnp.where(v == mx, idx, n), axis=0, keepdims=True)
    return mx, idx == first


def _router_kernel(x_ref, rw_ref, rb_ref, cw_ref):
    gsz = N_EXPERTS // N_GROUPS
    logits = lax.dot_general(rw_ref[...], x_ref[...], (((1,), (1,)), ((), ())),
                             precision=lax.Precision.HIGHEST,
                             preferred_element_type=F32)
    tm = logits.shape[1]
    scores = jax.nn.sigmoid(logits)
    biased = scores + rb_ref[...]
    iw = _iota((gsz, tm), 0)
    gs = []
    for g in range(N_GROUPS):
        blk = biased[g * gsz:(g + 1) * gsz]
        m1, pick = _first_max(blk, iw, gsz)
        m2 = jnp.max(jnp.where(pick, -jnp.inf, blk), axis=0, keepdims=True)
        gs.append(m1 + m2)
    gs = jnp.concatenate(gs, axis=0)
    ig = _iota((N_GROUPS, tm), 0)
    gmask = jnp.zeros((N_GROUPS, tm), F32)
    for _ in range(TOPK_GROUPS):
        _, pick = _first_max(gs, ig, N_GROUPS)
        gmask = jnp.where(pick, 1.0, gmask)
        gs = jnp.where(pick, -jnp.inf, gs)
    emask = jnp.concatenate(
        [jnp.broadcast_to(gmask[g:g + 1], (gsz, tm)) for g in range(N_GROUPS)], axis=0)
    cand = jnp.where(emask > 0.5, biased, NEG)
    ie = _iota((N_EXPERTS, tm), 0)
    chosen = jnp.zeros((N_EXPERTS, tm), F32)
    for _ in range(TOP_K):
        _, pick = _first_max(cand, ie, N_EXPERTS)
        chosen = jnp.where(pick, 1.0, chosen)
        cand = jnp.where(pick, -jnp.inf, cand)
    w = scores * chosen
    cw_ref[...] = w / jnp.sum(w, axis=0, keepdims=True) * ROUTED_SCALE


def _router(x, rw_t, rb):
    t, d = x.shape
    tm = _tile(t, 512)
    return pl.pallas_call(
        _router_kernel,
        grid=(t // tm,),
        in_specs=[pl.BlockSpec((tm, d), lambda i: (i, 0)),
                  pl.BlockSpec((N_EXPERTS, d), lambda i: (0, 0)),
                  pl.BlockSpec((N_EXPERTS, 1), lambda i: (0, 0))],
        out_specs=pl.BlockSpec((N_EXPERTS, tm), lambda i: (0, i)),
        out_shape=jax.ShapeDtypeStruct((N_EXPERTS, t), F32),
        compiler_params=_cparams("parallel"),
        name="router",
    )(x, rw_t, rb)


def _moe_kernel(x_ref, cw_ref, wg_ref, wu_ref, wd_ref, y_ref, *, ne):
    c = pl.program_id(1)
    x = x_ref[...]
    edim = wg_ref.shape[2]
    cw = cw_ref[...]
    cw_hi = cw.astype(BF16)
    cw_lo = (cw - cw_hi.astype(F32)).astype(BF16)
    expand = jnp.concatenate(
        [jnp.where(_iota((N_EXPERTS, edim), 0) == c * ne + e, 1.0, 0.0) for e in range(ne)],
        axis=1).astype(BF16)
    scale = _dot(cw_hi, expand) + _dot(cw_lo, expand)
    hs = []
    for e in range(ne):
        hs.append(jax.nn.silu(_dot(x, wg_ref[e])) * _dot(x, wu_ref[e]))
    h = (jnp.concatenate(hs, axis=1) * scale).astype(BF16)
    y = _dot(h, wd_ref[...].reshape(ne * edim, wd_ref.shape[2]))

    @pl.when(c == 0)
    def _():
        y_ref[...] = y

    @pl.when(c != 0)
    def _():
        y_ref[...] += y


def _moe_dense(xb, cw, wg, wu, wd):
    t, d = xb.shape
    edim = wg.shape[2]
    tm = _tile(t, 512)
    ne = 8
    return pl.pallas_call(
        functools.partial(_moe_kernel, ne=ne),
        grid=(t // tm, N_EXPERTS // ne),
        in_specs=[pl.BlockSpec((tm, d), lambda i, c: (i, 0)),
                  pl.BlockSpec((tm, N_EXPERTS), lambda i, c: (i, 0)),
                  pl.BlockSpec((ne, d, edim), lambda i, c: (c, 0, 0)),
                  pl.BlockSpec((ne, d, edim), lambda i, c: (c, 0, 0)),
                  pl.BlockSpec((ne, edim, d), lambda i, c: (c, 0, 0))],
        out_specs=pl.BlockSpec((tm, d), lambda i, c: (i, 0)),
        out_shape=jax.ShapeDtypeStruct((t, d), F32),
        compiler_params=_cparams("parallel", "arbitrary"),
        name="moe_experts",
    )(xb, cw, wg, wu, wd)


def _tail_kernel(x_ref, xb_ref, y_ref, p_ref, sg_ref, su_ref, sd_ref, pg_ref, pp_ref,
                 lng_ref, lnb_ref, xo_ref, xbo_ref, *, alpha):
    xb = xb_ref[...]
    h = jax.nn.silu(_dot(xb, sg_ref[...])) * _dot(xb, su_ref[...])
    shared = _dot(h.astype(BF16), sd_ref[...])
    ple = jax.nn.sigmoid(_dot(xb, pg_ref[...])) * _dot(p_ref[...].astype(BF16), pp_ref[...])
    z = alpha * x_ref[...] + y_ref[...] + shared + ple
    y = _layer_norm(z, lng_ref[...], lnb_ref[...])
    xo_ref[...] = y
    xbo_ref[...] = y.astype(BF16)


def _tail(x, xb, y, p, sg, su, sd, pg, pp, lng, lnb, alpha):
    t, d = x.shape
    tm = _tile(t, 256)
    row = lambda w: pl.BlockSpec((tm, w), lambda i: (i, 0))
    full = lambda a: pl.BlockSpec(a.shape, lambda i: (0,) * a.ndim)
    return pl.pallas_call(
        functools.partial(_tail_kernel, alpha=alpha),
        grid=(t // tm,),
        in_specs=[row(d), row(d), row(d), row(p.shape[1]), full(sg), full(su), full(sd),
                  full(pg), full(pp), full(lng), full(lnb)],
        out_specs=[row(d), row(d)],
        out_shape=[jax.ShapeDtypeStruct((t, d), F32), jax.ShapeDtypeStruct((t, d), BF16)],
        compiler_params=_cparams("parallel"),
        name="ffn_tail_norm",
    )(x, xb, y, p, sg, su, sd, pg, pp, lng, lnb)


def _compressed_slots(c, pos_cols):
    b, nb, hkv, dh = c.shape
    c = c.transpose(0, 2, 1, 3)
    extra = jnp.zeros((b, hkv, nb, LANE - dh), F32)
    if pos_cols:
        cpos = np.arange(nb) * NSA_BLOCK + NSA_BLOCK - 1
        cols = np.zeros((nb, LANE - dh), np.float32)
        cols[:, 0] = cpos // LANE
        cols[:, 1] = cpos % LANE
        extra = extra + jnp.asarray(cols)
    c = jnp.concatenate([c, extra], axis=-1)
    c = jnp.pad(c, ((0, 0), (0, 0), (0, NBLK_PAD - nb), (0, 0)))
    return c.astype(BF16)


def kernel(x, p, w_in, nsa_pe_k, nsa_w1_k, nsa_w2_k, nsa_pe_v, nsa_w1_v, nsa_w2_v, swa_sinks,
           diff_lq1, diff_lk1, diff_lq2, diff_lk2, diff_norm_g, w_branch_a, w_branch_b, w_branch_c,
           w_out, ln1_g, ln1_b, router_w, router_bias, exp_w_gate, exp_w_up, exp_w_down,
           sh_w_gate, sh_w_up, sh_w_down, ple_w_proj, ple_w_gate, ln2_g, ln2_b):
    bsz, seq, d = x.shape
    depth = w_in.shape[0]
    t = bsz * seq
    alpha = (2 * depth) ** 0.25
    lay = _LAYOUT
    nb = seq // NSA_BLOCK
    assert nb <= NBLK_PAD and seq % LANE == 0
    n_sel = min(NSA_TOPN, nb)

    aux = jnp.asarray(lay.aux)
    col_scale = jnp.asarray(lay.scale)
    src = jnp.asarray(np.where(lay.src < 0, GM_OFF + 3 * d, lay.src))
    blk_mask = jnp.asarray(
        (np.arange(seq)[:, None] // NSA_BLOCK == np.arange(NBLK_PAD)[None, :]) * MASK_NEG, BF16)
    ex = np.zeros((3, LANE, NSA_HEADS * HEAD_DIM), np.float32)
    for c in range(3):
        for h in range(NSA_HEADS):
            ex[c, c * NSA_HEADS + h, h * HEAD_DIM:(h + 1) * HEAD_DIM] = 1.0
    expand = jnp.asarray(ex, BF16)
    slopes8 = jnp.asarray(_slopes(NSA_HEADS), F32)

    xf = x.reshape(t, d)
    xb = xf.astype(BF16)
    for i in range(depth):
        w_ext = jnp.concatenate([w_in[i], jnp.zeros((d, 1), F32)], axis=1)
        w_slots = (jnp.take(w_ext, src, axis=1) * col_scale).astype(BF16)
        proj2d = _project(xb, w_slots, aux, seq, BF16, 1408)
        gm = _project(xb, w_in[i][:, GM_OFF:].astype(BF16), None, seq, F32, 1024)
        proj = proj2d.reshape(bsz, seq, lay.width)

        def blocks(off):
            c = proj[:, :, off:off + LANE].reshape(bsz, nb, NSA_BLOCK, NSA_KV, HEAD_DIM)
            return c.transpose(0, 1, 3, 2, 4).reshape(bsz * nb * NSA_KV, NSA_BLOCK * HEAD_DIM)
        cmp = _compress(
            jnp.stack([blocks(lay.kc), blocks(lay.vc)]),
            jnp.stack([nsa_pe_k[i].reshape(1, -1), nsa_pe_v[i].reshape(1, -1)]),
            jnp.stack([nsa_w1_k[i], nsa_w1_v[i]]).astype(BF16),
            jnp.stack([nsa_w2_k[i], nsa_w2_v[i]]).astype(BF16))
        cmp = cmp.reshape(2, bsz, nb, NSA_KV, HEAD_DIM)
        kca = _compressed_slots(cmp[0], True)
        vca = _compressed_slots(cmp[1], False)

        o_cmp, selc = _nsa_compressed(proj, kca, vca, seq, n_sel)
        ks = proj[:, :, lay.ks:lay.ks + NSA_KV * LANE].reshape(bsz, seq, NSA_KV, LANE).transpose(0, 2, 1, 3)
        ksel = jnp.concatenate(
            [ks, jnp.broadcast_to(blk_mask[None, None], (bsz, NSA_KV, seq, NBLK_PAD))], axis=-1)
        o_sel = _nsa_selected(proj, selc, ksel, seq)
        hp_a = jnp.stack([slopes8, jnp.zeros_like(slopes8)])
        o_win = _banded(proj, hp_a, seq, NSA_WINDOW, lay.qa, lay.kw, lay.vw, False, F32, "nsa_window")
        hp_b = jnp.stack([slopes8, swa_sinks[i].astype(F32)])
        o_b = _banded(proj, hp_b, seq, SWA_WINDOW, lay.qb, lay.kb, lay.vb, True, BF16, "swa_sinks")

        lambda_init = 0.8 - 0.6 * math.exp(-0.3 * i)
        lam_vecs = jnp.stack([diff_lq1[i], diff_lk1[i], diff_lq2[i], diff_lk2[i]]).astype(F32)
        o_d = _diff_attention(proj, lam_vecs, diff_norm_g[i].reshape(1, -1), seq, lambda_init)

        ha = NSA_HEADS * HEAD_DIM
        xf, xb = _merge(
            xf, o_cmp.reshape(t, ha), o_sel.reshape(t, ha), o_win.reshape(t, ha), proj2d,
            o_b.reshape(t, ha), o_d.reshape(t, ha), gm, expand,
            w_branch_a[i].astype(BF16), w_branch_b[i].astype(BF16), w_branch_c[i].astype(BF16),
            w_out[i].astype(BF16), ln1_g[i].reshape(1, d), ln1_b[i].reshape(1, d), alpha)

        cw_t = _router(xf, router_w[i].T, router_bias[i].reshape(-1, 1))
        y = _moe_dense(xb, cw_t.T, exp_w_gate[i].astype(BF16), exp_w_up[i].astype(BF16),
                       exp_w_down[i].astype(BF16))
        xf, xb = _tail(xf, xb, y, p[i].reshape(t, -1),
                       sh_w_gate[i].astype(BF16), sh_w_up[i].astype(BF16), sh_w_down[i].astype(BF16),
                       ple_w_gate[i].astype(BF16), ple_w_proj[i].astype(BF16),
                       ln2_g[i].reshape(1, d), ln2_b[i].reshape(1, d), alpha)
    return xf.reshape(bsz, seq, d)
```

```python
import functools
import math

import numpy as np
import jax
import jax.numpy as jnp
from jax import lax
from jax.experimental import pallas as pl
from jax.experimental.pallas import tpu as pltpu

F32 = jnp.float32
BF16 = jnp.bfloat16

HEAD_DIM = 64
NSA_HEADS = 8
NSA_KV = 2
NSA_BLOCK = 64
NSA_TOPN = 16
NSA_LOCAL = 2
NSA_WINDOW = 512
SWA_HEADS = 8
SWA_KV = 2
SWA_WINDOW = 128
DIFF_HEADS = 4
N_EXPERTS = 64
TOP_K = 8
N_GROUPS = 8
TOPK_GROUPS = 4
ROUTED_SCALE = 2.5
LN_EPS = 1e-5
NEG = -1e30
FORCE = 1e4
MASK_NEG = -(2.0 ** 100)

LANE = 128
GROUP = NSA_HEADS // NSA_KV
NBLK_PAD = 128
VMEM_LIMIT = 56 * 1024 * 1024

IN_SIZES = (NSA_HEADS * HEAD_DIM,) + (NSA_KV * HEAD_DIM,) * 6 + (NSA_HEADS * 3,) + \
    (SWA_HEADS * HEAD_DIM, SWA_KV * HEAD_DIM, SWA_KV * HEAD_DIM) + \
    (DIFF_HEADS * 2 * HEAD_DIM,) * 3
IN_OFFS = np.concatenate([[0], np.cumsum(IN_SIZES)]).tolist()
GM_OFF = IN_OFFS[-1]


def _slopes(n):
    return [2.0 ** (-8.0 * (h + 1) / n) for h in range(n)]


class _Layout:
    def __init__(self):
        src, scale, bias, pa, pb = [], [], [], [], []

        def slot(cols, sc=1.0, consts=(), pos=False, width=LANE):
            s = [-1] * width
            c = [0.0] * width
            a = [0.0] * width
            b = [0.0] * width
            s[:len(cols)] = cols
            for off, val in consts:
                c[off] = val
            if pos:
                a[HEAD_DIM] = 1.0
                b[HEAD_DIM + 1] = 1.0
            start = len(src)
            src.extend(s)
            scale.extend([sc] * width)
            bias.extend(c)
            pa.extend(a)
            pb.extend(b)
            return start

        def rng(base, n):
            return list(range(base, base + n))

        qs = HEAD_DIM ** -0.5
        o = IN_OFFS
        sl8 = _slopes(NSA_HEADS)
        sl4 = _slopes(DIFF_HEADS)
        self.qa = len(src)
        for h in range(NSA_HEADS):
            slot(rng(o[0] + h * HEAD_DIM, HEAD_DIM), qs,
                 [(HEAD_DIM, sl8[h] * LANE), (HEAD_DIM + 1, sl8[h])])
        self.qb = len(src)
        for h in range(SWA_HEADS):
            slot(rng(o[8] + h * HEAD_DIM, HEAD_DIM), qs,
                 [(HEAD_DIM, sl8[h] * LANE), (HEAD_DIM + 1, sl8[h])])
        self.qd = len(src)
        for h in range(DIFF_HEADS):
            for r in range(2):
                slot(rng(o[11] + (h * 2 + r) * HEAD_DIM, HEAD_DIM), qs,
                     [(HEAD_DIM, sl4[h] * LANE), (HEAD_DIM + 1, sl4[h])])
        self.kd = len(src)
        for h in range(DIFF_HEADS):
            for r in range(2):
                slot(rng(o[12] + (h * 2 + r) * HEAD_DIM, HEAD_DIM), pos=True)
        self.vd = len(src)
        for h in range(DIFF_HEADS):
            slot(rng(o[13] + h * 2 * HEAD_DIM, 2 * HEAD_DIM),
                 consts=[(2 * HEAD_DIM, 1.0)], width=2 * LANE)

        def kv_slots(k_src, v_src):
            k_off = len(src)
            for k in range(NSA_KV):
                slot(rng(k_src + k * HEAD_DIM, HEAD_DIM), pos=True)
            v_off = len(src)
            for k in range(NSA_KV):
                slot(rng(v_src + k * HEAD_DIM, HEAD_DIM), consts=[(HEAD_DIM, 1.0)])
            return k_off, v_off

        self.ks, self.vs = kv_slots(o[3], o[4])
        self.kw, self.vw = kv_slots(o[5], o[6])
        self.kb, self.vb = kv_slots(o[9], o[10])
        self.kc = slot(rng(o[1], NSA_KV * HEAD_DIM))
        self.vc = slot(rng(o[2], NSA_KV * HEAD_DIM))
        self.ga = slot([o[7] + h * 3 + c for c in range(3) for h in range(NSA_HEADS)])
        self.width = len(src)
        self.src = np.asarray(src, np.int32)
        self.scale = np.asarray(scale, np.float32)
        aux = np.zeros((8, self.width), np.float32)
        aux[0] = bias
        aux[1] = pa
        aux[2] = pb
        self.aux = aux


_LAYOUT = _Layout()


def _cparams(*sem):
    return pltpu.CompilerParams(dimension_semantics=sem, vmem_limit_bytes=VMEM_LIMIT)


def _tile(n, pref):
    t = min(n, pref)
    assert n % t == 0, (n, pref)
    return t


def _iota(shape, dim):
    return lax.broadcasted_iota(jnp.int32, shape, dim)


def _dot(a, b):
    return jnp.dot(a, b, preferred_element_type=F32)


def _dot_nt(a, b):
    return lax.dot_general(a, b, (((1,), (1,)), ((), ())), preferred_element_type=F32)


def _proj_kernel(x_ref, w_ref, aux_ref, o_ref, *, seq, tm):
    acc = _dot(x_ref[...], w_ref[...])
    pos = (pl.program_id(0) * tm) % seq + _iota((tm, 1), 0)
    a = (pos >> 7).astype(F32)
    b = (pos & (LANE - 1)).astype(F32)
    aux = aux_ref[...]
    o_ref[...] = (acc + aux[0:1] + a * aux[1:2] + b * aux[2:3]).astype(o_ref.dtype)


def _matmul_kernel(x_ref, w_ref, o_ref):
    o_ref[...] = _dot(x_ref[...], w_ref[...]).astype(o_ref.dtype)


def _project(xb, w, aux, seq, out_dtype, tn_pref):
    t, d = xb.shape
    n = w.shape[1]
    tm = _tile(t, 1024)
    tn = _tile(n, tn_pref)
    in_specs = [pl.BlockSpec((tm, d), lambda i, j: (i, 0)),
                pl.BlockSpec((d, tn), lambda i, j: (0, j))]
    args = [xb, w]
    if aux is None:
        body = _matmul_kernel
    else:
        body = functools.partial(_proj_kernel, seq=seq, tm=tm)
        in_specs.append(pl.BlockSpec((8, tn), lambda i, j: (0, j)))
        args.append(aux)
    return pl.pallas_call(
        body,
        grid=(t // tm, n // tn),
        in_specs=in_specs,
        out_specs=pl.BlockSpec((tm, tn), lambda i, j: (i, j)),
        out_shape=jax.ShapeDtypeStruct((t, n), out_dtype),
        compiler_params=_cparams("parallel", "parallel"),
        name="proj" if aux is not None else "gate_proj",
    )(*args)


def _compress_kernel(x_ref, pe_ref, w1_ref, w2_ref, o_ref):
    xb = (x_ref[0].astype(F32) + pe_ref[0]).astype(BF16)
    h = jax.nn.gelu(_dot(xb, w1_ref[0]))
    o_ref[0] = _dot(h.astype(BF16), w2_ref[0])


def _compress(xs, pes, w1s, w2s):
    _, r, kdim = xs.shape
    hid = w1s.shape[2]
    dh = w2s.shape[2]
    return pl.pallas_call(
        _compress_kernel,
        grid=(2,),
        in_specs=[pl.BlockSpec((1, r, kdim), lambda i: (i, 0, 0)),
                  pl.BlockSpec((1, 1, kdim), lambda i: (i, 0, 0)),
                  pl.BlockSpec((1, kdim, hid), lambda i: (i, 0, 0)),
                  pl.BlockSpec((1, hid, dh), lambda i: (i, 0, 0))],
        out_specs=pl.BlockSpec((1, r, dh), lambda i: (i, 0, 0)),
        out_shape=jax.ShapeDtypeStruct((2, r, dh), F32),
        compiler_params=_cparams("parallel"),
        name="nsa_compress",
    )(xs, pes, w1s, w2s)


def _stack_heads(q, n):
    return jnp.concatenate([q[:, g * LANE:(g + 1) * LANE] for g in range(n)], axis=0)


def _unstack_heads(o, n, tq, width):
    return jnp.concatenate([o[g * tq:(g + 1) * tq, :width] for g in range(n)], axis=1)


def _cmp_kernel(q_ref, kc_ref, vc_ref, oc_ref, sel_ref, touch_ref, *, tq, n_sel):
    i = pl.program_id(2)
    qs = _stack_heads(q_ref[0], GROUP)
    s = _dot_nt(qs, kc_ref[0, 0])
    t1 = i * tq + _iota((tq, 1), 0)
    t = jnp.concatenate([t1] * GROUP, axis=0)
    n = _iota((1, NBLK_PAD), 1)
    valid = (n * NSA_BLOCK + (NSA_BLOCK - 1)) <= t
    s = jnp.where(valid, s, NEG)
    mx = jnp.max(s, axis=-1, keepdims=True)
    e = jnp.where(valid, jnp.exp(s - mx), 0.0)
    p = e / jnp.maximum(jnp.sum(e, axis=-1, keepdims=True), 1e-30)
    o = _dot(p.astype(BF16), vc_ref[0, 0])
    oc_ref[0] = _unstack_heads(o, GROUP, tq, HEAD_DIM)

    imp = p[0:tq]
    for g in range(1, GROUP):
        imp = imp + p[g * tq:(g + 1) * tq]
    rel = (t1 >> 6) - n
    forced = (n == 0) | ((rel >= 0) & (rel < NSA_LOCAL))
    val = jnp.where(forced, FORCE, jnp.where(rel >= 0, imp, -1.0))
    sel = jnp.zeros((tq, NBLK_PAD), F32)
    nf = n.astype(F32)
    for _ in range(n_sel):
        mx = jnp.max(val, axis=-1, keepdims=True)
        idx = jnp.min(jnp.where(val == mx, nf, float(NBLK_PAD)), axis=-1, keepdims=True)
        pick = nf == idx
        sel = jnp.where(pick & (mx >= 0.0), 1.0, sel)
        val = jnp.where(pick, -jnp.inf, val)
    sel_ref[0, 0] = (1.0 - sel).astype(BF16)
    touch_ref[0, 0, 0] = jnp.broadcast_to(jnp.max(sel, axis=0, keepdims=True), (8, NBLK_PAD))


def _nsa_compressed(proj, kca, vca, seq, n_sel):
    b = proj.shape[0]
    tq = _tile(seq, 256)
    lay = _LAYOUT
    gw = GROUP * LANE
    return pl.pallas_call(
        functools.partial(_cmp_kernel, tq=tq, n_sel=n_sel),
        grid=(b, NSA_KV, seq // tq),
        in_specs=[pl.BlockSpec((1, tq, gw), lambda b_, k, i: (b_, i, lay.qa // gw + k)),
                  pl.BlockSpec((1, 1, NBLK_PAD, LANE), lambda b_, k, i: (b_, k, 0, 0)),
                  pl.BlockSpec((1, 1, NBLK_PAD, LANE), lambda b_, k, i: (b_, k, 0, 0))],
        out_specs=[pl.BlockSpec((1, tq, GROUP * HEAD_DIM), lambda b_, k, i: (b_, i, k)),
                   pl.BlockSpec((1, 1, tq, NBLK_PAD), lambda b_, k, i: (b_, k, i, 0)),
                   pl.BlockSpec((1, 1, 1, 8, NBLK_PAD), lambda b_, k, i: (b_, k, i, 0, 0))],
        out_shape=[jax.ShapeDtypeStruct((b, seq, NSA_HEADS * HEAD_DIM), F32),
                   jax.ShapeDtypeStruct((b, NSA_KV, seq, NBLK_PAD), BF16),
                   jax.ShapeDtypeStruct((b, NSA_KV, seq // tq, 8, NBLK_PAD), F32)],
        compiler_params=_cparams("parallel", "parallel", "parallel"),
        name="nsa_compressed",
    )(proj, kca, vca)


def _flash(chunk_fns, v_ref, i, tq, tk, m_ref, acc_ref, n_active=None, tile_at=None):
    m_ref[...] = jnp.full(m_ref.shape, NEG, F32)
    acc_ref[...] = jnp.zeros(acc_ref.shape, F32)
    acc_reps = acc_ref.shape[1] // LANE
    n_full = (i * tq) // tk
    if n_active is None:
        n_active = n_full
        tile_at = lambda n: n

    def nth_tile(n):
        return jnp.where(n < n_active, tile_at(n), n_full)

    def scores(j):
        start = pl.multiple_of(j * tk, tk)
        return tuple(fn(start) for fn in chunk_fns)

    def accumulate(j, ss, visible):
        start = pl.multiple_of(j * tk, tk)
        vt = v_ref[pl.ds(start, tk), :]
        for c, s in enumerate(ss):
            rows = slice(c * tq, (c + 1) * tq)
            if visible is not None:
                s = jnp.where(visible, s, NEG)
            m_old = m_ref[rows, :]
            m_new = jnp.maximum(m_old, jnp.max(s, axis=-1, keepdims=True))
            p = jnp.exp((s - jnp.concatenate([m_new] * (tk // LANE), axis=1)).astype(BF16))
            alpha = jnp.concatenate([jnp.exp(m_old - m_new)] * acc_reps, axis=1)
            acc_ref[rows, :] = alpha * acc_ref[rows, :] + _dot(p, vt)
            m_ref[rows, :] = m_new

    def body(n, ss):
        nxt = scores(nth_tile(n + 1))
        accumulate(nth_tile(n), ss, None)
        return nxt

    ss = lax.fori_loop(0, n_active, body, scores(nth_tile(0)))
    visible = (_iota((tq, tk), 1) - _iota((tq, tk), 0)) <= i * tq - n_full * tk
    accumulate(n_full, ss, visible)


def _sel_kernel(order_ref, count_ref, q_ref, sel_ref, k_ref, v_ref, o_ref, qa_ref, m_ref, acc_ref,
                *, tq, tk, n_tiles):
    i = pl.program_id(2)
    tile_id = (pl.program_id(0) * pl.num_programs(1) + pl.program_id(1)) * pl.num_programs(2) + i
    q = q_ref[0]
    sc = sel_ref[0, 0]
    for g in range(GROUP):
        qa_ref[g * tq:(g + 1) * tq, 0:LANE] = q[:, g * LANE:(g + 1) * LANE]
        qa_ref[g * tq:(g + 1) * tq, LANE:2 * LANE] = sc
    kv = k_ref.at[0, 0]

    def head_scores(g):
        return lambda start: _dot_nt(qa_ref[g * tq:(g + 1) * tq, :], kv[pl.ds(start, tk), :])

    _flash([head_scores(g) for g in range(GROUP)], v_ref.at[0], i, tq, tk, m_ref, acc_ref,
           n_active=count_ref[tile_id], tile_at=lambda n: order_ref[tile_id * n_tiles + n])
    acc = acc_ref[...]
    o = acc / acc[:, HEAD_DIM:HEAD_DIM + 1]
    o_ref[0] = _unstack_heads(o, GROUP, tq, HEAD_DIM)


def _nsa_selected(proj, selc, touch, ksel, seq):
    b = proj.shape[0]
    tq = _tile(seq, 256)
    tk = _tile(seq, 512)
    n_tiles = seq // tk
    lay = _LAYOUT
    gw = GROUP * LANE
    per_tile = touch[:, :, :, 0, :seq // NSA_BLOCK].reshape(b, NSA_KV, seq // tq, n_tiles, tk // NSA_BLOCK)
    n_full = (np.arange(seq // tq) * tq) // tk
    active = (jnp.max(per_tile, axis=-1) > 0.0) & jnp.asarray(np.arange(n_tiles)[None, :] < n_full[:, None])
    order = jnp.argsort(jnp.where(active, 0, 1), axis=-1, stable=True).astype(jnp.int32).reshape(-1)
    count = jnp.sum(active, axis=-1).astype(jnp.int32).reshape(-1)
    grid_spec = pltpu.PrefetchScalarGridSpec(
        num_scalar_prefetch=2,
        grid=(b, NSA_KV, seq // tq),
        in_specs=[pl.BlockSpec((1, tq, gw), lambda b_, k, i, o, c: (b_, i, lay.qa // gw + k)),
                  pl.BlockSpec((1, 1, tq, NBLK_PAD), lambda b_, k, i, o, c: (b_, k, i, 0)),
                  pl.BlockSpec((1, 1, seq, 2 * LANE), lambda b_, k, i, o, c: (b_, k, 0, 0)),
                  pl.BlockSpec((1, seq, LANE), lambda b_, k, i, o, c: (b_, 0, lay.vs // LANE + k))],
        out_specs=pl.BlockSpec((1, tq, GROUP * HEAD_DIM), lambda b_, k, i, o, c: (b_, i, k)),
        scratch_shapes=[pltpu.VMEM((GROUP * tq, 2 * LANE), BF16),
                        pltpu.VMEM((GROUP * tq, LANE), F32),
                        pltpu.VMEM((GROUP * tq, LANE), F32)])
    return pl.pallas_call(
        functools.partial(_sel_kernel, tq=tq, tk=tk, n_tiles=n_tiles),
        grid_spec=grid_spec,
        out_shape=jax.ShapeDtypeStruct((b, seq, NSA_HEADS * HEAD_DIM), F32),
        compiler_params=_cparams("parallel", "parallel", "arbitrary"),
        name="nsa_selected",
    )(order, count, proj, selc, ksel, proj)


def _diff_kernel(lam_ref, q_ref, k_ref, v_ref, g_ref, o_ref, m_ref, acc_ref, *, tq, tk, lambda_init):
    i = pl.program_id(2)
    kv = k_ref.at[0]

    def map_scores(r):
        cols = slice(r * LANE, (r + 1) * LANE)
        return lambda start: _dot_nt(q_ref[0, :, cols], kv[pl.ds(start, tk), cols])

    _flash([map_scores(0), map_scores(1)], v_ref.at[0], i, tq, tk, m_ref, acc_ref)
    acc = acc_ref[...]
    dv = 2 * HEAD_DIM
    a = acc[:, :dv] / acc[:, dv:dv + 1]
    lv = lam_ref[...]
    lam = (jnp.exp(jnp.sum(lv[0:1] * lv[1:2], axis=-1, keepdims=True))
           - jnp.exp(jnp.sum(lv[2:3] * lv[3:4], axis=-1, keepdims=True)) + lambda_init)
    o = a[0:tq] - lam * a[tq:2 * tq]
    o = o * lax.rsqrt(jnp.mean(o * o, axis=-1, keepdims=True) + LN_EPS) * g_ref[...]
    o_ref[0] = (o * (1.0 - lambda_init)).astype(o_ref.dtype)


def _diff_attention(proj, lam_vecs, norm_g, seq, lambda_init):
    b = proj.shape[0]
    tq = _tile(seq, 512)
    tk = _tile(seq, 512)
    lay = _LAYOUT
    w2 = 2 * LANE
    return pl.pallas_call(
        functools.partial(_diff_kernel, tq=tq, tk=tk, lambda_init=lambda_init),
        grid=(b, DIFF_HEADS, seq // tq),
        in_specs=[pl.BlockSpec((4, HEAD_DIM), lambda b_, h, i: (0, 0)),
                  pl.BlockSpec((1, tq, w2), lambda b_, h, i: (b_, i, lay.qd // w2 + h)),
                  pl.BlockSpec((1, seq, w2), lambda b_, h, i: (b_, 0, lay.kd // w2 + h)),
                  pl.BlockSpec((1, seq, w2), lambda b_, h, i: (b_, 0, lay.vd // w2 + h)),
                  pl.BlockSpec((1, 2 * HEAD_DIM), lambda b_, h, i: (0, 0))],
        out_specs=pl.BlockSpec((1, tq, 2 * HEAD_DIM), lambda b_, h, i: (b_, i, h)),
        out_shape=jax.ShapeDtypeStruct((b, seq, DIFF_HEADS * 2 * HEAD_DIM), BF16),
        scratch_shapes=[pltpu.VMEM((2 * tq, LANE), F32),
                        pltpu.VMEM((2 * tq, w2), F32)],
        compiler_params=_cparams("parallel", "parallel", "arbitrary"),
        name="diff_attention",
    )(lam_vecs, proj, proj, proj, norm_g)


def _band_kernel(hp_ref, q_ref, k_ref, v_ref, o_ref, *, tq, window, band, use_sinks):
    kvh = pl.program_id(1)
    i = pl.program_id(2)
    start = pl.multiple_of(jnp.maximum(i * tq - window, 0), LANE)
    kb = k_ref[0, pl.ds(start, band), :]
    vb = v_ref[0, pl.ds(start, band), :]
    t1 = i * tq + _iota((tq, 1), 0)
    dist = t1 - (start + _iota((1, band), 1))
    visible = (dist >= 0) & (dist < window)
    outs = []
    for g in range(GROUP):
        s = jnp.where(visible, _dot_nt(q_ref[0, :, g * LANE:(g + 1) * LANE], kb), NEG)
        mx = jnp.max(s, axis=-1, keepdims=True)
        if use_sinks:
            h = kvh * GROUP + g
            sink = hp_ref[1, h] + hp_ref[0, h] * t1.astype(F32)
            mx = jnp.maximum(mx, sink)
        acc = _dot(jnp.exp((s - mx).astype(BF16)), vb)
        den = acc[:, HEAD_DIM:HEAD_DIM + 1]
        if use_sinks:
            den = den + jnp.exp(sink - mx)
        outs.append((acc / den)[:, :HEAD_DIM])
    o_ref[0] = jnp.concatenate(outs, axis=1).astype(o_ref.dtype)


def _banded(proj, head_params, seq, window, q_off, k_off, v_off, use_sinks, out_dtype, name):
    b = proj.shape[0]
    tq = _tile(seq, 256)
    band = min(window + tq, seq)
    gw = GROUP * LANE
    return pl.pallas_call(
        functools.partial(_band_kernel, tq=tq, window=window, band=band, use_sinks=use_sinks),
        grid=(b, NSA_KV, seq // tq),
        in_specs=[pl.BlockSpec(memory_space=pltpu.SMEM),
                  pl.BlockSpec((1, tq, gw), lambda b_, k, i: (b_, i, q_off // gw + k)),
                  pl.BlockSpec((1, seq, LANE), lambda b_, k, i: (b_, 0, k_off // LANE + k)),
                  pl.BlockSpec((1, seq, LANE), lambda b_, k, i: (b_, 0, v_off // LANE + k))],
        out_specs=pl.BlockSpec((1, tq, GROUP * HEAD_DIM), lambda b_, k, i: (b_, i, k)),
        out_shape=jax.ShapeDtypeStruct((b, seq, NSA_HEADS * HEAD_DIM), out_dtype),
        compiler_params=_cparams("parallel", "parallel", "parallel"),
        name=name,
    )(head_params, proj, proj, proj)


def _layer_norm(z, g, b):
    mu = jnp.mean(z, axis=-1, keepdims=True)
    zc = z - mu
    var = jnp.mean(zc * zc, axis=-1, keepdims=True)
    return zc * lax.rsqrt(var + LN_EPS) * g + b


def _merge_kernel(x_ref, oc_ref, os_ref, ow_ref, ga_ref, ob_ref, od_ref, gm_ref, ex_ref,
                  wa_ref, wb_ref, wc_ref, wo_ref, lng_ref, lnb_ref, xo_ref, xb_ref, *, alpha):
    d = x_ref.shape[1]
    ga = jax.nn.sigmoid(ga_ref[...].astype(F32)).astype(BF16)
    oa = (_dot(ga, ex_ref[0]) * oc_ref[...] + _dot(ga, ex_ref[1]) * os_ref[...]
          + _dot(ga, ex_ref[2]) * ow_ref[...])
    gm = jax.nn.sigmoid(gm_ref[...])
    merged = (gm[:, 0:d] * _dot(oa.astype(BF16), wa_ref[...])
              + gm[:, d:2 * d] * _dot(ob_ref[...], wb_ref[...])
              + gm[:, 2 * d:3 * d] * _dot(od_ref[...], wc_ref[...]))
    z = alpha * x_ref[...] + _dot(merged.astype(BF16), wo_ref[...])
    y = _layer_norm(z, lng_ref[...], lnb_ref[...])
    xo_ref[...] = y
    xb_ref[...] = y.astype(BF16)


def _merge(x, oc, os_, ow, proj2d, ob, od, gm, expand, wa, wb, wc, wo, lng, lnb, alpha):
    t, d = x.shape
    tm = _tile(t, 256)
    ha = NSA_HEADS * HEAD_DIM
    row = lambda w: pl.BlockSpec((tm, w), lambda i: (i, 0))
    full = lambda a: pl.BlockSpec(a.shape, lambda i: (0,) * a.ndim)
    return pl.pallas_call(
        functools.partial(_merge_kernel, alpha=alpha),
        grid=(t // tm,),
        in_specs=[row(d), row(ha), row(ha), row(ha),
                  pl.BlockSpec((tm, LANE), lambda i: (i, _LAYOUT.ga // LANE)),
                  row(ha), row(ha), row(3 * d), full(expand),
                  full(wa), full(wb), full(wc), full(wo), full(lng), full(lnb)],
        out_specs=[row(d), row(d)],
        out_shape=[jax.ShapeDtypeStruct((t, d), F32), jax.ShapeDtypeStruct((t, d), BF16)],
        compiler_params=_cparams("parallel"),
        name="merge_norm",
    )(x, oc, os_, ow, proj2d, ob, od, gm, expand, wa, wb, wc, wo, lng, lnb)


def _first_max(v, idx, n):
    mx = jnp.max(v, axis=0, keepdims=True)
    first = jnp.min(jnp.where(v == mx, idx, n), axis=0, keepdims=True)
    return mx, idx == first


def _router_kernel(x_ref, rw_ref, rb_ref, cw_ref):
    gsz = N_EXPERTS // N_GROUPS
    logits = lax.dot_general(rw_ref[...], x_ref[...], (((1,), (1,)), ((), ())),
                             precision=lax.Precision.HIGHEST,
                             preferred_element_type=F32)
    tm = logits.shape[1]
    scores = jax.nn.sigmoid(logits)
    biased = scores + rb_ref[...]
    iw = _iota((gsz, tm), 0)
    gs = []
    for g in range(N_GROUPS):
        blk = biased[g * gsz:(g + 1) * gsz]
        m1, pick = _first_max(blk, iw, gsz)
        m2 = jnp.max(jnp.where(pick, -jnp.inf, blk), axis=0, keepdims=True)
        gs.append(m1 + m2)
    gs = jnp.concatenate(gs, axis=0)
    ig = _iota((N_GROUPS, tm), 0)
    gmask = jnp.zeros((N_GROUPS, tm), F32)
    for _ in range(TOPK_GROUPS):
        _, pick = _first_max(gs, ig, N_GROUPS)
        gmask = jnp.where(pick, 1.0, gmask)
        gs = jnp.where(pick, -jnp.inf, gs)
    emask = jnp.concatenate(
        [jnp.broadcast_to(gmask[g:g + 1], (gsz, tm)) for g in range(N_GROUPS)], axis=0)
    cand = jnp.where(emask > 0.5, biased, NEG)
    ie = _iota((N_EXPERTS, tm), 0)
    chosen = jnp.zeros((N_EXPERTS, tm), F32)
    for _ in range(TOP_K):
        _, pick = _first_max(cand, ie, N_EXPERTS)
        chosen = jnp.where(pick, 1.0, chosen)
        cand = jnp.where(pick, -jnp.inf, cand)
    w = scores * chosen
    cw_ref[...] = w / jnp.sum(w, axis=0, keepdims=True) * ROUTED_SCALE


def _router(x, rw_t, rb):
    t, d = x.shape
    tm = _tile(t, 512)
    return pl.pallas_call(
        _router_kernel,
        grid=(t // tm,),
        in_specs=[pl.BlockSpec((tm, d), lambda i: (i, 0)),
                  pl.BlockSpec((N_EXPERTS, d), lambda i: (0, 0)),
                  pl.BlockSpec((N_EXPERTS, 1), lambda i: (0, 0))],
        out_specs=pl.BlockSpec((N_EXPERTS, tm), lambda i: (0, i)),
        out_shape=jax.ShapeDtypeStruct((N_EXPERTS, t), F32),
        compiler_params=_cparams("parallel"),
        name="router",
    )(x, rw_t, rb)


def _moe_kernel(x_ref, cw_ref, wg_ref, wu_ref, wd_ref, y_ref, *, ne):
    c = pl.program_id(1)
    x = x_ref[...]
    edim = wg_ref.shape[2]
    cw = cw_ref[...]
    cw_hi = cw.astype(BF16)
    cw_lo = (cw - cw_hi.astype(F32)).astype(BF16)
    expand = jnp.concatenate(
        [jnp.where(_iota((N_EXPERTS, edim), 0) == c * ne + e, 1.0, 0.0) for e in range(ne)],
        axis=1).astype(BF16)
    scale = _dot(cw_hi, expand) + _dot(cw_lo, expand)
    hs = []
    for e in range(ne):
        hs.append(jax.nn.silu(_dot(x, wg_ref[e])) * _dot(x, wu_ref[e]))
    h = (jnp.concatenate(hs, axis=1) * scale).astype(BF16)
    y = _dot(h, wd_ref[...].reshape(ne * edim, wd_ref.shape[2]))

    @pl.when(c == 0)
    def _():
        y_ref[...] = y

    @pl.when(c != 0)
    def _():
        y_ref[...] += y


def _moe_dense(xb, cw, wg, wu, wd):
    t, d = xb.shape
    edim = wg.shape[2]
    tm = _tile(t, 512)
    ne = 8
    return pl.pallas_call(
        functools.partial(_moe_kernel, ne=ne),
        grid=(t // tm, N_EXPERTS // ne),
        in_specs=[pl.BlockSpec((tm, d), lambda i, c: (i, 0)),
                  pl.BlockSpec((tm, N_EXPERTS), lambda i, c: (i, 0)),
                  pl.BlockSpec((ne, d, edim), lambda i, c: (c, 0, 0)),
                  pl.BlockSpec((ne, d, edim), lambda i, c: (c, 0, 0)),
                  pl.BlockSpec((ne, edim, d), lambda i, c: (c, 0, 0))],
        out_specs=pl.BlockSpec((tm, d), lambda i, c: (i, 0)),
        out_shape=jax.ShapeDtypeStruct((t, d), F32),
        compiler_params=_cparams("parallel", "arbitrary"),
        name="moe_experts",
    )(xb, cw, wg, wu, wd)


def _tail_kernel(x_ref, xb_ref, y_ref, p_ref, sg_ref, su_ref, sd_ref, pg_ref, pp_ref,
                 lng_ref, lnb_ref, xo_ref, xbo_ref, *, alpha):
    xb = xb_ref[...]
    h = jax.nn.silu(_dot(xb, sg_ref[...])) * _dot(xb, su_ref[...])
    shared = _dot(h.astype(BF16), sd_ref[...])
    ple = jax.nn.sigmoid(_dot(xb, pg_ref[...])) * _dot(p_ref[...].astype(BF16), pp_ref[...])
    z = alpha * x_ref[...] + y_ref[...] + shared + ple
    y = _layer_norm(z, lng_ref[...], lnb_ref[...])
    xo_ref[...] = y
    xbo_ref[...] = y.astype(BF16)


def _tail(x, xb, y, p, sg, su, sd, pg, pp, lng, lnb, alpha):
    t, d = x.shape
    tm = _tile(t, 256)
    row = lambda w: pl.BlockSpec((tm, w), lambda i: (i, 0))
    full = lambda a: pl.BlockSpec(a.shape, lambda i: (0,) * a.ndim)
    return pl.pallas_call(
        functools.partial(_tail_kernel, alpha=alpha),
        grid=(t // tm,),
        in_specs=[row(d), row(d), row(d), row(p.shape[1]), full(sg), full(su), full(sd),
                  full(pg), full(pp), full(lng), full(lnb)],
        out_specs=[row(d), row(d)],
        out_shape=[jax.ShapeDtypeStruct((t, d), F32), jax.ShapeDtypeStruct((t, d), BF16)],
        compiler_params=_cparams("parallel"),
        name="ffn_tail_norm",
    )(x, xb, y, p, sg, su, sd, pg, pp, lng, lnb)


def _slot_weights(w):
    d = w.shape[0]
    o = IN_OFFS
    qs = HEAD_DIM ** -0.5

    def heads(off, n, width=HEAD_DIM, slot=LANE, scale=1.0):
        blk = w[:, off:off + n * width].reshape(d, n, width) * scale
        return jnp.pad(blk, ((0, 0), (0, 0), (0, slot - width))).reshape(d, n * slot)

    gates = w[:, o[7]:o[8]].reshape(d, NSA_HEADS, 3).transpose(0, 2, 1).reshape(d, 3 * NSA_HEADS)
    parts = [heads(o[0], NSA_HEADS, scale=qs), heads(o[8], SWA_HEADS, scale=qs),
             heads(o[11], 2 * DIFF_HEADS, scale=qs), heads(o[12], 2 * DIFF_HEADS),
             heads(o[13], DIFF_HEADS, 2 * HEAD_DIM, 2 * LANE),
             heads(o[3], NSA_KV), heads(o[4], NSA_KV), heads(o[5], NSA_KV), heads(o[6], NSA_KV),
             heads(o[9], SWA_KV), heads(o[10], SWA_KV),
             w[:, o[1]:o[2]], w[:, o[2]:o[3]],
             jnp.pad(gates, ((0, 0), (0, LANE - 3 * NSA_HEADS)))]
    out = jnp.concatenate(parts, axis=1)
    assert out.shape[1] == _LAYOUT.width
    return out.astype(BF16)


def _compressed_slots(c, pos_cols):
    b, nb, hkv, dh = c.shape
    c = c.transpose(0, 2, 1, 3)
    extra = jnp.zeros((b, hkv, nb, LANE - dh), F32)
    if pos_cols:
        cpos = np.arange(nb) * NSA_BLOCK + NSA_BLOCK - 1
        cols = np.zeros((nb, LANE - dh), np.float32)
        cols[:, 0] = cpos // LANE
        cols[:, 1] = cpos % LANE
        extra = extra + jnp.asarray(cols)
    c = jnp.concatenate([c, extra], axis=-1)
    c = jnp.pad(c, ((0, 0), (0, 0), (0, NBLK_PAD - nb), (0, 0)))
    return c.astype(BF16)


def kernel(x, p, w_in, nsa_pe_k, nsa_w1_k, nsa_w2_k, nsa_pe_v, nsa_w1_v, nsa_w2_v, swa_sinks,
           diff_lq1, diff_lk1, diff_lq2, diff_lk2, diff_norm_g, w_branch_a, w_branch_b, w_branch_c,
           w_out, ln1_g, ln1_b, router_w, router_bias, exp_w_gate, exp_w_up, exp_w_down,
           sh_w_gate, sh_w_up, sh_w_down, ple_w_proj, ple_w_gate, ln2_g, ln2_b):
    bsz, seq, d = x.shape
    depth = w_in.shape[0]
    t = bsz * seq
    alpha = (2 * depth) ** 0.25
    lay = _LAYOUT
    nb = seq // NSA_BLOCK
    assert nb <= NBLK_PAD and seq % LANE == 0
    n_sel = min(NSA_TOPN, nb)

    aux = jnp.asarray(lay.aux)
    blk_mask = jnp.asarray(
        (np.arange(seq)[:, None] // NSA_BLOCK == np.arange(NBLK_PAD)[None, :]) * MASK_NEG, BF16)
    ex = np.zeros((3, LANE, NSA_HEADS * HEAD_DIM), np.float32)
    for c in range(3):
        for h in range(NSA_HEADS):
            ex[c, c * NSA_HEADS + h, h * HEAD_DIM:(h + 1) * HEAD_DIM] = 1.0
    expand = jnp.asarray(ex, BF16)
    slopes8 = jnp.asarray(_slopes(NSA_HEADS), F32)

    xf = x.reshape(t, d)
    xb = xf.astype(BF16)
    for i in range(depth):
        w_slots = _slot_weights(w_in[i])
        proj2d = _project(xb, w_slots, aux, seq, BF16, 1408)
        gm = _project(xb, w_in[i][:, GM_OFF:].astype(BF16), None, seq, F32, 1024)
        proj = proj2d.reshape(bsz, seq, lay.width)

        def blocks(off):
            c = proj[:, :, off:off + LANE].reshape(bsz, nb, NSA_BLOCK, NSA_KV, HEAD_DIM)
            return c.transpose(0, 1, 3, 2, 4).reshape(bsz * nb * NSA_KV, NSA_BLOCK * HEAD_DIM)
        cmp = _compress(
            jnp.stack([blocks(lay.kc), blocks(lay.vc)]),
            jnp.stack([nsa_pe_k[i].reshape(1, -1), nsa_pe_v[i].reshape(1, -1)]),
            jnp.stack([nsa_w1_k[i], nsa_w1_v[i]]).astype(BF16),
            jnp.stack([nsa_w2_k[i], nsa_w2_v[i]]).astype(BF16))
        cmp = cmp.reshape(2, bsz, nb, NSA_KV, HEAD_DIM)
        kca = _compressed_slots(cmp[0], True)
        vca = _compressed_slots(cmp[1], False)

        o_cmp, selc, touch = _nsa_compressed(proj, kca, vca, seq, n_sel)
        ks = proj[:, :, lay.ks:lay.ks + NSA_KV * LANE].reshape(bsz, seq, NSA_KV, LANE).transpose(0, 2, 1, 3)
        ksel = jnp.concatenate(
            [ks, jnp.broadcast_to(blk_mask[None, None], (bsz, NSA_KV, seq, NBLK_PAD))], axis=-1)
        o_sel = _nsa_selected(proj, selc, touch, ksel, seq)
        hp_a = jnp.stack([slopes8, jnp.zeros_like(slopes8)])
        o_win = _banded(proj, hp_a, seq, NSA_WINDOW, lay.qa, lay.kw, lay.vw, False, F32, "nsa_window")
        hp_b = jnp.stack([slopes8, swa_sinks[i].astype(F32)])
        o_b = _banded(proj, hp_b, seq, SWA_WINDOW, lay.qb, lay.kb, lay.vb, True, BF16, "swa_sinks")

        lambda_init = 0.8 - 0.6 * math.exp(-0.3 * i)
        lam_vecs = jnp.stack([diff_lq1[i], diff_lk1[i], diff_lq2[i], diff_lk2[i]]).astype(F32)
        o_d = _diff_attention(proj, lam_vecs, diff_norm_g[i].reshape(1, -1), seq, lambda_init)

        ha = NSA_HEADS * HEAD_DIM
        xf, xb = _merge(
            xf, o_cmp.reshape(t, ha), o_sel.reshape(t, ha), o_win.reshape(t, ha), proj2d,
            o_b.reshape(t, ha), o_d.reshape(t, ha), gm, expand,
            w_branch_a[i].astype(BF16), w_branch_b[i].astype(BF16), w_branch_c[i].astype(BF16),
            w_out[i].astype(BF16), ln1_g[i].reshape(1, d), ln1_b[i].reshape(1, d), alpha)

        cw_t = _router(xf, router_w[i].T, router_bias[i].reshape(-1, 1))
        y = _moe_dense(xb, cw_t.T, exp_w_gate[i].astype(BF16), exp_w_up[i].astype(BF16),
                       exp_w_down[i].astype(BF16))
        xf, xb = _tail(xf, xb, y, p[i].reshape(t, -1),
                       sh_w_gate[i].astype(BF16), sh_w_up[i].astype(BF16), sh_w_down[i].astype(BF16),
                       ple_w_gate[i].astype(BF16), ple_w_proj[i].astype(BF16),
                       ln2_g[i].reshape(1, d), ln2_b[i].reshape(1, d), alpha)
    return xf.reshape(bsz, seq, d)
```

```python
import functools
import math

import numpy as np
import jax
import jax.numpy as jnp
from jax import lax
from jax.experimental import pallas as pl
from jax.experimental.pallas import tpu as pltpu

F32 = jnp.float32
BF16 = jnp.bfloat16

HEAD_DIM = 64
NSA_HEADS = 8
NSA_KV = 2
NSA_BLOCK = 64
NSA_TOPN = 16
NSA_LOCAL = 2
NSA_WINDOW = 512
SWA_HEADS = 8
SWA_KV = 2
SWA_WINDOW = 128
DIFF_HEADS = 4
N_EXPERTS = 64
TOP_K = 8
N_GROUPS = 8
TOPK_GROUPS = 4
ROUTED_SCALE = 2.5
LN_EPS = 1e-5
NEG = -1e30
FORCE = 1e4
MASK_NEG = -(2.0 ** 100)

LANE = 128
GROUP = NSA_HEADS // NSA_KV
NBLK_PAD = 128
VMEM_LIMIT = 56 * 1024 * 1024

IN_SIZES = (NSA_HEADS * HEAD_DIM,) + (NSA_KV * HEAD_DIM,) * 6 + (NSA_HEADS * 3,) + \
    (SWA_HEADS * HEAD_DIM, SWA_KV * HEAD_DIM, SWA_KV * HEAD_DIM) + \
    (DIFF_HEADS * 2 * HEAD_DIM,) * 3
IN_OFFS = np.concatenate([[0], np.cumsum(IN_SIZES)]).tolist()
GM_OFF = IN_OFFS[-1]


def _slopes(n):
    return [2.0 ** (-8.0 * (h + 1) / n) for h in range(n)]


class _Layout:
    def __init__(self):
        src, scale, bias, pa, pb = [], [], [], [], []

        def slot(cols, sc=1.0, consts=(), pos=False, width=LANE):
            s = [-1] * width
            c = [0.0] * width
            a = [0.0] * width
            b = [0.0] * width
            s[:len(cols)] = cols
            for off, val in consts:
                c[off] = val
            if pos:
                a[HEAD_DIM] = 1.0
                b[HEAD_DIM + 1] = 1.0
            start = len(src)
            src.extend(s)
            scale.extend([sc] * width)
            bias.extend(c)
            pa.extend(a)
            pb.extend(b)
            return start

        def rng(base, n):
            return list(range(base, base + n))

        qs = HEAD_DIM ** -0.5
        o = IN_OFFS
        sl8 = _slopes(NSA_HEADS)
        sl4 = _slopes(DIFF_HEADS)
        self.qa = len(src)
        for h in range(NSA_HEADS):
            slot(rng(o[0] + h * HEAD_DIM, HEAD_DIM), qs,
                 [(HEAD_DIM, sl8[h] * LANE), (HEAD_DIM + 1, sl8[h])])
        self.qb = len(src)
        for h in range(SWA_HEADS):
            slot(rng(o[8] + h * HEAD_DIM, HEAD_DIM), qs,
                 [(HEAD_DIM, sl8[h] * LANE), (HEAD_DIM + 1, sl8[h])])
        self.qd = len(src)
        for h in range(DIFF_HEADS):
            for r in range(2):
                slot(rng(o[11] + (h * 2 + r) * HEAD_DIM, HEAD_DIM), qs,
                     [(HEAD_DIM, sl4[h] * LANE), (HEAD_DIM + 1, sl4[h])])
        self.kd = len(src)
        for h in range(DIFF_HEADS):
            for r in range(2):
                slot(rng(o[12] + (h * 2 + r) * HEAD_DIM, HEAD_DIM), pos=True)
        self.vd = len(src)
        for h in range(DIFF_HEADS):
            slot(rng(o[13] + h * 2 * HEAD_DIM, 2 * HEAD_DIM),
                 consts=[(2 * HEAD_DIM, 1.0)], width=2 * LANE)

        def kv_slots(k_src, v_src):
            k_off = len(src)
            for k in range(NSA_KV):
                slot(rng(k_src + k * HEAD_DIM, HEAD_DIM), pos=True)
            v_off = len(src)
            for k in range(NSA_KV):
                slot(rng(v_src + k * HEAD_DIM, HEAD_DIM), consts=[(HEAD_DIM, 1.0)])
            return k_off, v_off

        self.ks, self.vs = kv_slots(o[3], o[4])
        self.kw, self.vw = kv_slots(o[5], o[6])
        self.kb, self.vb = kv_slots(o[9], o[10])
        self.kc = slot(rng(o[1], NSA_KV * HEAD_DIM))
        self.vc = slot(rng(o[2], NSA_KV * HEAD_DIM))
        self.ga = slot([o[7] + h * 3 + c for c in range(3) for h in range(NSA_HEADS)])
        self.width = len(src)
        self.src = np.asarray(src, np.int32)
        self.scale = np.asarray(scale, np.float32)
        aux = np.zeros((8, self.width), np.float32)
        aux[0] = bias
        aux[1] = pa
        aux[2] = pb
        self.aux = aux


_LAYOUT = _Layout()


def _cparams(*sem):
    return pltpu.CompilerParams(dimension_semantics=sem, vmem_limit_bytes=VMEM_LIMIT)


def _tile(n, pref):
    t = min(n, pref)
    assert n % t == 0, (n, pref)
    return t


def _iota(shape, dim):
    return lax.broadcasted_iota(jnp.int32, shape, dim)


def _dot(a, b):
    return jnp.dot(a, b, preferred_element_type=F32)


def _dot_nt(a, b):
    return lax.dot_general(a, b, (((1,), (1,)), ((), ())), preferred_element_type=F32)


def _proj_kernel(x_ref, w_ref, aux_ref, o_ref, *, seq, tm):
    acc = _dot(x_ref[...], w_ref[...])
    pos = (pl.program_id(0) * tm) % seq + _iota((tm, 1), 0)
    a = (pos >> 7).astype(F32)
    b = (pos & (LANE - 1)).astype(F32)
    aux = aux_ref[...]
    o_ref[...] = (acc + aux[0:1] + a * aux[1:2] + b * aux[2:3]).astype(o_ref.dtype)


def _matmul_kernel(x_ref, w_ref, o_ref):
    o_ref[...] = _dot(x_ref[...], w_ref[...]).astype(o_ref.dtype)


def _project(xb, w, aux, seq, out_dtype, tn_pref):
    t, d = xb.shape
    n = w.shape[1]
    tm = _tile(t, 1024)
    tn = _tile(n, tn_pref)
    in_specs = [pl.BlockSpec((tm, d), lambda i, j: (i, 0)),
                pl.BlockSpec((d, tn), lambda i, j: (0, j))]
    args = [xb, w]
    if aux is None:
        body = _matmul_kernel
    else:
        body = functools.partial(_proj_kernel, seq=seq, tm=tm)
        in_specs.append(pl.BlockSpec((8, tn), lambda i, j: (0, j)))
        args.append(aux)
    return pl.pallas_call(
        body,
        grid=(t // tm, n // tn),
        in_specs=in_specs,
        out_specs=pl.BlockSpec((tm, tn), lambda i, j: (i, j)),
        out_shape=jax.ShapeDtypeStruct((t, n), out_dtype),
        compiler_params=_cparams("parallel", "parallel"),
        name="proj" if aux is not None else "gate_proj",
    )(*args)


def _compress_kernel(x_ref, pe_ref, w1_ref, w2_ref, o_ref):
    xb = (x_ref[0].astype(F32) + pe_ref[0]).astype(BF16)
    h = jax.nn.gelu(_dot(xb, w1_ref[0]))
    o_ref[0] = _dot(h.astype(BF16), w2_ref[0])


def _compress(xs, pes, w1s, w2s):
    _, r, kdim = xs.shape
    hid = w1s.shape[2]
    dh = w2s.shape[2]
    return pl.pallas_call(
        _compress_kernel,
        grid=(2,),
        in_specs=[pl.BlockSpec((1, r, kdim), lambda i: (i, 0, 0)),
                  pl.BlockSpec((1, 1, kdim), lambda i: (i, 0, 0)),
                  pl.BlockSpec((1, kdim, hid), lambda i: (i, 0, 0)),
                  pl.BlockSpec((1, hid, dh), lambda i: (i, 0, 0))],
        out_specs=pl.BlockSpec((1, r, dh), lambda i: (i, 0, 0)),
        out_shape=jax.ShapeDtypeStruct((2, r, dh), F32),
        compiler_params=_cparams("parallel"),
        name="nsa_compress",
    )(xs, pes, w1s, w2s)


def _stack_heads(q, n):
    return jnp.concatenate([q[:, g * LANE:(g + 1) * LANE] for g in range(n)], axis=0)


def _unstack_heads(o, n, tq, width):
    return jnp.concatenate([o[g * tq:(g + 1) * tq, :width] for g in range(n)], axis=1)


def _cmp_kernel(q_ref, kc_ref, vc_ref, oc_ref, sel_ref, touch_ref, *, tq, n_sel):
    i = pl.program_id(2)
    qs = _stack_heads(q_ref[0], GROUP)
    s = _dot_nt(qs, kc_ref[0, 0])
    t1 = i * tq + _iota((tq, 1), 0)
    t = jnp.concatenate([t1] * GROUP, axis=0)
    n = _iota((1, NBLK_PAD), 1)
    valid = (n * NSA_BLOCK + (NSA_BLOCK - 1)) <= t
    s = jnp.where(valid, s, NEG)
    mx = jnp.max(s, axis=-1, keepdims=True)
    e = jnp.where(valid, jnp.exp(s - mx), 0.0)
    p = e / jnp.maximum(jnp.sum(e, axis=-1, keepdims=True), 1e-30)
    o = _dot(p.astype(BF16), vc_ref[0, 0])
    oc_ref[0] = _unstack_heads(o, GROUP, tq, HEAD_DIM)

    imp = p[0:tq]
    for g in range(1, GROUP):
        imp = imp + p[g * tq:(g + 1) * tq]
    rel = (t1 >> 6) - n
    forced = (n == 0) | ((rel >= 0) & (rel < NSA_LOCAL))
    val = jnp.where(forced, FORCE, jnp.where(rel >= 0, imp, -1.0))
    sel = jnp.zeros((tq, NBLK_PAD), F32)
    nf = n.astype(F32)
    for _ in range(n_sel):
        mx = jnp.max(val, axis=-1, keepdims=True)
        idx = jnp.min(jnp.where(val == mx, nf, float(NBLK_PAD)), axis=-1, keepdims=True)
        pick = nf == idx
        sel = jnp.where(pick & (mx >= 0.0), 1.0, sel)
        val = jnp.where(pick, -jnp.inf, val)
    sel_ref[0, 0] = (1.0 - sel).astype(BF16)
    touch_ref[0, 0, 0] = jnp.broadcast_to(jnp.max(sel, axis=0, keepdims=True), (8, NBLK_PAD))


def _nsa_compressed(proj, kca, vca, seq, n_sel):
    b = proj.shape[0]
    tq = _tile(seq, 256)
    lay = _LAYOUT
    gw = GROUP * LANE
    return pl.pallas_call(
        functools.partial(_cmp_kernel, tq=tq, n_sel=n_sel),
        grid=(b, NSA_KV, seq // tq),
        in_specs=[pl.BlockSpec((1, tq, gw), lambda b_, k, i: (b_, i, lay.qa // gw + k)),
                  pl.BlockSpec((1, 1, NBLK_PAD, LANE), lambda b_, k, i: (b_, k, 0, 0)),
                  pl.BlockSpec((1, 1, NBLK_PAD, LANE), lambda b_, k, i: (b_, k, 0, 0))],
        out_specs=[pl.BlockSpec((1, tq, GROUP * HEAD_DIM), lambda b_, k, i: (b_, i, k)),
                   pl.BlockSpec((1, 1, tq, NBLK_PAD), lambda b_, k, i: (b_, k, i, 0)),
                   pl.BlockSpec((1, 1, 1, 8, NBLK_PAD), lambda b_, k, i: (b_, k, i, 0, 0))],
        out_shape=[jax.ShapeDtypeStruct((b, seq, NSA_HEADS * HEAD_DIM), F32),
                   jax.ShapeDtypeStruct((b, NSA_KV, seq, NBLK_PAD), BF16),
                   jax.ShapeDtypeStruct((b, NSA_KV, seq // tq, 8, NBLK_PAD), F32)],
        compiler_params=_cparams("parallel", "parallel", "parallel"),
        name="nsa_compressed",
    )(proj, kca, vca)


def _flash(chunk_fns, v_ref, i, tq, tk, m_ref, acc_ref, s_refs, n_active=None, tile_at=None):
    m_ref[...] = jnp.full(m_ref.shape, NEG, F32)
    acc_ref[...] = jnp.zeros(acc_ref.shape, F32)
    acc_reps = acc_ref.shape[1] // LANE
    n_full = (i * tq) // tk
    if n_active is None:
        n_active = n_full
        tile_at = lambda n: n

    def nth_tile(n):
        return jnp.where(n < n_active, tile_at(n), n_full)

    def issue(n, s_ref):
        start = pl.multiple_of(nth_tile(n) * tk, tk)
        for c, fn in enumerate(chunk_fns):
            s_ref[c * tq:(c + 1) * tq, :] = fn(start)

    def accumulate(j, s_ref, visible):
        start = pl.multiple_of(j * tk, tk)
        vt = v_ref[pl.ds(start, tk), :]
        for c in range(len(chunk_fns)):
            rows = slice(c * tq, (c + 1) * tq)
            s = s_ref[rows, :]
            if visible is not None:
                s = jnp.where(visible, s, NEG)
            m_old = m_ref[rows, :]
            m_new = jnp.maximum(m_old, jnp.max(s, axis=-1, keepdims=True))
            p = jnp.exp((s - jnp.concatenate([m_new] * (tk // LANE), axis=1)).astype(BF16))
            alpha = jnp.concatenate([jnp.exp(m_old - m_new)] * acc_reps, axis=1)
            acc_ref[rows, :] = alpha * acc_ref[rows, :] + _dot(p, vt)
            m_ref[rows, :] = m_new

    s0, s1 = s_refs
    issue(0, s0)

    def body(pair, carry):
        n = 2 * pair
        issue(n + 1, s1)
        accumulate(nth_tile(n), s0, None)
        issue(n + 2, s0)
        accumulate(nth_tile(n + 1), s1, None)
        return carry

    pairs = n_active // 2
    lax.fori_loop(0, pairs, body, 0)
    visible = (_iota((tq, tk), 1) - _iota((tq, tk), 0)) <= i * tq - n_full * tk
    leftover = n_active - 2 * pairs

    @pl.when(leftover == 1)
    def _():
        issue(n_active, s1)
        accumulate(nth_tile(n_active - 1), s0, None)
        accumulate(n_full, s1, visible)

    @pl.when(leftover == 0)
    def _():
        accumulate(n_full, s0, visible)


def _sel_kernel(order_ref, count_ref, q_ref, sel_ref, k_ref, v_ref, o_ref, qa_ref, m_ref, acc_ref,
                s0_ref, s1_ref, *, tq, tk, n_tiles):
    i = pl.program_id(2)
    tile_id = (pl.program_id(0) * pl.num_programs(1) + pl.program_id(1)) * pl.num_programs(2) + i
    q = q_ref[0]
    sc = sel_ref[0, 0]
    for g in range(GROUP):
        qa_ref[g * tq:(g + 1) * tq, 0:LANE] = q[:, g * LANE:(g + 1) * LANE]
        qa_ref[g * tq:(g + 1) * tq, LANE:2 * LANE] = sc
    kv = k_ref.at[0, 0]

    def head_scores(g):
        return lambda start: _dot_nt(qa_ref[g * tq:(g + 1) * tq, :], kv[pl.ds(start, tk), :])

    _flash([head_scores(g) for g in range(GROUP)], v_ref.at[0], i, tq, tk, m_ref, acc_ref, (s0_ref, s1_ref),
           n_active=count_ref[tile_id], tile_at=lambda n: order_ref[tile_id * n_tiles + n])
    acc = acc_ref[...]
    o = acc / acc[:, HEAD_DIM:HEAD_DIM + 1]
    o_ref[0] = _unstack_heads(o, GROUP, tq, HEAD_DIM)


def _nsa_selected(proj, selc, touch, ksel, seq):
    b = proj.shape[0]
    tq = _tile(seq, 256)
    tk = _tile(seq, 512)
    n_tiles = seq // tk
    lay = _LAYOUT
    gw = GROUP * LANE
    per_tile = touch[:, :, :, 0, :seq // NSA_BLOCK].reshape(b, NSA_KV, seq // tq, n_tiles, tk // NSA_BLOCK)
    n_full = (np.arange(seq // tq) * tq) // tk
    active = (jnp.max(per_tile, axis=-1) > 0.0) & jnp.asarray(np.arange(n_tiles)[None, :] < n_full[:, None])
    order = jnp.argsort(jnp.where(active, 0, 1), axis=-1, stable=True).astype(jnp.int32).reshape(-1)
    count = jnp.sum(active, axis=-1).astype(jnp.int32).reshape(-1)
    grid_spec = pltpu.PrefetchScalarGridSpec(
        num_scalar_prefetch=2,
        grid=(b, NSA_KV, seq // tq),
        in_specs=[pl.BlockSpec((1, tq, gw), lambda b_, k, i, o, c: (b_, i, lay.qa // gw + k)),
                  pl.BlockSpec((1, 1, tq, NBLK_PAD), lambda b_, k, i, o, c: (b_, k, i, 0)),
                  pl.BlockSpec((1, 1, seq, 2 * LANE), lambda b_, k, i, o, c: (b_, k, 0, 0)),
                  pl.BlockSpec((1, seq, LANE), lambda b_, k, i, o, c: (b_, 0, lay.vs // LANE + k))],
        out_specs=pl.BlockSpec((1, tq, GROUP * HEAD_DIM), lambda b_, k, i, o, c: (b_, i, k)),
        scratch_shapes=[pltpu.VMEM((GROUP * tq, 2 * LANE), BF16),
                        pltpu.VMEM((GROUP * tq, LANE), F32),
                        pltpu.VMEM((GROUP * tq, LANE), F32),
                        pltpu.VMEM((GROUP * tq, tk), F32),
                        pltpu.VMEM((GROUP * tq, tk), F32)])
    return pl.pallas_call(
        functools.partial(_sel_kernel, tq=tq, tk=tk, n_tiles=n_tiles),
        grid_spec=grid_spec,
        out_shape=jax.ShapeDtypeStruct((b, seq, NSA_HEADS * HEAD_DIM), F32),
        compiler_params=_cparams("parallel", "parallel", "arbitrary"),
        name="nsa_selected",
    )(order, count, proj, selc, ksel, proj)


def _diff_kernel(lam_ref, q_ref, k_ref, v_ref, g_ref, o_ref, m_ref, acc_ref, s0_ref, s1_ref,
                 *, tq, tk, lambda_init):
    i = pl.program_id(2)
    kv = k_ref.at[0]

    def map_scores(r):
        cols = slice(r * LANE, (r + 1) * LANE)
        return lambda start: _dot_nt(q_ref[0, :, cols], kv[pl.ds(start, tk), cols])

    _flash([map_scores(0), map_scores(1)], v_ref.at[0], i, tq, tk, m_ref, acc_ref, (s0_ref, s1_ref))
    acc = acc_ref[...]
    dv = 2 * HEAD_DIM
    a = acc[:, :dv] / acc[:, dv:dv + 1]
    lv = lam_ref[...]
    lam = (jnp.exp(jnp.sum(lv[0:1] * lv[1:2], axis=-1, keepdims=True))
           - jnp.exp(jnp.sum(lv[2:3] * lv[3:4], axis=-1, keepdims=True)) + lambda_init)
    o = a[0:tq] - lam * a[tq:2 * tq]
    o = o * lax.rsqrt(jnp.mean(o * o, axis=-1, keepdims=True) + LN_EPS) * g_ref[...]
    o_ref[0] = (o * (1.0 - lambda_init)).astype(o_ref.dtype)


def _diff_attention(proj, lam_vecs, norm_g, seq, lambda_init):
    b = proj.shape[0]
    tq = _tile(seq, 512)
    tk = _tile(seq, 512)
    lay = _LAYOUT
    w2 = 2 * LANE
    return pl.pallas_call(
        functools.partial(_diff_kernel, tq=tq, tk=tk, lambda_init=lambda_init),
        grid=(b, DIFF_HEADS, seq // tq),
        in_specs=[pl.BlockSpec((4, HEAD_DIM), lambda b_, h, i: (0, 0)),
                  pl.BlockSpec((1, tq, w2), lambda b_, h, i: (b_, i, lay.qd // w2 + h)),
                  pl.BlockSpec((1, seq, w2), lambda b_, h, i: (b_, 0, lay.kd // w2 + h)),
                  pl.BlockSpec((1, seq, w2), lambda b_, h, i: (b_, 0, lay.vd // w2 + h)),
                  pl.BlockSpec((1, 2 * HEAD_DIM), lambda b_, h, i: (0, 0))],
        out_specs=pl.BlockSpec((1, tq, 2 * HEAD_DIM), lambda b_, h, i: (b_, i, h)),
        out_shape=jax.ShapeDtypeStruct((b, seq, DIFF_HEADS * 2 * HEAD_DIM), BF16),
        scratch_shapes=[pltpu.VMEM((2 * tq, LANE), F32),
                        pltpu.VMEM((2 * tq, w2), F32),
                        pltpu.VMEM((2 * tq, tk), F32),
                        pltpu.VMEM((2 * tq, tk), F32)],
        compiler_params=_cparams("parallel", "parallel", "arbitrary"),
        name="diff_attention",
    )(lam_vecs, proj, proj, proj, norm_g)


def _band_kernel(hp_ref, q_ref, k_ref, v_ref, o_ref, *, tq, window, band, use_sinks):
    kvh = pl.program_id(1)
    i = pl.program_id(2)
    start = pl.multiple_of(jnp.maximum(i * tq - window, 0), LANE)
    kb = k_ref[0, pl.ds(start, band), :]
    vb = v_ref[0, pl.ds(start, band), :]
    t1 = i * tq + _iota((tq, 1), 0)
    dist = t1 - (start + _iota((1, band), 1))
    visible = (dist >= 0) & (dist < window)
    outs = []
    for g in range(GROUP):
        s = jnp.where(visible, _dot_nt(q_ref[0, :, g * LANE:(g + 1) * LANE], kb), NEG)
        mx = jnp.max(s, axis=-1, keepdims=True)
        if use_sinks:
            h = kvh * GROUP + g
            sink = hp_ref[1, h] + hp_ref[0, h] * t1.astype(F32)
            mx = jnp.maximum(mx, sink)
        acc = _dot(jnp.exp((s - mx).astype(BF16)), vb)
        den = acc[:, HEAD_DIM:HEAD_DIM + 1]
        if use_sinks:
            den = den + jnp.exp(sink - mx)
        outs.append((acc / den)[:, :HEAD_DIM])
    o_ref[0] = jnp.concatenate(outs, axis=1).astype(o_ref.dtype)


def _banded(proj, head_params, seq, window, q_off, k_off, v_off, use_sinks, out_dtype, name):
    b = proj.shape[0]
    tq = _tile(seq, 256)
    band = min(window + tq, seq)
    gw = GROUP * LANE
    return pl.pallas_call(
        functools.partial(_band_kernel, tq=tq, window=window, band=band, use_sinks=use_sinks),
        grid=(b, NSA_KV, seq // tq),
        in_specs=[pl.BlockSpec(memory_space=pltpu.SMEM),
                  pl.BlockSpec((1, tq, gw), lambda b_, k, i: (b_, i, q_off // gw + k)),
                  pl.BlockSpec((1, seq, LANE), lambda b_, k, i: (b_, 0, k_off // LANE + k)),
                  pl.BlockSpec((1, seq, LANE), lambda b_, k, i: (b_, 0, v_off // LANE + k))],
        out_specs=pl.BlockSpec((1, tq, GROUP * HEAD_DIM), lambda b_, k, i: (b_, i, k)),
        out_shape=jax.ShapeDtypeStruct((b, seq, NSA_HEADS * HEAD_DIM), out_dtype),
        compiler_params=_cparams("parallel", "parallel", "parallel"),
        name=name,
    )(head_params, proj, proj, proj)


def _layer_norm(z, g, b):
    mu = jnp.mean(z, axis=-1, keepdims=True)
    zc = z - mu
    var = jnp.mean(zc * zc, axis=-1, keepdims=True)
    return zc * lax.rsqrt(var + LN_EPS) * g + b


def _merge_kernel(x_ref, oc_ref, os_ref, ow_ref, ga_ref, ob_ref, od_ref, gm_ref, ex_ref,
                  wa_ref, wb_ref, wc_ref, wo_ref, lng_ref, lnb_ref, xo_ref, xb_ref, *, alpha):
    d = x_ref.shape[1]
    ga = jax.nn.sigmoid(ga_ref[...].astype(F32)).astype(BF16)
    oa = (_dot(ga, ex_ref[0]) * oc_ref[...] + _dot(ga, ex_ref[1]) * os_ref[...]
          + _dot(ga, ex_ref[2]) * ow_ref[...])
    gm = jax.nn.sigmoid(gm_ref[...])
    merged = (gm[:, 0:d] * _dot(oa.astype(BF16), wa_ref[...])
              + gm[:, d:2 * d] * _dot(ob_ref[...], wb_ref[...])
              + gm[:, 2 * d:3 * d] * _dot(od_ref[...], wc_ref[...]))
    z = alpha * x_ref[...] + _dot(merged.astype(BF16), wo_ref[...])
    y = _layer_norm(z, lng_ref[...], lnb_ref[...])
    xo_ref[...] = y
    xb_ref[...] = y.astype(BF16)


def _merge(x, oc, os_, ow, proj2d, ob, od, gm, expand, wa, wb, wc, wo, lng, lnb, alpha):
    t, d = x.shape
    tm = _tile(t, 256)
    ha = NSA_HEADS * HEAD_DIM
    row = lambda w: pl.BlockSpec((tm, w), lambda i: (i, 0))
    full = lambda a: pl.BlockSpec(a.shape, lambda i: (0,) * a.ndim)
    return pl.pallas_call(
        functools.partial(_merge_kernel, alpha=alpha),
        grid=(t // tm,),
        in_specs=[row(d), row(ha), row(ha), row(ha),
                  pl.BlockSpec((tm, LANE), lambda i: (i, _LAYOUT.ga // LANE)),
                  row(ha), row(ha), row(3 * d), full(expand),
                  full(wa), full(wb), full(wc), full(wo), full(lng), full(lnb)],
        out_specs=[row(d), row(d)],
        out_shape=[jax.ShapeDtypeStruct((t, d), F32), jax.ShapeDtypeStruct((t, d), BF16)],
        compiler_params=_cparams("parallel"),
        name="merge_norm",
    )(x, oc, os_, ow, proj2d, ob, od, gm, expand, wa, wb, wc, wo, lng, lnb)


def _first_max(v, idx, n):
    mx = jnp.max(v, axis=0, keepdims=True)
    first = jnp.min(jnp.where(v == mx, idx, n), axis=0, keepdims=True)
    return mx, idx == first


def _router_kernel(x_ref, rw_ref, rb_ref, cw_ref):
    gsz = N_EXPERTS // N_GROUPS
    logits = lax.dot_general(rw_ref[...], x_ref[...], (((1,), (1,)), ((), ())),
                             precision=lax.Precision.HIGHEST,
                             preferred_element_type=F32)
    tm = logits.shape[1]
    scores = jax.nn.sigmoid(logits)
    biased = scores + rb_ref[...]
    iw = _iota((gsz, tm), 0)
    gs = []
    for g in range(N_GROUPS):
        blk = biased[g * gsz:(g + 1) * gsz]
        m1, pick = _first_max(blk, iw, gsz)
        m2 = jnp.max(jnp.where(pick, -jnp.inf, blk), axis=0, keepdims=True)
        gs.append(m1 + m2)
    gs = jnp.concatenate(gs, axis=0)
    ig = _iota((N_GROUPS, tm), 0)
    gmask = jnp.zeros((N_GROUPS, tm), F32)
    for _ in range(TOPK_GROUPS):
        _, pick = _first_max(gs, ig, N_GROUPS)
        gmask = jnp.where(pick, 1.0, gmask)
        gs = jnp.where(pick, -jnp.inf, gs)
    emask = jnp.concatenate(
        [jnp.broadcast_to(gmask[g:g + 1], (gsz, tm)) for g in range(N_GROUPS)], axis=0)
    cand = jnp.where(emask > 0.5, biased, NEG)
    ie = _iota((N_EXPERTS, tm), 0)
    chosen = jnp.zeros((N_EXPERTS, tm), F32)
    for _ in range(TOP_K):
        _, pick = _first_max(cand, ie, N_EXPERTS)
        chosen = jnp.where(pick, 1.0, chosen)
        cand = jnp.where(pick, -jnp.inf, cand)
    w = scores * chosen
    cw_ref[...] = w / jnp.sum(w, axis=0, keepdims=True) * ROUTED_SCALE


def _router(x, rw_t, rb):
    t, d = x.shape
    tm = _tile(t, 512)
    return pl.pallas_call(
        _router_kernel,
        grid=(t // tm,),
        in_specs=[pl.BlockSpec((tm, d), lambda i: (i, 0)),
                  pl.BlockSpec((N_EXPERTS, d), lambda i: (0, 0)),
                  pl.BlockSpec((N_EXPERTS, 1), lambda i: (0, 0))],
        out_specs=pl.BlockSpec((N_EXPERTS, tm), lambda i: (0, i)),
        out_shape=jax.ShapeDtypeStruct((N_EXPERTS, t), F32),
        compiler_params=_cparams("parallel"),
        name="router",
    )(x, rw_t, rb)


def _moe_kernel(x_ref, cw_ref, wg_ref, wu_ref, wd_ref, y_ref, *, ne):
    c = pl.program_id(1)
    x = x_ref[...]
    edim = wg_ref.shape[2]
    cw = cw_ref[...]
    cw_hi = cw.astype(BF16)
    cw_lo = (cw - cw_hi.astype(F32)).astype(BF16)
    expand = jnp.concatenate(
        [jnp.where(_iota((N_EXPERTS, edim), 0) == c * ne + e, 1.0, 0.0) for e in range(ne)],
        axis=1).astype(BF16)
    scale = _dot(cw_hi, expand) + _dot(cw_lo, expand)
    hs = []
    for e in range(ne):
        hs.append(jax.nn.silu(_dot(x, wg_ref[e])) * _dot(x, wu_ref[e]))
    h = (jnp.concatenate(hs, axis=1) * scale).astype(BF16)
    y = _dot(h, wd_ref[...].reshape(ne * edim, wd_ref.shape[2]))

    @pl.when(c == 0)
    def _():
        y_ref[...] = y

    @pl.when(c != 0)
    def _():
        y_ref[...] += y


def _moe_dense(xb, cw, wg, wu, wd):
    t, d = xb.shape
    edim = wg.shape[2]
    tm = _tile(t, 512)
    ne = 8
    return pl.pallas_call(
        functools.partial(_moe_kernel, ne=ne),
        grid=(t // tm, N_EXPERTS // ne),
        in_specs=[pl.BlockSpec((tm, d), lambda i, c: (i, 0)),
                  pl.BlockSpec((tm, N_EXPERTS), lambda i, c: (i, 0)),
                  pl.BlockSpec((ne, d, edim), lambda i, c: (c, 0, 0)),
                  pl.BlockSpec((ne, d, edim), lambda i, c: (c, 0, 0)),
                  pl.BlockSpec((ne, edim, d), lambda i, c: (c, 0, 0))],
        out_specs=pl.BlockSpec((tm, d), lambda i, c: (i, 0)),
        out_shape=jax.ShapeDtypeStruct((t, d), F32),
        compiler_params=_cparams("parallel", "arbitrary"),
        name="moe_experts",
    )(xb, cw, wg, wu, wd)


def _tail_kernel(x_ref, xb_ref, y_ref, p_ref, sg_ref, su_ref, sd_ref, pg_ref, pp_ref,
                 lng_ref, lnb_ref, xo_ref, xbo_ref, *, alpha):
    xb = xb_ref[...]
    h = jax.nn.silu(_dot(xb, sg_ref[...])) * _dot(xb, su_ref[...])
    shared = _dot(h.astype(BF16), sd_ref[...])
    ple = jax.nn.sigmoid(_dot(xb, pg_ref[...])) * _dot(p_ref[...].astype(BF16), pp_ref[...])
    z = alpha * x_ref[...] + y_ref[...] + shared + ple
    y = _layer_norm(z, lng_ref[...], lnb_ref[...])
    xo_ref[...] = y
    xbo_ref[...] = y.astype(BF16)


def _tail(x, xb, y, p, sg, su, sd, pg, pp, lng, lnb, alpha):
    t, d = x.shape
    tm = _tile(t, 256)
    row = lambda w: pl.BlockSpec((tm, w), lambda i: (i, 0))
    full = lambda a: pl.BlockSpec(a.shape, lambda i: (0,) * a.ndim)
    return pl.pallas_call(
        functools.partial(_tail_kernel, alpha=alpha),
        grid=(t // tm,),
        in_specs=[row(d), row(d), row(d), row(p.shape[1]), full(sg), full(su), full(sd),
                  full(pg), full(pp), full(lng), full(lnb)],
        out_specs=[row(d), row(d)],
        out_shape=[jax.ShapeDtypeStruct((t, d), F32), jax.ShapeDtypeStruct((t, d), BF16)],
        compiler_params=_cparams("parallel"),
        name="ffn_tail_norm",
    )(x, xb, y, p, sg, su, sd, pg, pp, lng, lnb)


def _slot_weights(w):
    d = w.shape[0]
    o = IN_OFFS
    qs = HEAD_DIM ** -0.5

    def heads(off, n, width=HEAD_DIM, slot=LANE, scale=1.0):
        blk = w[:, off:off + n * width].reshape(d, n, width) * scale
        return jnp.pad(blk, ((0, 0), (0, 0), (0, slot - width))).reshape(d, n * slot)

    gates = w[:, o[7]:o[8]].reshape(d, NSA_HEADS, 3).transpose(0, 2, 1).reshape(d, 3 * NSA_HEADS)
    parts = [heads(o[0], NSA_HEADS, scale=qs), heads(o[8], SWA_HEADS, scale=qs),
             heads(o[11], 2 * DIFF_HEADS, scale=qs), heads(o[12], 2 * DIFF_HEADS),
             heads(o[13], DIFF_HEADS, 2 * HEAD_DIM, 2 * LANE),
             heads(o[3], NSA_KV), heads(o[4], NSA_KV), heads(o[5], NSA_KV), heads(o[6], NSA_KV),
             heads(o[9], SWA_KV), heads(o[10], SWA_KV),
             w[:, o[1]:o[2]], w[:, o[2]:o[3]],
             jnp.pad(gates, ((0, 0), (0, LANE - 3 * NSA_HEADS)))]
    out = jnp.concatenate(parts, axis=1)
    assert out.shape[1] == _LAYOUT.width
    return out.astype(BF16)


def _compressed_slots(c, pos_cols):
    b, nb, hkv, dh = c.shape
    c = c.transpose(0, 2, 1, 3)
    extra = jnp.zeros((b, hkv, nb, LANE - dh), F32)
    if pos_cols:
        cpos = np.arange(nb) * NSA_BLOCK + NSA_BLOCK - 1
        cols = np.zeros((nb, LANE - dh), np.float32)
        cols[:, 0] = cpos // LANE
        cols[:, 1] = cpos % LANE
        extra = extra + jnp.asarray(cols)
    c = jnp.concatenate([c, extra], axis=-1)
    c = jnp.pad(c, ((0, 0), (0, 0), (0, NBLK_PAD - nb), (0, 0)))
    return c.astype(BF16)


def kernel(x, p, w_in, nsa_pe_k, nsa_w1_k, nsa_w2_k, nsa_pe_v, nsa_w1_v, nsa_w2_v, swa_sinks,
           diff_lq1, diff_lk1, diff_lq2, diff_lk2, diff_norm_g, w_branch_a, w_branch_b, w_branch_c,
           w_out, ln1_g, ln1_b, router_w, router_bias, exp_w_gate, exp_w_up, exp_w_down,
           sh_w_gate, sh_w_up, sh_w_down, ple_w_proj, ple_w_gate, ln2_g, ln2_b):
    bsz, seq, d = x.shape
    depth = w_in.shape[0]
    t = bsz * seq
    alpha = (2 * depth) ** 0.25
    lay = _LAYOUT
    nb = seq // NSA_BLOCK
    assert nb <= NBLK_PAD and seq % LANE == 0
    n_sel = min(NSA_TOPN, nb)

    aux = jnp.asarray(lay.aux)
    blk_mask = jnp.asarray(
        (np.arange(seq)[:, None] // NSA_BLOCK == np.arange(NBLK_PAD)[None, :]) * MASK_NEG, BF16)
    ex = np.zeros((3, LANE, NSA_HEADS * HEAD_DIM), np.float32)
    for c in range(3):
        for h in range(NSA_HEADS):
            ex[c, c * NSA_HEADS + h, h * HEAD_DIM:(h + 1) * HEAD_DIM] = 1.0
    expand = jnp.asarray(ex, BF16)
    slopes8 = jnp.asarray(_slopes(NSA_HEADS), F32)

    xf = x.reshape(t, d)
    xb = xf.astype(BF16)
    for i in range(depth):
        w_slots = _slot_weights(w_in[i])
        proj2d = _project(xb, w_slots, aux, seq, BF16, 1408)
        gm = _project(xb, w_in[i][:, GM_OFF:].astype(BF16), None, seq, F32, 1024)
        proj = proj2d.reshape(bsz, seq, lay.width)

        def blocks(off):
            c = proj[:, :, off:off + LANE].reshape(bsz, nb, NSA_BLOCK, NSA_KV, HEAD_DIM)
            return c.transpose(0, 1, 3, 2, 4).reshape(bsz * nb * NSA_KV, NSA_BLOCK * HEAD_DIM)
        cmp = _compress(
            jnp.stack([blocks(lay.kc), blocks(lay.vc)]),
            jnp.stack([nsa_pe_k[i].reshape(1, -1), nsa_pe_v[i].reshape(1, -1)]),
            jnp.stack([nsa_w1_k[i], nsa_w1_v[i]]).astype(BF16),
            jnp.stack([nsa_w2_k[i], nsa_w2_v[i]]).astype(BF16))
        cmp = cmp.reshape(2, bsz, nb, NSA_KV, HEAD_DIM)
        kca = _compressed_slots(cmp[0], True)
        vca = _compressed_slots(cmp[1], False)

        o_cmp, selc, touch = _nsa_compressed(proj, kca, vca, seq, n_sel)
        ks = proj[:, :, lay.ks:lay.ks + NSA_KV * LANE].reshape(bsz, seq, NSA_KV, LANE).transpose(0, 2, 1, 3)
        ksel = jnp.concatenate(
            [ks, jnp.broadcast_to(blk_mask[None, None], (bsz, NSA_KV, seq, NBLK_PAD))], axis=-1)
        o_sel = _nsa_selected(proj, selc, touch, ksel, seq)
        hp_a = jnp.stack([slopes8, jnp.zeros_like(slopes8)])
        o_win = _banded(proj, hp_a, seq, NSA_WINDOW, lay.qa, lay.kw, lay.vw, False, F32, "nsa_window")
        hp_b = jnp.stack([slopes8, swa_sinks[i].astype(F32)])
        o_b = _banded(proj, hp_b, seq, SWA_WINDOW, lay.qb, lay.kb, lay.vb, True, BF16, "swa_sinks")

        lambda_init = 0.8 - 0.6 * math.exp(-0.3 * i)
        lam_vecs = jnp.stack([diff_lq1[i], diff_lk1[i], diff_lq2[i], diff_lk2[i]]).astype(F32)
        o_d = _diff_attention(proj, lam_vecs, diff_norm_g[i].reshape(1, -1), seq, lambda_init)

        ha = NSA_HEADS * HEAD_DIM
        xf, xb = _merge(
            xf, o_cmp.reshape(t, ha), o_sel.reshape(t, ha), o_win.reshape(t, ha), proj2d,
            o_b.reshape(t, ha), o_d.reshape(t, ha), gm, expand,
            w_branch_a[i].astype(BF16), w_branch_b[i].astype(BF16), w_branch_c[i].astype(BF16),
            w_out[i].astype(BF16), ln1_g[i].reshape(1, d), ln1_b[i].reshape(1, d), alpha)

        cw_t = _router(xf, router_w[i].T, router_bias[i].reshape(-1, 1))
        y = _moe_dense(xb, cw_t.T, exp_w_gate[i].astype(BF16), exp_w_up[i].astype(BF16),
                       exp_w_down[i].astype(BF16))
        xf, xb = _tail(xf, xb, y, p[i].reshape(t, -1),
                       sh_w_gate[i].astype(BF16), sh_w_up[i].astype(BF16), sh_w_down[i].astype(BF16),
                       ple_w_gate[i].astype(BF16), ple_w_proj[i].astype(BF16),
                       ln2_g[i].reshape(1, d), ln2_b[i].reshape(1, d), alpha)
    return xf.reshape(bsz, seq, d)
```

```python
import functools
import math

import numpy as np
import jax
import jax.numpy as jnp
from jax import lax
from jax.experimental import pallas as pl
from jax.experimental.pallas import tpu as pltpu

F32 = jnp.float32
BF16 = jnp.bfloat16

HEAD_DIM = 64
NSA_HEADS = 8
NSA_KV = 2
NSA_BLOCK = 64
NSA_TOPN = 16
NSA_LOCAL = 2
NSA_WINDOW = 512
SWA_HEADS = 8
SWA_KV = 2
SWA_WINDOW = 128
DIFF_HEADS = 4
N_EXPERTS = 64
TOP_K = 8
N_GROUPS = 8
TOPK_GROUPS = 4
ROUTED_SCALE = 2.5
LN_EPS = 1e-5
NEG = -1e30
FORCE = 1e4
MASK_NEG = -(2.0 ** 100)

LANE = 128
GROUP = NSA_HEADS // NSA_KV
NBLK_PAD = 128
VMEM_LIMIT = 56 * 1024 * 1024

IN_SIZES = (NSA_HEADS * HEAD_DIM,) + (NSA_KV * HEAD_DIM,) * 6 + (NSA_HEADS * 3,) + \
    (SWA_HEADS * HEAD_DIM, SWA_KV * HEAD_DIM, SWA_KV * HEAD_DIM) + \
    (DIFF_HEADS * 2 * HEAD_DIM,) * 3
IN_OFFS = np.concatenate([[0], np.cumsum(IN_SIZES)]).tolist()
GM_OFF = IN_OFFS[-1]


def _slopes(n):
    return [2.0 ** (-8.0 * (h + 1) / n) for h in range(n)]


class _Layout:
    def __init__(self):
        src, scale, bias, pa, pb = [], [], [], [], []

        def slot(cols, sc=1.0, consts=(), pos=False, width=LANE):
            s = [-1] * width
            c = [0.0] * width
            a = [0.0] * width
            b = [0.0] * width
            s[:len(cols)] = cols
            for off, val in consts:
                c[off] = val
            if pos:
                a[HEAD_DIM] = 1.0
                b[HEAD_DIM + 1] = 1.0
            start = len(src)
            src.extend(s)
            scale.extend([sc] * width)
            bias.extend(c)
            pa.extend(a)
            pb.extend(b)
            return start

        def rng(base, n):
            return list(range(base, base + n))

        qs = HEAD_DIM ** -0.5
        o = IN_OFFS
        sl8 = _slopes(NSA_HEADS)
        sl4 = _slopes(DIFF_HEADS)
        self.qa = len(src)
        for h in range(NSA_HEADS):
            slot(rng(o[0] + h * HEAD_DIM, HEAD_DIM), qs,
                 [(HEAD_DIM, sl8[h] * LANE), (HEAD_DIM + 1, sl8[h])])
        self.qb = len(src)
        for h in range(SWA_HEADS):
            slot(rng(o[8] + h * HEAD_DIM, HEAD_DIM), qs,
                 [(HEAD_DIM, sl8[h] * LANE), (HEAD_DIM + 1, sl8[h])])
        self.qd = len(src)
        for h in range(DIFF_HEADS):
            for r in range(2):
                slot(rng(o[11] + (h * 2 + r) * HEAD_DIM, HEAD_DIM), qs,
                     [(HEAD_DIM, sl4[h] * LANE), (HEAD_DIM + 1, sl4[h])])
        self.kd = len(src)
        for h in range(DIFF_HEADS):
            for r in range(2):
                slot(rng(o[12] + (h * 2 + r) * HEAD_DIM, HEAD_DIM), pos=True)
        self.vd = len(src)
        for h in range(DIFF_HEADS):
            slot(rng(o[13] + h * 2 * HEAD_DIM, 2 * HEAD_DIM),
                 consts=[(2 * HEAD_DIM, 1.0)], width=2 * LANE)

        def kv_slots(k_src, v_src):
            k_off = len(src)
            for k in range(NSA_KV):
                slot(rng(k_src + k * HEAD_DIM, HEAD_DIM), pos=True)
            v_off = len(src)
            for k in range(NSA_KV):
                slot(rng(v_src + k * HEAD_DIM, HEAD_DIM), consts=[(HEAD_DIM, 1.0)])
            return k_off, v_off

        self.ks, self.vs = kv_slots(o[3], o[4])
        self.kw, self.vw = kv_slots(o[5], o[6])
        self.kb, self.vb = kv_slots(o[9], o[10])
        self.kc = slot(rng(o[1], NSA_KV * HEAD_DIM))
        self.vc = slot(rng(o[2], NSA_KV * HEAD_DIM))
        self.ga = slot([o[7] + h * 3 + c for c in range(3) for h in range(NSA_HEADS)])
        self.width = len(src)
        self.src = np.asarray(src, np.int32)
        self.scale = np.asarray(scale, np.float32)
        aux = np.zeros((8, self.width), np.float32)
        aux[0] = bias
        aux[1] = pa
        aux[2] = pb
        self.aux = aux


_LAYOUT = _Layout()


def _cparams(*sem):
    return pltpu.CompilerParams(dimension_semantics=sem, vmem_limit_bytes=VMEM_LIMIT)


def _tile(n, pref):
    t = min(n, pref)
    assert n % t == 0, (n, pref)
    return t


def _iota(shape, dim):
    return lax.broadcasted_iota(jnp.int32, shape, dim)


def _dot(a, b):
    return jnp.dot(a, b, preferred_element_type=F32)


def _dot_nt(a, b):
    return lax.dot_general(a, b, (((1,), (1,)), ((), ())), preferred_element_type=F32)


def _proj_kernel(x_ref, w_ref, aux_ref, o_ref, *, seq, tm):
    acc = _dot(x_ref[...], w_ref[...])
    pos = (pl.program_id(0) * tm) % seq + _iota((tm, 1), 0)
    a = (pos >> 7).astype(F32)
    b = (pos & (LANE - 1)).astype(F32)
    aux = aux_ref[...]
    o_ref[...] = (acc + aux[0:1] + a * aux[1:2] + b * aux[2:3]).astype(o_ref.dtype)


def _project(xb, w, aux, seq, out_dtype, tn_pref):
    t, d = xb.shape
    n = w.shape[1]
    tm = _tile(t, 1024)
    tn = _tile(n, tn_pref)
    return pl.pallas_call(
        functools.partial(_proj_kernel, seq=seq, tm=tm),
        grid=(t // tm, n // tn),
        in_specs=[pl.BlockSpec((tm, d), lambda i, j: (i, 0)),
                  pl.BlockSpec((d, tn), lambda i, j: (0, j)),
                  pl.BlockSpec((8, tn), lambda i, j: (0, j))],
        out_specs=pl.BlockSpec((tm, tn), lambda i, j: (i, j)),
        out_shape=jax.ShapeDtypeStruct((t, n), out_dtype),
        compiler_params=_cparams("parallel", "parallel"),
        name="proj",
    )(xb, w, aux)


def _compress_kernel(x_ref, pe_ref, w1_ref, w2_ref, o_ref):
    xb = (x_ref[0].astype(F32) + pe_ref[0]).astype(BF16)
    h = jax.nn.gelu(_dot(xb, w1_ref[0]))
    o_ref[0] = _dot(h.astype(BF16), w2_ref[0])


def _compress(xs, pes, w1s, w2s):
    _, r, kdim = xs.shape
    hid = w1s.shape[2]
    dh = w2s.shape[2]
    return pl.pallas_call(
        _compress_kernel,
        grid=(2,),
        in_specs=[pl.BlockSpec((1, r, kdim), lambda i: (i, 0, 0)),
                  pl.BlockSpec((1, 1, kdim), lambda i: (i, 0, 0)),
                  pl.BlockSpec((1, kdim, hid), lambda i: (i, 0, 0)),
                  pl.BlockSpec((1, hid, dh), lambda i: (i, 0, 0))],
        out_specs=pl.BlockSpec((1, r, dh), lambda i: (i, 0, 0)),
        out_shape=jax.ShapeDtypeStruct((2, r, dh), F32),
        compiler_params=_cparams("parallel"),
        name="nsa_compress",
    )(xs, pes, w1s, w2s)


def _stack_heads(q, n):
    return jnp.concatenate([q[:, g * LANE:(g + 1) * LANE] for g in range(n)], axis=0)


def _unstack_heads(o, n, tq, width):
    return jnp.concatenate([o[g * tq:(g + 1) * tq, :width] for g in range(n)], axis=1)


def _cmp_kernel(q_ref, kc_ref, vc_ref, oc_ref, sel_ref, touch_ref, *, tq, n_sel):
    i = pl.program_id(2)
    qs = _stack_heads(q_ref[0], GROUP)
    s = _dot_nt(qs, kc_ref[0, 0])
    t1 = i * tq + _iota((tq, 1), 0)
    t = jnp.concatenate([t1] * GROUP, axis=0)
    n = _iota((1, NBLK_PAD), 1)
    valid = (n * NSA_BLOCK + (NSA_BLOCK - 1)) <= t
    s = jnp.where(valid, s, NEG)
    mx = jnp.max(s, axis=-1, keepdims=True)
    e = jnp.where(valid, jnp.exp(s - mx), 0.0)
    p = e / jnp.maximum(jnp.sum(e, axis=-1, keepdims=True), 1e-30)
    o = _dot(p.astype(BF16), vc_ref[0, 0])
    oc_ref[0] = _unstack_heads(o, GROUP, tq, HEAD_DIM)

    imp = p[0:tq]
    for g in range(1, GROUP):
        imp = imp + p[g * tq:(g + 1) * tq]
    nb = _iota((NBLK_PAD, 1), 0)
    rel = ((i * tq + _iota((1, tq), 1)) >> 6) - nb
    forced = (nb == 0) | ((rel >= 0) & (rel < NSA_LOCAL))
    val = jnp.where(forced, FORCE, jnp.where(rel >= 0, imp.T, -1.0))
    sel = jnp.zeros((NBLK_PAD, tq), F32)
    nf = nb.astype(F32)
    for _ in range(n_sel):
        mx = jnp.max(val, axis=0, keepdims=True)
        idx = jnp.min(jnp.where(val == mx, nf, float(NBLK_PAD)), axis=0, keepdims=True)
        pick = nf == idx
        sel = jnp.where(pick & (mx >= 0.0), 1.0, sel)
        val = jnp.where(pick, -jnp.inf, val)
    sel = sel.T
    sel_ref[0, 0] = (1.0 - sel).astype(BF16)
    touch_ref[0, 0, 0] = jnp.broadcast_to(jnp.max(sel, axis=0, keepdims=True), (8, NBLK_PAD))


def _nsa_compressed(proj, kca, vca, seq, n_sel):
    b = proj.shape[0]
    tq = _tile(seq, 256)
    lay = _LAYOUT
    gw = GROUP * LANE
    return pl.pallas_call(
        functools.partial(_cmp_kernel, tq=tq, n_sel=n_sel),
        grid=(b, NSA_KV, seq // tq),
        in_specs=[pl.BlockSpec((1, tq, gw), lambda b_, k, i: (b_, i, lay.qa // gw + k)),
                  pl.BlockSpec((1, 1, NBLK_PAD, LANE), lambda b_, k, i: (b_, k, 0, 0)),
                  pl.BlockSpec((1, 1, NBLK_PAD, LANE), lambda b_, k, i: (b_, k, 0, 0))],
        out_specs=[pl.BlockSpec((1, tq, GROUP * HEAD_DIM), lambda b_, k, i: (b_, i, k)),
                   pl.BlockSpec((1, 1, tq, NBLK_PAD), lambda b_, k, i: (b_, k, i, 0)),
                   pl.BlockSpec((1, 1, 1, 8, NBLK_PAD), lambda b_, k, i: (b_, k, i, 0, 0))],
        out_shape=[jax.ShapeDtypeStruct((b, seq, NSA_HEADS * HEAD_DIM), F32),
                   jax.ShapeDtypeStruct((b, NSA_KV, seq, NBLK_PAD), BF16),
                   jax.ShapeDtypeStruct((b, NSA_KV, seq // tq, 8, NBLK_PAD), F32)],
        compiler_params=_cparams("parallel", "parallel", "parallel"),
        name="nsa_compressed",
    )(proj, kca, vca)


def _flash(chunk_fns, v_ref, i, tq, tk, m_ref, acc_ref, s_refs, n_active=None, tile_at=None):
    m_ref[...] = jnp.full(m_ref.shape, NEG, F32)
    acc_ref[...] = jnp.zeros(acc_ref.shape, F32)
    acc_reps = acc_ref.shape[1] // LANE
    n_full = (i * tq) // tk
    if n_active is None:
        n_active = n_full
        tile_at = lambda n: n

    def nth_tile(n):
        return jnp.where(n < n_active, tile_at(n), n_full)

    def issue(n, s_ref):
        start = pl.multiple_of(nth_tile(n) * tk, tk)
        for c, fn in enumerate(chunk_fns):
            s_ref[c * tq:(c + 1) * tq, :] = fn(start)

    def accumulate(j, s_ref, visible):
        start = pl.multiple_of(j * tk, tk)
        vt = v_ref[pl.ds(start, tk), :]
        for c in range(len(chunk_fns)):
            rows = slice(c * tq, (c + 1) * tq)
            s = s_ref[rows, :]
            if visible is not None:
                s = jnp.where(visible, s, NEG)
            m_old = m_ref[rows, :]
            m_new = jnp.maximum(m_old, jnp.max(s, axis=-1, keepdims=True))
            p = jnp.exp((s - jnp.concatenate([m_new] * (tk // LANE), axis=1)).astype(BF16))
            alpha = jnp.concatenate([jnp.exp(m_old - m_new)] * acc_reps, axis=1)
            acc_ref[rows, :] = alpha * acc_ref[rows, :] + _dot(p, vt)
            m_ref[rows, :] = m_new

    s0, s1 = s_refs
    issue(0, s0)

    def body(pair, carry):
        n = 2 * pair
        issue(n + 1, s1)
        accumulate(nth_tile(n), s0, None)
        issue(n + 2, s0)
        accumulate(nth_tile(n + 1), s1, None)
        return carry

    pairs = n_active // 2
    lax.fori_loop(0, pairs, body, 0)
    visible = (_iota((tq, tk), 1) - _iota((tq, tk), 0)) <= i * tq - n_full * tk
    leftover = n_active - 2 * pairs

    @pl.when(leftover == 1)
    def _():
        issue(n_active, s1)
        accumulate(nth_tile(n_active - 1), s0, None)
        accumulate(n_full, s1, visible)

    @pl.when(leftover == 0)
    def _():
        accumulate(n_full, s0, visible)


def _sel_kernel(order_ref, count_ref, q_ref, sel_ref, k_ref, v_ref, o_ref, qa_ref, m_ref, acc_ref,
                s0_ref, s1_ref, *, tq, tk, n_tiles):
    i = pl.program_id(2)
    tile_id = (pl.program_id(0) * pl.num_programs(1) + pl.program_id(1)) * pl.num_programs(2) + i
    q = q_ref[0]
    sc = sel_ref[0, 0]
    for g in range(GROUP):
        qa_ref[g * tq:(g + 1) * tq, 0:LANE] = q[:, g * LANE:(g + 1) * LANE]
        qa_ref[g * tq:(g + 1) * tq, LANE:2 * LANE] = sc
    kv = k_ref.at[0, 0]

    def head_scores(g):
        return lambda start: _dot_nt(qa_ref[g * tq:(g + 1) * tq, :], kv[pl.ds(start, tk), :])

    _flash([head_scores(g) for g in range(GROUP)], v_ref.at[0], i, tq, tk, m_ref, acc_ref, (s0_ref, s1_ref),
           n_active=count_ref[tile_id], tile_at=lambda n: order_ref[tile_id * n_tiles + n])
    acc = acc_ref[...]
    o = acc / acc[:, HEAD_DIM:HEAD_DIM + 1]
    o_ref[0] = _unstack_heads(o, GROUP, tq, HEAD_DIM)


def _nsa_selected(proj, selc, touch, ksel, seq):
    b = proj.shape[0]
    tq = _tile(seq, 256)
    tk = _tile(seq, 512)
    n_tiles = seq // tk
    lay = _LAYOUT
    gw = GROUP * LANE
    per_tile = touch[:, :, :, 0, :seq // NSA_BLOCK].reshape(b, NSA_KV, seq // tq, n_tiles, tk // NSA_BLOCK)
    n_full = (np.arange(seq // tq) * tq) // tk
    active = (jnp.max(per_tile, axis=-1) > 0.0) & jnp.asarray(np.arange(n_tiles)[None, :] < n_full[:, None])
    order = jnp.argsort(jnp.where(active, 0, 1), axis=-1, stable=True).astype(jnp.int32).reshape(-1)
    count = jnp.sum(active, axis=-1).astype(jnp.int32).reshape(-1)
    grid_spec = pltpu.PrefetchScalarGridSpec(
        num_scalar_prefetch=2,
        grid=(b, NSA_KV, seq // tq),
        in_specs=[pl.BlockSpec((1, tq, gw), lambda b_, k, i, o, c: (b_, i, lay.qa // gw + k)),
                  pl.BlockSpec((1, 1, tq, NBLK_PAD), lambda b_, k, i, o, c: (b_, k, i, 0)),
                  pl.BlockSpec((1, 1, seq, 2 * LANE), lambda b_, k, i, o, c: (b_, k, 0, 0)),
                  pl.BlockSpec((1, seq, LANE), lambda b_, k, i, o, c: (b_, 0, lay.vs // LANE + k))],
        out_specs=pl.BlockSpec((1, tq, GROUP * HEAD_DIM), lambda b_, k, i, o, c: (b_, i, k)),
        scratch_shapes=[pltpu.VMEM((GROUP * tq, 2 * LANE), BF16),
                        pltpu.VMEM((GROUP * tq, LANE), F32),
                        pltpu.VMEM((GROUP * tq, LANE), F32),
                        pltpu.VMEM((GROUP * tq, tk), F32),
                        pltpu.VMEM((GROUP * tq, tk), F32)])
    return pl.pallas_call(
        functools.partial(_sel_kernel, tq=tq, tk=tk, n_tiles=n_tiles),
        grid_spec=grid_spec,
        out_shape=jax.ShapeDtypeStruct((b, seq, NSA_HEADS * HEAD_DIM), F32),
        compiler_params=_cparams("parallel", "parallel", "arbitrary"),
        name="nsa_selected",
    )(order, count, proj, selc, ksel, proj)


def _diff_kernel(lam_ref, q_ref, k_ref, v_ref, g_ref, o_ref, m_ref, acc_ref, s0_ref, s1_ref,
                 *, tq, tk, lambda_init):
    i = pl.program_id(2)
    kv = k_ref.at[0]

    def map_scores(r):
        cols = slice(r * LANE, (r + 1) * LANE)
        return lambda start: _dot_nt(q_ref[0, :, cols], kv[pl.ds(start, tk), cols])

    _flash([map_scores(0), map_scores(1)], v_ref.at[0], i, tq, tk, m_ref, acc_ref, (s0_ref, s1_ref))
    acc = acc_ref[...]
    dv = 2 * HEAD_DIM
    a = acc[:, :dv] / acc[:, dv:dv + 1]
    lv = lam_ref[...]
    lam = (jnp.exp(jnp.sum(lv[0:1] * lv[1:2], axis=-1, keepdims=True))
           - jnp.exp(jnp.sum(lv[2:3] * lv[3:4], axis=-1, keepdims=True)) + lambda_init)
    o = a[0:tq] - lam * a[tq:2 * tq]
    o = o * lax.rsqrt(jnp.mean(o * o, axis=-1, keepdims=True) + LN_EPS) * g_ref[...]
    o_ref[0] = (o * (1.0 - lambda_init)).astype(o_ref.dtype)


def _diff_attention(proj, lam_vecs, norm_g, seq, lambda_init):
    b = proj.shape[0]
    tq = _tile(seq, 512)
    tk = _tile(seq, 512)
    lay = _LAYOUT
    w2 = 2 * LANE
    return pl.pallas_call(
        functools.partial(_diff_kernel, tq=tq, tk=tk, lambda_init=lambda_init),
        grid=(b, DIFF_HEADS, seq // tq),
        in_specs=[pl.BlockSpec((4, HEAD_DIM), lambda b_, h, i: (0, 0)),
                  pl.BlockSpec((1, tq, w2), lambda b_, h, i: (b_, i, lay.qd // w2 + h)),
                  pl.BlockSpec((1, seq, w2), lambda b_, h, i: (b_, 0, lay.kd // w2 + h)),
                  pl.BlockSpec((1, seq, w2), lambda b_, h, i: (b_, 0, lay.vd // w2 + h)),
                  pl.BlockSpec((1, 2 * HEAD_DIM), lambda b_, h, i: (0, 0))],
        out_specs=pl.BlockSpec((1, tq, 2 * HEAD_DIM), lambda b_, h, i: (b_, i, h)),
        out_shape=jax.ShapeDtypeStruct((b, seq, DIFF_HEADS * 2 * HEAD_DIM), BF16),
        scratch_shapes=[pltpu.VMEM((2 * tq, LANE), F32),
                        pltpu.VMEM((2 * tq, w2), F32),
                        pltpu.VMEM((2 * tq, tk), F32),
                        pltpu.VMEM((2 * tq, tk), F32)],
        compiler_params=_cparams("parallel", "parallel", "arbitrary"),
        name="diff_attention",
    )(lam_vecs, proj, proj, proj, norm_g)


def _band_kernel(hp_ref, q_ref, k_ref, v_ref, o_ref, *, tq, window, band, use_sinks):
    kvh = pl.program_id(1)
    i = pl.program_id(2)
    start = pl.multiple_of(jnp.maximum(i * tq - window, 0), LANE)
    kb = k_ref[0, pl.ds(start, band), :]
    vb = v_ref[0, pl.ds(start, band), :]
    t1 = i * tq + _iota((tq, 1), 0)
    dist = t1 - (start + _iota((1, band), 1))
    visible = (dist >= 0) & (dist < window)
    outs = []
    for g in range(GROUP):
        s = jnp.where(visible, _dot_nt(q_ref[0, :, g * LANE:(g + 1) * LANE], kb), NEG)
        mx = jnp.max(s, axis=-1, keepdims=True)
        if use_sinks:
            h = kvh * GROUP + g
            sink = hp_ref[1, h] + hp_ref[0, h] * t1.astype(F32)
            mx = jnp.maximum(mx, sink)
        acc = _dot(jnp.exp((s - mx).astype(BF16)), vb)
        den = acc[:, HEAD_DIM:HEAD_DIM + 1]
        if use_sinks:
            den = den + jnp.exp(sink - mx)
        outs.append((acc / den)[:, :HEAD_DIM])
    o_ref[0] = jnp.concatenate(outs, axis=1).astype(o_ref.dtype)


def _banded(proj, head_params, seq, window, q_off, k_off, v_off, use_sinks, out_dtype, name):
    b = proj.shape[0]
    tq = _tile(seq, 256)
    band = min(window + tq, seq)
    gw = GROUP * LANE
    return pl.pallas_call(
        functools.partial(_band_kernel, tq=tq, window=window, band=band, use_sinks=use_sinks),
        grid=(b, NSA_KV, seq // tq),
        in_specs=[pl.BlockSpec(memory_space=pltpu.SMEM),
                  pl.BlockSpec((1, tq, gw), lambda b_, k, i: (b_, i, q_off // gw + k)),
                  pl.BlockSpec((1, seq, LANE), lambda b_, k, i: (b_, 0, k_off // LANE + k)),
                  pl.BlockSpec((1, seq, LANE), lambda b_, k, i: (b_, 0, v_off // LANE + k))],
        out_specs=pl.BlockSpec((1, tq, GROUP * HEAD_DIM), lambda b_, k, i: (b_, i, k)),
        out_shape=jax.ShapeDtypeStruct((b, seq, NSA_HEADS * HEAD_DIM), out_dtype),
        compiler_params=_cparams("parallel", "parallel", "parallel"),
        name=name,
    )(head_params, proj, proj, proj)


def _layer_norm(z, g, b):
    mu = jnp.mean(z, axis=-1, keepdims=True)
    zc = z - mu
    var = jnp.mean(zc * zc, axis=-1, keepdims=True)
    return zc * lax.rsqrt(var + LN_EPS) * g + b


def _merge_kernel(x_ref, xin_ref, oc_ref, os_ref, ow_ref, ga_ref, ob_ref, od_ref, wgm_ref, ex_ref,
                  wa_ref, wb_ref, wc_ref, wo_ref, lng_ref, lnb_ref, xo_ref, xb_ref, *, alpha):
    d = x_ref.shape[1]
    ga = jax.nn.sigmoid(ga_ref[...].astype(F32)).astype(BF16)
    oa = (_dot(ga, ex_ref[0]) * oc_ref[...] + _dot(ga, ex_ref[1]) * os_ref[...]
          + _dot(ga, ex_ref[2]) * ow_ref[...])
    gm = jax.nn.sigmoid(_dot(xin_ref[...], wgm_ref[...]))
    merged = (gm[:, 0:d] * _dot(oa.astype(BF16), wa_ref[...])
              + gm[:, d:2 * d] * _dot(ob_ref[...], wb_ref[...])
              + gm[:, 2 * d:3 * d] * _dot(od_ref[...], wc_ref[...]))
    z = alpha * x_ref[...] + _dot(merged.astype(BF16), wo_ref[...])
    y = _layer_norm(z, lng_ref[...], lnb_ref[...])
    xo_ref[...] = y
    xb_ref[...] = y.astype(BF16)


def _merge(x, xb, oc, os_, ow, proj2d, ob, od, wgm, expand, wa, wb, wc, wo, lng, lnb, alpha):
    t, d = x.shape
    tm = _tile(t, 256)
    ha = NSA_HEADS * HEAD_DIM
    row = lambda w: pl.BlockSpec((tm, w), lambda i: (i, 0))
    full = lambda a: pl.BlockSpec(a.shape, lambda i: (0,) * a.ndim)
    return pl.pallas_call(
        functools.partial(_merge_kernel, alpha=alpha),
        grid=(t // tm,),
        in_specs=[row(d), row(d), row(ha), row(ha), row(ha),
                  pl.BlockSpec((tm, LANE), lambda i: (i, _LAYOUT.ga // LANE)),
                  row(ha), row(ha), full(wgm), full(expand),
                  full(wa), full(wb), full(wc), full(wo), full(lng), full(lnb)],
        out_specs=[row(d), row(d)],
        out_shape=[jax.ShapeDtypeStruct((t, d), F32), jax.ShapeDtypeStruct((t, d), BF16)],
        compiler_params=_cparams("parallel"),
        name="merge_norm",
    )(x, xb, oc, os_, ow, proj2d, ob, od, wgm, expand, wa, wb, wc, wo, lng, lnb)


def _first_max(v, idx, n):
    mx = jnp.max(v, axis=0, keepdims=True)
    first = jnp.min(jnp.where(v == mx, idx, n), axis=0, keepdims=True)
    return mx, idx == first


def _router_kernel(x_ref, rw_ref, rb_ref, cw_ref):
    gsz = N_EXPERTS // N_GROUPS
    logits = lax.dot_general(rw_ref[...], x_ref[...], (((1,), (1,)), ((), ())),
                             precision=lax.Precision.HIGHEST,
                             preferred_element_type=F32)
    tm = logits.shape[1]
    scores = jax.nn.sigmoid(logits)
    biased = scores + rb_ref[...]
    iw = _iota((gsz, tm), 0)
    gs = []
    for g in range(N_GROUPS):
        blk = biased[g * gsz:(g + 1) * gsz]
        m1, pick = _first_max(blk, iw, gsz)
        m2 = jnp.max(jnp.where(pick, -jnp.inf, blk), axis=0, keepdims=True)
        gs.append(m1 + m2)
    gs = jnp.concatenate(gs, axis=0)
    ig = _iota((N_GROUPS, tm), 0)
    gmask = jnp.zeros((N_GROUPS, tm), F32)
    for _ in range(TOPK_GROUPS):
        _, pick = _first_max(gs, ig, N_GROUPS)
        gmask = jnp.where(pick, 1.0, gmask)
        gs = jnp.where(pick, -jnp.inf, gs)
    emask = jnp.concatenate(
        [jnp.broadcast_to(gmask[g:g + 1], (gsz, tm)) for g in range(N_GROUPS)], axis=0)
    cand = jnp.where(emask > 0.5, biased, NEG)
    ie = _iota((N_EXPERTS, tm), 0)
    chosen = jnp.zeros((N_EXPERTS, tm), F32)
    for _ in range(TOP_K):
        _, pick = _first_max(cand, ie, N_EXPERTS)
        chosen = jnp.where(pick, 1.0, chosen)
        cand = jnp.where(pick, -jnp.inf, cand)
    w = scores * chosen
    cw_ref[...] = w / jnp.sum(w, axis=0, keepdims=True) * ROUTED_SCALE


def _router(x, rw_t, rb):
    t, d = x.shape
    tm = _tile(t, 512)
    return pl.pallas_call(
        _router_kernel,
        grid=(t // tm,),
        in_specs=[pl.BlockSpec((tm, d), lambda i: (i, 0)),
                  pl.BlockSpec((N_EXPERTS, d), lambda i: (0, 0)),
                  pl.BlockSpec((N_EXPERTS, 1), lambda i: (0, 0))],
        out_specs=pl.BlockSpec((N_EXPERTS, tm), lambda i: (0, i)),
        out_shape=jax.ShapeDtypeStruct((N_EXPERTS, t), F32),
        compiler_params=_cparams("parallel"),
        name="router",
    )(x, rw_t, rb)


def _moe_kernel(x_ref, cw_ref, wg_ref, wu_ref, wd_ref, y_ref, *, ne):
    c = pl.program_id(1)
    x = x_ref[...]
    edim = wg_ref.shape[2]
    cw = cw_ref[...]
    cw_hi = cw.astype(BF16)
    cw_lo = (cw - cw_hi.astype(F32)).astype(BF16)
    expand = jnp.concatenate(
        [jnp.where(_iota((N_EXPERTS, edim), 0) == c * ne + e, 1.0, 0.0) for e in range(ne)],
        axis=1).astype(BF16)
    scale = _dot(cw_hi, expand) + _dot(cw_lo, expand)
    hs = []
    for e in range(ne):
        hs.append(jax.nn.silu(_dot(x, wg_ref[e])) * _dot(x, wu_ref[e]))
    h = (jnp.concatenate(hs, axis=1) * scale).astype(BF16)
    y = _dot(h, wd_ref[...].reshape(ne * edim, wd_ref.shape[2]))

    @pl.when(c == 0)
    def _():
        y_ref[...] = y

    @pl.when(c != 0)
    def _():
        y_ref[...] += y


def _moe_dense(xb, cw, wg, wu, wd):
    t, d = xb.shape
    edim = wg.shape[2]
    tm = _tile(t, 1024)
    ne = 4
    return pl.pallas_call(
        functools.partial(_moe_kernel, ne=ne),
        grid=(t // tm, N_EXPERTS // ne),
        in_specs=[pl.BlockSpec((tm, d), lambda i, c: (i, 0)),
                  pl.BlockSpec((tm, N_EXPERTS), lambda i, c: (i, 0)),
                  pl.BlockSpec((ne, d, edim), lambda i, c: (c, 0, 0)),
                  pl.BlockSpec((ne, d, edim), lambda i, c: (c, 0, 0)),
                  pl.BlockSpec((ne, edim, d), lambda i, c: (c, 0, 0))],
        out_specs=pl.BlockSpec((tm, d), lambda i, c: (i, 0)),
        out_shape=jax.ShapeDtypeStruct((t, d), F32),
        compiler_params=_cparams("parallel", "arbitrary"),
        name="moe_experts",
    )(xb, cw, wg, wu, wd)


def _tail_kernel(x_ref, xb_ref, y_ref, p_ref, sg_ref, su_ref, sd_ref, pg_ref, pp_ref,
                 lng_ref, lnb_ref, xo_ref, xbo_ref, *, alpha):
    xb = xb_ref[...]
    h = jax.nn.silu(_dot(xb, sg_ref[...])) * _dot(xb, su_ref[...])
    shared = _dot(h.astype(BF16), sd_ref[...])
    ple = jax.nn.sigmoid(_dot(xb, pg_ref[...])) * _dot(p_ref[...].astype(BF16), pp_ref[...])
    z = alpha * x_ref[...] + y_ref[...] + shared + ple
    y = _layer_norm(z, lng_ref[...], lnb_ref[...])
    xo_ref[...] = y
    xbo_ref[...] = y.astype(BF16)


def _tail(x, xb, y, p, sg, su, sd, pg, pp, lng, lnb, alpha):
    t, d = x.shape
    tm = _tile(t, 256)
    row = lambda w: pl.BlockSpec((tm, w), lambda i: (i, 0))
    full = lambda a: pl.BlockSpec(a.shape, lambda i: (0,) * a.ndim)
    return pl.pallas_call(
        functools.partial(_tail_kernel, alpha=alpha),
        grid=(t // tm,),
        in_specs=[row(d), row(d), row(d), row(p.shape[1]), full(sg), full(su), full(sd),
                  full(pg), full(pp), full(lng), full(lnb)],
        out_specs=[row(d), row(d)],
        out_shape=[jax.ShapeDtypeStruct((t, d), F32), jax.ShapeDtypeStruct((t, d), BF16)],
        compiler_params=_cparams("parallel"),
        name="ffn_tail_norm",
    )(x, xb, y, p, sg, su, sd, pg, pp, lng, lnb)


def _slot_weights(w):
    d = w.shape[0]
    o = IN_OFFS
    qs = HEAD_DIM ** -0.5

    def heads(off, n, width=HEAD_DIM, slot=LANE, scale=1.0):
        blk = w[:, off:off + n * width].reshape(d, n, width) * scale
        return jnp.pad(blk, ((0, 0), (0, 0), (0, slot - width))).reshape(d, n * slot)

    gates = w[:, o[7]:o[8]].reshape(d, NSA_HEADS, 3).transpose(0, 2, 1).reshape(d, 3 * NSA_HEADS)
    parts = [heads(o[0], NSA_HEADS, scale=qs), heads(o[8], SWA_HEADS, scale=qs),
             heads(o[11], 2 * DIFF_HEADS, scale=qs), heads(o[12], 2 * DIFF_HEADS),
             heads(o[13], DIFF_HEADS, 2 * HEAD_DIM, 2 * LANE),
             heads(o[3], NSA_KV), heads(o[4], NSA_KV), heads(o[5], NSA_KV), heads(o[6], NSA_KV),
             heads(o[9], SWA_KV), heads(o[10], SWA_KV),
             w[:, o[1]:o[2]], w[:, o[2]:o[3]],
             jnp.pad(gates, ((0, 0), (0, LANE - 3 * NSA_HEADS)))]
    out = jnp.concatenate(parts, axis=1)
    assert out.shape[1] == _LAYOUT.width
    return out.astype(BF16)


def _compressed_slots(c, pos_cols):
    b, nb, hkv, dh = c.shape
    c = c.transpose(0, 2, 1, 3)
    extra = jnp.zeros((b, hkv, nb, LANE - dh), F32)
    if pos_cols:
        cpos = np.arange(nb) * NSA_BLOCK + NSA_BLOCK - 1
        cols = np.zeros((nb, LANE - dh), np.float32)
        cols[:, 0] = cpos // LANE
        cols[:, 1] = cpos % LANE
        extra = extra + jnp.asarray(cols)
    c = jnp.concatenate([c, extra], axis=-1)
    c = jnp.pad(c, ((0, 0), (0, 0), (0, NBLK_PAD - nb), (0, 0)))
    return c.astype(BF16)


def kernel(x, p, w_in, nsa_pe_k, nsa_w1_k, nsa_w2_k, nsa_pe_v, nsa_w1_v, nsa_w2_v, swa_sinks,
           diff_lq1, diff_lk1, diff_lq2, diff_lk2, diff_norm_g, w_branch_a, w_branch_b, w_branch_c,
           w_out, ln1_g, ln1_b, router_w, router_bias, exp_w_gate, exp_w_up, exp_w_down,
           sh_w_gate, sh_w_up, sh_w_down, ple_w_proj, ple_w_gate, ln2_g, ln2_b):
    bsz, seq, d = x.shape
    depth = w_in.shape[0]
    t = bsz * seq
    alpha = (2 * depth) ** 0.25
    lay = _LAYOUT
    nb = seq // NSA_BLOCK
    assert nb <= NBLK_PAD and seq % LANE == 0
    n_sel = min(NSA_TOPN, nb)

    aux = jnp.asarray(lay.aux)
    blk_mask = jnp.asarray(
        (np.arange(seq)[:, None] // NSA_BLOCK == np.arange(NBLK_PAD)[None, :]) * MASK_NEG, BF16)
    ex = np.zeros((3, LANE, NSA_HEADS * HEAD_DIM), np.float32)
    for c in range(3):
        for h in range(NSA_HEADS):
            ex[c, c * NSA_HEADS + h, h * HEAD_DIM:(h + 1) * HEAD_DIM] = 1.0
    expand = jnp.asarray(ex, BF16)
    slopes8 = jnp.asarray(_slopes(NSA_HEADS), F32)

    xf = x.reshape(t, d)
    xb = xf.astype(BF16)
    for i in range(depth):
        w_slots = _slot_weights(w_in[i])
        proj2d = _project(xb, w_slots, aux, seq, BF16, 1408)
        proj = proj2d.reshape(bsz, seq, lay.width)

        def blocks(off):
            c = proj[:, :, off:off + LANE].reshape(bsz, nb, NSA_BLOCK, NSA_KV, HEAD_DIM)
            return c.transpose(0, 1, 3, 2, 4).reshape(bsz * nb * NSA_KV, NSA_BLOCK * HEAD_DIM)
        cmp = _compress(
            jnp.stack([blocks(lay.kc), blocks(lay.vc)]),
            jnp.stack([nsa_pe_k[i].reshape(1, -1), nsa_pe_v[i].reshape(1, -1)]),
            jnp.stack([nsa_w1_k[i], nsa_w1_v[i]]).astype(BF16),
            jnp.stack([nsa_w2_k[i], nsa_w2_v[i]]).astype(BF16))
        cmp = cmp.reshape(2, bsz, nb, NSA_KV, HEAD_DIM)
        kca = _compressed_slots(cmp[0], True)
        vca = _compressed_slots(cmp[1], False)

        o_cmp, selc, touch = _nsa_compressed(proj, kca, vca, seq, n_sel)
        ks = proj[:, :, lay.ks:lay.ks + NSA_KV * LANE].reshape(bsz, seq, NSA_KV, LANE).transpose(0, 2, 1, 3)
        ksel = jnp.concatenate(
            [ks, jnp.broadcast_to(blk_mask[None, None], (bsz, NSA_KV, seq, NBLK_PAD))], axis=-1)
        o_sel = _nsa_selected(proj, selc, touch, ksel, seq)
        hp_a = jnp.stack([slopes8, jnp.zeros_like(slopes8)])
        o_win = _banded(proj, hp_a, seq, NSA_WINDOW, lay.qa, lay.kw, lay.vw, False, F32, "nsa_window")
        hp_b = jnp.stack([slopes8, swa_sinks[i].astype(F32)])
        o_b = _banded(proj, hp_b, seq, SWA_WINDOW, lay.qb, lay.kb, lay.vb, True, BF16, "swa_sinks")

        lambda_init = 0.8 - 0.6 * math.exp(-0.3 * i)
        lam_vecs = jnp.stack([diff_lq1[i], diff_lk1[i], diff_lq2[i], diff_lk2[i]]).astype(F32)
        o_d = _diff_attention(proj, lam_vecs, diff_norm_g[i].reshape(1, -1), seq, lambda_init)

        ha = NSA_HEADS * HEAD_DIM
        xf, xb = _merge(
            xf, xb, o_cmp.reshape(t, ha), o_sel.reshape(t, ha), o_win.reshape(t, ha), proj2d,
            o_b.reshape(t, ha), o_d.reshape(t, ha), w_in[i][:, GM_OFF:].astype(BF16), expand,
            w_branch_a[i].astype(BF16), w_branch_b[i].astype(BF16), w_branch_c[i].astype(BF16),
            w_out[i].astype(BF16), ln1_g[i].reshape(1, d), ln1_b[i].reshape(1, d), alpha)

        cw_t = _router(xf, router_w[i].T, router_bias[i].reshape(-1, 1))
        y = _moe_dense(xb, cw_t.T, exp_w_gate[i].astype(BF16), exp_w_up[i].astype(BF16),
                       exp_w_down[i].astype(BF16))
        xf, xb = _tail(xf, xb, y, p[i].reshape(t, -1),
                       sh_w_gate[i].astype(BF16), sh_w_up[i].astype(BF16), sh_w_down[i].astype(BF16),
                       ple_w_gate[i].astype(BF16), ple_w_proj[i].astype(BF16),
                       ln2_g[i].reshape(1, d), ln2_b[i].reshape(1, d), alpha)
    return xf.reshape(bsz, seq, d)
```

```python
import functools
import math

import numpy as np
import jax
import jax.numpy as jnp
from jax import lax
from jax.experimental import pallas as pl
from jax.experimental.pallas import tpu as pltpu

F32 = jnp.float32
BF16 = jnp.bfloat16

HEAD_DIM = 64
NSA_HEADS = 8
NSA_KV = 2
NSA_BLOCK = 64
NSA_TOPN = 16
NSA_LOCAL = 2
NSA_WINDOW = 512
SWA_HEADS = 8
SWA_KV = 2
SWA_WINDOW = 128
DIFF_HEADS = 4
N_EXPERTS = 64
TOP_K = 8
N_GROUPS = 8
TOPK_GROUPS = 4
ROUTED_SCALE = 2.5
LN_EPS = 1e-5
NEG = -1e30
FORCE = 1e4
MASK_NEG = -(2.0 ** 100)

LANE = 128
GROUP = NSA_HEADS // NSA_KV
NBLK_PAD = 128
VMEM_LIMIT = 56 * 1024 * 1024

IN_SIZES = (NSA_HEADS * HEAD_DIM,) + (NSA_KV * HEAD_DIM,) * 6 + (NSA_HEADS * 3,) + \
    (SWA_HEADS * HEAD_DIM, SWA_KV * HEAD_DIM, SWA_KV * HEAD_DIM) + \
    (DIFF_HEADS * 2 * HEAD_DIM,) * 3
IN_OFFS = np.concatenate([[0], np.cumsum(IN_SIZES)]).tolist()
GM_OFF = IN_OFFS[-1]


def _slopes(n):
    return [2.0 ** (-8.0 * (h + 1) / n) for h in range(n)]


class _Layout:
    def __init__(self):
        src, scale, bias, pa, pb = [], [], [], [], []

        def slot(cols, sc=1.0, consts=(), pos=False, width=LANE):
            s = [-1] * width
            c = [0.0] * width
            a = [0.0] * width
            b = [0.0] * width
            s[:len(cols)] = cols
            for off, val in consts:
                c[off] = val
            if pos:
                a[HEAD_DIM] = 1.0
                b[HEAD_DIM + 1] = 1.0
            start = len(src)
            src.extend(s)
            scale.extend([sc] * width)
            bias.extend(c)
            pa.extend(a)
            pb.extend(b)
            return start

        def rng(base, n):
            return list(range(base, base + n))

        qs = HEAD_DIM ** -0.5
        o = IN_OFFS
        sl8 = _slopes(NSA_HEADS)
        sl4 = _slopes(DIFF_HEADS)
        self.qa = len(src)
        for h in range(NSA_HEADS):
            slot(rng(o[0] + h * HEAD_DIM, HEAD_DIM), qs,
                 [(HEAD_DIM, sl8[h] * LANE), (HEAD_DIM + 1, sl8[h])])
        self.qb = len(src)
        for h in range(SWA_HEADS):
            slot(rng(o[8] + h * HEAD_DIM, HEAD_DIM), qs,
                 [(HEAD_DIM, sl8[h] * LANE), (HEAD_DIM + 1, sl8[h])])
        self.qd = len(src)
        for h in range(DIFF_HEADS):
            for r in range(2):
                slot(rng(o[11] + (h * 2 + r) * HEAD_DIM, HEAD_DIM), qs,
                     [(HEAD_DIM, sl4[h] * LANE), (HEAD_DIM + 1, sl4[h])])
        self.kd = len(src)
        for h in range(DIFF_HEADS):
            for r in range(2):
                slot(rng(o[12] + (h * 2 + r) * HEAD_DIM, HEAD_DIM), pos=True)
        self.vd = len(src)
        for h in range(DIFF_HEADS):
            slot(rng(o[13] + h * 2 * HEAD_DIM, 2 * HEAD_DIM),
                 consts=[(2 * HEAD_DIM, 1.0)], width=2 * LANE)

        def kv_slots(k_src, v_src):
            k_off = len(src)
            for k in range(NSA_KV):
                slot(rng(k_src + k * HEAD_DIM, HEAD_DIM), pos=True)
            v_off = len(src)
            for k in range(NSA_KV):
                slot(rng(v_src + k * HEAD_DIM, HEAD_DIM), consts=[(HEAD_DIM, 1.0)])
            return k_off, v_off

        self.ks, self.vs = kv_slots(o[3], o[4])
        self.kw, self.vw = kv_slots(o[5], o[6])
        self.kb, self.vb = kv_slots(o[9], o[10])
        self.kc = slot(rng(o[1], NSA_KV * HEAD_DIM))
        self.vc = slot(rng(o[2], NSA_KV * HEAD_DIM))
        self.ga = slot([o[7] + h * 3 + c for c in range(3) for h in range(NSA_HEADS)])
        self.width = len(src)
        self.src = np.asarray(src, np.int32)
        self.scale = np.asarray(scale, np.float32)
        aux = np.zeros((8, self.width), np.float32)
        aux[0] = bias
        aux[1] = pa
        aux[2] = pb
        self.aux = aux


_LAYOUT = _Layout()


def _cparams(*sem):
    return pltpu.CompilerParams(dimension_semantics=sem, vmem_limit_bytes=VMEM_LIMIT)


def _tile(n, pref):
    t = min(n, pref)
    assert n % t == 0, (n, pref)
    return t


def _iota(shape, dim):
    return lax.broadcasted_iota(jnp.int32, shape, dim)


def _dot(a, b):
    return jnp.dot(a, b, preferred_element_type=F32)


def _dot_nt(a, b):
    return lax.dot_general(a, b, (((1,), (1,)), ((), ())), preferred_element_type=F32)


def _proj_kernel(x_ref, w_ref, aux_ref, o_ref, *, seq, tm):
    acc = _dot(x_ref[...], w_ref[...])
    pos = (pl.program_id(0) * tm) % seq + _iota((tm, 1), 0)
    a = (pos >> 7).astype(F32)
    b = (pos & (LANE - 1)).astype(F32)
    aux = aux_ref[...]
    o_ref[...] = (acc + aux[0:1] + a * aux[1:2] + b * aux[2:3]).astype(o_ref.dtype)


def _project(xb, w, aux, seq, out_dtype, tn_pref):
    t, d = xb.shape
    n = w.shape[1]
    tm = _tile(t, 1024)
    tn = _tile(n, tn_pref)
    return pl.pallas_call(
        functools.partial(_proj_kernel, seq=seq, tm=tm),
        grid=(t // tm, n // tn),
        in_specs=[pl.BlockSpec((tm, d), lambda i, j: (i, 0)),
                  pl.BlockSpec((d, tn), lambda i, j: (0, j)),
                  pl.BlockSpec((8, tn), lambda i, j: (0, j))],
        out_specs=pl.BlockSpec((tm, tn), lambda i, j: (i, j)),
        out_shape=jax.ShapeDtypeStruct((t, n), out_dtype),
        compiler_params=_cparams("parallel", "parallel"),
        name="proj",
    )(xb, w, aux)


def _compress_kernel(x_ref, pe_ref, w1_ref, w2_ref, o_ref):
    xb = (x_ref[0].astype(F32) + pe_ref[0]).astype(BF16)
    h = jax.nn.gelu(_dot(xb, w1_ref[0]))
    o_ref[0] = _dot(h.astype(BF16), w2_ref[0])


def _compress(xs, pes, w1s, w2s):
    _, r, kdim = xs.shape
    hid = w1s.shape[2]
    dh = w2s.shape[2]
    return pl.pallas_call(
        _compress_kernel,
        grid=(2,),
        in_specs=[pl.BlockSpec((1, r, kdim), lambda i: (i, 0, 0)),
                  pl.BlockSpec((1, 1, kdim), lambda i: (i, 0, 0)),
                  pl.BlockSpec((1, kdim, hid), lambda i: (i, 0, 0)),
                  pl.BlockSpec((1, hid, dh), lambda i: (i, 0, 0))],
        out_specs=pl.BlockSpec((1, r, dh), lambda i: (i, 0, 0)),
        out_shape=jax.ShapeDtypeStruct((2, r, dh), F32),
        compiler_params=_cparams("parallel"),
        name="nsa_compress",
    )(xs, pes, w1s, w2s)


def _stack_heads(q, n):
    return jnp.concatenate([q[:, g * LANE:(g + 1) * LANE] for g in range(n)], axis=0)


def _unstack_heads(o, n, tq, width):
    return jnp.concatenate([o[g * tq:(g + 1) * tq, :width] for g in range(n)], axis=1)


def _cmp_kernel(q_ref, kc_ref, vc_ref, oc_ref, sel_ref, touch_ref, *, tq, n_sel):
    i = pl.program_id(2)
    qs = _stack_heads(q_ref[0], GROUP)
    s = _dot_nt(qs, kc_ref[0, 0])
    t1 = i * tq + _iota((tq, 1), 0)
    t = jnp.concatenate([t1] * GROUP, axis=0)
    n = _iota((1, NBLK_PAD), 1)
    valid = (n * NSA_BLOCK + (NSA_BLOCK - 1)) <= t
    s = jnp.where(valid, s, NEG)
    mx = jnp.max(s, axis=-1, keepdims=True)
    e = jnp.where(valid, jnp.exp(s - mx), 0.0)
    p = e / jnp.maximum(jnp.sum(e, axis=-1, keepdims=True), 1e-30)
    o = _dot(p.astype(BF16), vc_ref[0, 0])
    oc_ref[0] = _unstack_heads(o, GROUP, tq, HEAD_DIM)

    imp = p[0:tq]
    for g in range(1, GROUP):
        imp = imp + p[g * tq:(g + 1) * tq]
    nb = _iota((NBLK_PAD, 1), 0)
    rel = ((i * tq + _iota((1, tq), 1)) >> 6) - nb
    forced = (nb == 0) | ((rel >= 0) & (rel < NSA_LOCAL))
    val = jnp.where(forced, FORCE, jnp.where(rel >= 0, imp.T, -1.0))
    sel = jnp.zeros((NBLK_PAD, tq), F32)
    nf = nb.astype(F32)
    for _ in range(n_sel):
        mx = jnp.max(val, axis=0, keepdims=True)
        idx = jnp.min(jnp.where(val == mx, nf, float(NBLK_PAD)), axis=0, keepdims=True)
        pick = nf == idx
        sel = jnp.where(pick & (mx >= 0.0), 1.0, sel)
        val = jnp.where(pick, -jnp.inf, val)
    sel = sel.T
    sel_ref[0, 0] = (1.0 - sel).astype(BF16)
    touch_ref[0, 0, 0] = jnp.broadcast_to(jnp.max(sel, axis=0, keepdims=True), (8, NBLK_PAD))


def _nsa_compressed(proj, kca, vca, seq, n_sel):
    b = proj.shape[0]
    tq = _tile(seq, 256)
    lay = _LAYOUT
    gw = GROUP * LANE
    return pl.pallas_call(
        functools.partial(_cmp_kernel, tq=tq, n_sel=n_sel),
        grid=(b, NSA_KV, seq // tq),
        in_specs=[pl.BlockSpec((1, tq, gw), lambda b_, k, i: (b_, i, lay.qa // gw + k)),
                  pl.BlockSpec((1, 1, NBLK_PAD, LANE), lambda b_, k, i: (b_, k, 0, 0)),
                  pl.BlockSpec((1, 1, NBLK_PAD, LANE), lambda b_, k, i: (b_, k, 0, 0))],
        out_specs=[pl.BlockSpec((1, tq, GROUP * HEAD_DIM), lambda b_, k, i: (b_, i, k)),
                   pl.BlockSpec((1, 1, tq, NBLK_PAD), lambda b_, k, i: (b_, k, i, 0)),
                   pl.BlockSpec((1, 1, 1, 8, NBLK_PAD), lambda b_, k, i: (b_, k, i, 0, 0))],
        out_shape=[jax.ShapeDtypeStruct((b, seq, NSA_HEADS * HEAD_DIM), F32),
                   jax.ShapeDtypeStruct((b, NSA_KV, seq, NBLK_PAD), BF16),
                   jax.ShapeDtypeStruct((b, NSA_KV, seq // tq, 8, NBLK_PAD), F32)],
        compiler_params=_cparams("parallel", "parallel", "parallel"),
        name="nsa_compressed",
    )(proj, kca, vca)


def _flash(chunk_fns, v_ref, i, tq, tk, m_ref, acc_ref, s_refs, n_active=None, tile_at=None):
    m_ref[...] = jnp.full(m_ref.shape, NEG, F32)
    acc_ref[...] = jnp.zeros(acc_ref.shape, F32)
    acc_reps = acc_ref.shape[1] // LANE
    n_full = (i * tq) // tk
    if n_active is None:
        n_active = n_full
        tile_at = lambda n: n

    def nth_tile(n):
        return jnp.where(n < n_active, tile_at(n), n_full)

    def issue(n, s_ref):
        start = pl.multiple_of(nth_tile(n) * tk, tk)
        for c, fn in enumerate(chunk_fns):
            s_ref[c * tq:(c + 1) * tq, :] = fn(start)

    def accumulate(j, s_ref, visible):
        start = pl.multiple_of(j * tk, tk)
        vt = v_ref[pl.ds(start, tk), :]
        for c in range(len(chunk_fns)):
            rows = slice(c * tq, (c + 1) * tq)
            s = s_ref[rows, :]
            if visible is not None:
                s = jnp.where(visible, s, NEG)
            m_old = m_ref[rows, :]
            m_new = jnp.maximum(m_old, jnp.max(s, axis=-1, keepdims=True))
            p = jnp.exp((s - jnp.concatenate([m_new] * (tk // LANE), axis=1)).astype(BF16))
            alpha = jnp.concatenate([jnp.exp(m_old - m_new)] * acc_reps, axis=1)
            acc_ref[rows, :] = alpha * acc_ref[rows, :] + _dot(p, vt)
            m_ref[rows, :] = m_new

    s0, s1 = s_refs
    issue(0, s0)

    def body(pair, carry):
        n = 2 * pair
        issue(n + 1, s1)
        accumulate(nth_tile(n), s0, None)
        issue(n + 2, s0)
        accumulate(nth_tile(n + 1), s1, None)
        return carry

    pairs = n_active // 2
    lax.fori_loop(0, pairs, body, 0)
    visible = (_iota((tq, tk), 1) - _iota((tq, tk), 0)) <= i * tq - n_full * tk
    leftover = n_active - 2 * pairs

    @pl.when(leftover == 1)
    def _():
        issue(n_active, s1)
        accumulate(nth_tile(n_active - 1), s0, None)
        accumulate(n_full, s1, visible)

    @pl.when(leftover == 0)
    def _():
        accumulate(n_full, s0, visible)


def _sel_kernel(order_ref, count_ref, q_ref, sel_ref, k_ref, v_ref, o_ref, qa_ref, m_ref, acc_ref,
                s0_ref, s1_ref, *, tq, tk, n_tiles):
    i = pl.program_id(2)
    tile_id = (pl.program_id(0) * pl.num_programs(1) + pl.program_id(1)) * pl.num_programs(2) + i
    q = q_ref[0]
    sc = sel_ref[0, 0]
    for g in range(GROUP):
        qa_ref[g * tq:(g + 1) * tq, 0:LANE] = q[:, g * LANE:(g + 1) * LANE]
        qa_ref[g * tq:(g + 1) * tq, LANE:2 * LANE] = sc
    kv = k_ref.at[0, 0]

    def head_scores(g):
        return lambda start: _dot_nt(qa_ref[g * tq:(g + 1) * tq, :], kv[pl.ds(start, tk), :])

    _flash([head_scores(g) for g in range(GROUP)], v_ref.at[0], i, tq, tk, m_ref, acc_ref, (s0_ref, s1_ref),
           n_active=count_ref[tile_id], tile_at=lambda n: order_ref[tile_id * n_tiles + n])
    acc = acc_ref[...]
    o = acc / acc[:, HEAD_DIM:HEAD_DIM + 1]
    o_ref[0] = _unstack_heads(o, GROUP, tq, HEAD_DIM)


def _nsa_selected(proj, selc, touch, ksel, seq):
    b = proj.shape[0]
    tq = _tile(seq, 256)
    tk = _tile(seq, 512)
    n_tiles = seq // tk
    lay = _LAYOUT
    gw = GROUP * LANE
    per_tile = touch[:, :, :, 0, :seq // NSA_BLOCK].reshape(b, NSA_KV, seq // tq, n_tiles, tk // NSA_BLOCK)
    n_full = (np.arange(seq // tq) * tq) // tk
    active = (jnp.max(per_tile, axis=-1) > 0.0) & jnp.asarray(np.arange(n_tiles)[None, :] < n_full[:, None])
    order = jnp.argsort(jnp.where(active, 0, 1), axis=-1, stable=True).astype(jnp.int32).reshape(-1)
    count = jnp.sum(active, axis=-1).astype(jnp.int32).reshape(-1)
    grid_spec = pltpu.PrefetchScalarGridSpec(
        num_scalar_prefetch=2,
        grid=(b, NSA_KV, seq // tq),
        in_specs=[pl.BlockSpec((1, tq, gw), lambda b_, k, i, o, c: (b_, i, lay.qa // gw + k)),
                  pl.BlockSpec((1, 1, tq, NBLK_PAD), lambda b_, k, i, o, c: (b_, k, i, 0)),
                  pl.BlockSpec((1, 1, seq, 2 * LANE), lambda b_, k, i, o, c: (b_, k, 0, 0)),
                  pl.BlockSpec((1, seq, LANE), lambda b_, k, i, o, c: (b_, 0, lay.vs // LANE + k))],
        out_specs=pl.BlockSpec((1, tq, GROUP * HEAD_DIM), lambda b_, k, i, o, c: (b_, i, k)),
        scratch_shapes=[pltpu.VMEM((GROUP * tq, 2 * LANE), BF16),
                        pltpu.VMEM((GROUP * tq, LANE), F32),
                        pltpu.VMEM((GROUP * tq, LANE), F32),
                        pltpu.VMEM((GROUP * tq, tk), F32),
                        pltpu.VMEM((GROUP * tq, tk), F32)])
    return pl.pallas_call(
        functools.partial(_sel_kernel, tq=tq, tk=tk, n_tiles=n_tiles),
        grid_spec=grid_spec,
        out_shape=jax.ShapeDtypeStruct((b, seq, NSA_HEADS * HEAD_DIM), F32),
        compiler_params=_cparams("parallel", "parallel", "arbitrary"),
        name="nsa_selected",
    )(order, count, proj, selc, ksel, proj)


def _diff_kernel(lam_ref, q_ref, k_ref, v_ref, g_ref, o_ref, m_ref, acc_ref, s0_ref, s1_ref,
                 *, tq, tk, lambda_init):
    i = pl.program_id(2)
    kv = k_ref.at[0]

    def map_scores(r):
        cols = slice(r * LANE, (r + 1) * LANE)
        return lambda start: _dot_nt(q_ref[0, :, cols], kv[pl.ds(start, tk), cols])

    _flash([map_scores(0), map_scores(1)], v_ref.at[0], i, tq, tk, m_ref, acc_ref, (s0_ref, s1_ref))
    acc = acc_ref[...]
    dv = 2 * HEAD_DIM
    a = acc[:, :dv] / acc[:, dv:dv + 1]
    lv = lam_ref[...]
    lam = (jnp.exp(jnp.sum(lv[0:1] * lv[1:2], axis=-1, keepdims=True))
           - jnp.exp(jnp.sum(lv[2:3] * lv[3:4], axis=-1, keepdims=True)) + lambda_init)
    o = a[0:tq] - lam * a[tq:2 * tq]
    o = o * lax.rsqrt(jnp.mean(o * o, axis=-1, keepdims=True) + LN_EPS) * g_ref[...]
    o_ref[0] = (o * (1.0 - lambda_init)).astype(o_ref.dtype)


def _diff_attention(proj, lam_vecs, norm_g, seq, lambda_init):
    b = proj.shape[0]
    tq = _tile(seq, 512)
    tk = _tile(seq, 512)
    lay = _LAYOUT
    w2 = 2 * LANE
    return pl.pallas_call(
        functools.partial(_diff_kernel, tq=tq, tk=tk, lambda_init=lambda_init),
        grid=(b, DIFF_HEADS, seq // tq),
        in_specs=[pl.BlockSpec((4, HEAD_DIM), lambda b_, h, i: (0, 0)),
                  pl.BlockSpec((1, tq, w2), lambda b_, h, i: (b_, i, lay.qd // w2 + h)),
                  pl.BlockSpec((1, seq, w2), lambda b_, h, i: (b_, 0, lay.kd // w2 + h)),
                  pl.BlockSpec((1, seq, w2), lambda b_, h, i: (b_, 0, lay.vd // w2 + h)),
                  pl.BlockSpec((1, 2 * HEAD_DIM), lambda b_, h, i: (0, 0))],
        out_specs=pl.BlockSpec((1, tq, 2 * HEAD_DIM), lambda b_, h, i: (b_, i, h)),
        out_shape=jax.ShapeDtypeStruct((b, seq, DIFF_HEADS * 2 * HEAD_DIM), BF16),
        scratch_shapes=[pltpu.VMEM((2 * tq, LANE), F32),
                        pltpu.VMEM((2 * tq, w2), F32),
                        pltpu.VMEM((2 * tq, tk), F32),
                        pltpu.VMEM((2 * tq, tk), F32)],
        compiler_params=_cparams("parallel", "parallel", "arbitrary"),
        name="diff_attention",
    )(lam_vecs, proj, proj, proj, norm_g)


def _band_kernel(hp_ref, q_ref, k_ref, v_ref, o_ref, *, tq, window, band, use_sinks):
    kvh = pl.program_id(1)
    i = pl.program_id(2)
    start = pl.multiple_of(jnp.maximum(i * tq - window, 0), LANE)
    kb = k_ref[0, pl.ds(start, band), :]
    vb = v_ref[0, pl.ds(start, band), :]
    t1 = i * tq + _iota((tq, 1), 0)
    dist = t1 - (start + _iota((1, band), 1))
    visible = (dist >= 0) & (dist < window)
    outs = []
    scores = [_dot_nt(q_ref[0, :, g * LANE:(g + 1) * LANE], kb) for g in range(GROUP)]
    for g in range(GROUP):
        s = jnp.where(visible, scores[g], NEG)
        mx = jnp.max(s, axis=-1, keepdims=True)
        if use_sinks:
            h = kvh * GROUP + g
            sink = hp_ref[1, h] + hp_ref[0, h] * t1.astype(F32)
            mx = jnp.maximum(mx, sink)
        acc = _dot(jnp.exp((s - mx).astype(BF16)), vb)
        den = acc[:, HEAD_DIM:HEAD_DIM + 1]
        if use_sinks:
            den = den + jnp.exp(sink - mx)
        outs.append((acc / den)[:, :HEAD_DIM])
    o_ref[0] = jnp.concatenate(outs, axis=1).astype(o_ref.dtype)


def _banded(proj, head_params, seq, window, q_off, k_off, v_off, use_sinks, out_dtype, name):
    b = proj.shape[0]
    tq = _tile(seq, 256)
    band = min(window + tq, seq)
    gw = GROUP * LANE
    return pl.pallas_call(
        functools.partial(_band_kernel, tq=tq, window=window, band=band, use_sinks=use_sinks),
        grid=(b, NSA_KV, seq // tq),
        in_specs=[pl.BlockSpec(memory_space=pltpu.SMEM),
                  pl.BlockSpec((1, tq, gw), lambda b_, k, i: (b_, i, q_off // gw + k)),
                  pl.BlockSpec((1, seq, LANE), lambda b_, k, i: (b_, 0, k_off // LANE + k)),
                  pl.BlockSpec((1, seq, LANE), lambda b_, k, i: (b_, 0, v_off // LANE + k))],
        out_specs=pl.BlockSpec((1, tq, GROUP * HEAD_DIM), lambda b_, k, i: (b_, i, k)),
        out_shape=jax.ShapeDtypeStruct((b, seq, NSA_HEADS * HEAD_DIM), out_dtype),
        compiler_params=_cparams("parallel", "parallel", "parallel"),
        name=name,
    )(head_params, proj, proj, proj)


def _layer_norm(z, g, b):
    mu = jnp.mean(z, axis=-1, keepdims=True)
    zc = z - mu
    var = jnp.mean(zc * zc, axis=-1, keepdims=True)
    return zc * lax.rsqrt(var + LN_EPS) * g + b


def _merge_kernel(x_ref, xin_ref, oc_ref, os_ref, ow_ref, ga_ref, ob_ref, od_ref, wgm_ref, ex_ref,
                  wa_ref, wb_ref, wc_ref, wo_ref, lng_ref, lnb_ref, xo_ref, xb_ref, *, alpha):
    d = x_ref.shape[1]
    ga = jax.nn.sigmoid(ga_ref[...].astype(F32)).astype(BF16)
    oa = (_dot(ga, ex_ref[0]) * oc_ref[...] + _dot(ga, ex_ref[1]) * os_ref[...]
          + _dot(ga, ex_ref[2]) * ow_ref[...])
    gm = jax.nn.sigmoid(_dot(xin_ref[...], wgm_ref[...]))
    merged = (gm[:, 0:d] * _dot(oa.astype(BF16), wa_ref[...])
              + gm[:, d:2 * d] * _dot(ob_ref[...], wb_ref[...])
              + gm[:, 2 * d:3 * d] * _dot(od_ref[...], wc_ref[...]))
    z = alpha * x_ref[...] + _dot(merged.astype(BF16), wo_ref[...])
    y = _layer_norm(z, lng_ref[...], lnb_ref[...])
    xo_ref[...] = y
    xb_ref[...] = y.astype(BF16)


def _merge(x, xb, oc, os_, ow, proj2d, ob, od, wgm, expand, wa, wb, wc, wo, lng, lnb, alpha):
    t, d = x.shape
    tm = _tile(t, 256)
    ha = NSA_HEADS * HEAD_DIM
    row = lambda w: pl.BlockSpec((tm, w), lambda i: (i, 0))
    full = lambda a: pl.BlockSpec(a.shape, lambda i: (0,) * a.ndim)
    return pl.pallas_call(
        functools.partial(_merge_kernel, alpha=alpha),
        grid=(t // tm,),
        in_specs=[row(d), row(d), row(ha), row(ha), row(ha),
                  pl.BlockSpec((tm, LANE), lambda i: (i, _LAYOUT.ga // LANE)),
                  row(ha), row(ha), full(wgm), full(expand),
                  full(wa), full(wb), full(wc), full(wo), full(lng), full(lnb)],
        out_specs=[row(d), row(d)],
        out_shape=[jax.ShapeDtypeStruct((t, d), F32), jax.ShapeDtypeStruct((t, d), BF16)],
        compiler_params=_cparams("parallel"),
        name="merge_norm",
    )(x, xb, oc, os_, ow, proj2d, ob, od, wgm, expand, wa, wb, wc, wo, lng, lnb)


def _first_max(v, idx, n):
    mx = jnp.max(v, axis=0, keepdims=True)
    first = jnp.min(jnp.where(v == mx, idx, n), axis=0, keepdims=True)
    return mx, idx == first


def _router_kernel(x_ref, rw_ref, rb_ref, cw_ref):
    gsz = N_EXPERTS // N_GROUPS
    logits = lax.dot_general(rw_ref[...], x_ref[...], (((1,), (1,)), ((), ())),
                             precision=lax.Precision.HIGHEST,
                             preferred_element_type=F32)
    tm = logits.shape[1]
    scores = jax.nn.sigmoid(logits)
    biased = scores + rb_ref[...]
    iw = _iota((gsz, tm), 0)
    gs = []
    for g in range(N_GROUPS):
        blk = biased[g * gsz:(g + 1) * gsz]
        m1, pick = _first_max(blk, iw, gsz)
        m2 = jnp.max(jnp.where(pick, -jnp.inf, blk), axis=0, keepdims=True)
        gs.append(m1 + m2)
    gs = jnp.concatenate(gs, axis=0)
    ig = _iota((N_GROUPS, tm), 0)
    gmask = jnp.zeros((N_GROUPS, tm), F32)
    for _ in range(TOPK_GROUPS):
        _, pick = _first_max(gs, ig, N_GROUPS)
        gmask = jnp.where(pick, 1.0, gmask)
        gs = jnp.where(pick, -jnp.inf, gs)
    emask = jnp.concatenate(
        [jnp.broadcast_to(gmask[g:g + 1], (gsz, tm)) for g in range(N_GROUPS)], axis=0)
    cand = jnp.where(emask > 0.5, biased, NEG)
    ie = _iota((N_EXPERTS, tm), 0)
    chosen = jnp.zeros((N_EXPERTS, tm), F32)
    for _ in range(TOP_K):
        _, pick = _first_max(cand, ie, N_EXPERTS)
        chosen = jnp.where(pick, 1.0, chosen)
        cand = jnp.where(pick, -jnp.inf, cand)
    w = scores * chosen
    cw_ref[...] = w / jnp.sum(w, axis=0, keepdims=True) * ROUTED_SCALE


def _router(x, rw_t, rb):
    t, d = x.shape
    tm = _tile(t, 512)
    return pl.pallas_call(
        _router_kernel,
        grid=(t // tm,),
        in_specs=[pl.BlockSpec((tm, d), lambda i: (i, 0)),
                  pl.BlockSpec((N_EXPERTS, d), lambda i: (0, 0)),
                  pl.BlockSpec((N_EXPERTS, 1), lambda i: (0, 0))],
        out_specs=pl.BlockSpec((N_EXPERTS, tm), lambda i: (0, i)),
        out_shape=jax.ShapeDtypeStruct((N_EXPERTS, t), F32),
        compiler_params=_cparams("parallel"),
        name="router",
    )(x, rw_t, rb)


def _moe_kernel(x_ref, cw_ref, wg_ref, wu_ref, wd_ref, y_ref, *, ne):
    c = pl.program_id(1)
    x = x_ref[...]
    edim = wg_ref.shape[2]
    cw = cw_ref[...]
    cw_hi = cw.astype(BF16)
    cw_lo = (cw - cw_hi.astype(F32)).astype(BF16)
    expand = jnp.concatenate(
        [jnp.where(_iota((N_EXPERTS, edim), 0) == c * ne + e, 1.0, 0.0) for e in range(ne)],
        axis=1).astype(BF16)
    scale = _dot(cw_hi, expand) + _dot(cw_lo, expand)
    hs = []
    for e in range(ne):
        hs.append(jax.nn.silu(_dot(x, wg_ref[e])) * _dot(x, wu_ref[e]))
    h = (jnp.concatenate(hs, axis=1) * scale).astype(BF16)
    y = _dot(h, wd_ref[...].reshape(ne * edim, wd_ref.shape[2]))

    @pl.when(c == 0)
    def _():
        y_ref[...] = y

    @pl.when(c != 0)
    def _():
        y_ref[...] += y


def _moe_dense(xb, cw, wg, wu, wd, layer):
    t, d = xb.shape
    edim = wg.shape[3]
    tm = _tile(t, 512)
    ne = 8
    return pl.pallas_call(
        functools.partial(_moe_kernel, ne=ne),
        grid=(t // tm, N_EXPERTS // ne),
        in_specs=[pl.BlockSpec((tm, d), lambda i, c: (i, 0)),
                  pl.BlockSpec((tm, N_EXPERTS), lambda i, c: (i, 0)),
                  pl.BlockSpec((None, ne, d, edim), lambda i, c: (layer, c, 0, 0)),
                  pl.BlockSpec((None, ne, d, edim), lambda i, c: (layer, c, 0, 0)),
                  pl.BlockSpec((None, ne, edim, d), lambda i, c: (layer, c, 0, 0))],
        out_specs=pl.BlockSpec((tm, d), lambda i, c: (i, 0)),
        out_shape=jax.ShapeDtypeStruct((t, d), F32),
        compiler_params=_cparams("parallel", "arbitrary"),
        name="moe_experts",
    )(xb, cw, wg, wu, wd)


def _tail_kernel(x_ref, xb_ref, y_ref, p_ref, sg_ref, su_ref, sd_ref, pg_ref, pp_ref,
                 lng_ref, lnb_ref, xo_ref, xbo_ref, *, alpha):
    xb = xb_ref[...]
    h = jax.nn.silu(_dot(xb, sg_ref[...])) * _dot(xb, su_ref[...])
    shared = _dot(h.astype(BF16), sd_ref[...])
    ple = jax.nn.sigmoid(_dot(xb, pg_ref[...])) * _dot(p_ref[...].astype(BF16), pp_ref[...])
    z = alpha * x_ref[...] + y_ref[...] + shared + ple
    y = _layer_norm(z, lng_ref[...], lnb_ref[...])
    xo_ref[...] = y
    xbo_ref[...] = y.astype(BF16)


def _tail(x, xb, y, p, sg, su, sd, pg, pp, lng, lnb, alpha):
    t, d = x.shape
    tm = _tile(t, 256)
    row = lambda w: pl.BlockSpec((tm, w), lambda i: (i, 0))
    full = lambda a: pl.BlockSpec(a.shape, lambda i: (0,) * a.ndim)
    return pl.pallas_call(
        functools.partial(_tail_kernel, alpha=alpha),
        grid=(t // tm,),
        in_specs=[row(d), row(d), row(d), row(p.shape[1]), full(sg), full(su), full(sd),
                  full(pg), full(pp), full(lng), full(lnb)],
        out_specs=[row(d), row(d)],
        out_shape=[jax.ShapeDtypeStruct((t, d), F32), jax.ShapeDtypeStruct((t, d), BF16)],
        compiler_params=_cparams("parallel"),
        name="ffn_tail_norm",
    )(x, xb, y, p, sg, su, sd, pg, pp, lng, lnb)


def _slot_weights(w):
    d = w.shape[0]
    o = IN_OFFS
    qs = HEAD_DIM ** -0.5

    def heads(off, n, width=HEAD_DIM, slot=LANE, scale=1.0):
        blk = w[:, off:off + n * width].reshape(d, n, width) * scale
        return jnp.pad(blk, ((0, 0), (0, 0), (0, slot - width))).reshape(d, n * slot)

    gates = w[:, o[7]:o[8]].reshape(d, NSA_HEADS, 3).transpose(0, 2, 1).reshape(d, 3 * NSA_HEADS)
    parts = [heads(o[0], NSA_HEADS, scale=qs), heads(o[8], SWA_HEADS, scale=qs),
             heads(o[11], 2 * DIFF_HEADS, scale=qs), heads(o[12], 2 * DIFF_HEADS),
             heads(o[13], DIFF_HEADS, 2 * HEAD_DIM, 2 * LANE),
             heads(o[3], NSA_KV), heads(o[4], NSA_KV), heads(o[5], NSA_KV), heads(o[6], NSA_KV),
             heads(o[9], SWA_KV), heads(o[10], SWA_KV),
             w[:, o[1]:o[2]], w[:, o[2]:o[3]],
             jnp.pad(gates, ((0, 0), (0, LANE - 3 * NSA_HEADS)))]
    out = jnp.concatenate(parts, axis=1)
    assert out.shape[1] == _LAYOUT.width
    return out.astype(BF16)


def _compressed_slots(c, pos_cols):
    b, nb, hkv, dh = c.shape
    c = c.transpose(0, 2, 1, 3)
    extra = jnp.zeros((b, hkv, nb, LANE - dh), F32)
    if pos_cols:
        cpos = np.arange(nb) * NSA_BLOCK + NSA_BLOCK - 1
        cols = np.zeros((nb, LANE - dh), np.float32)
        cols[:, 0] = cpos // LANE
        cols[:, 1] = cpos % LANE
        extra = extra + jnp.asarray(cols)
    c = jnp.concatenate([c, extra], axis=-1)
    c = jnp.pad(c, ((0, 0), (0, 0), (0, NBLK_PAD - nb), (0, 0)))
    return c.astype(BF16)


def kernel(x, p, w_in, nsa_pe_k, nsa_w1_k, nsa_w2_k, nsa_pe_v, nsa_w1_v, nsa_w2_v, swa_sinks,
           diff_lq1, diff_lk1, diff_lq2, diff_lk2, diff_norm_g, w_branch_a, w_branch_b, w_branch_c,
           w_out, ln1_g, ln1_b, router_w, router_bias, exp_w_gate, exp_w_up, exp_w_down,
           sh_w_gate, sh_w_up, sh_w_down, ple_w_proj, ple_w_gate, ln2_g, ln2_b):
    bsz, seq, d = x.shape
    depth = w_in.shape[0]
    t = bsz * seq
    alpha = (2 * depth) ** 0.25
    lay = _LAYOUT
    nb = seq // NSA_BLOCK
    assert nb <= NBLK_PAD and seq % LANE == 0
    n_sel = min(NSA_TOPN, nb)

    aux = jnp.asarray(lay.aux)
    blk_mask = jnp.asarray(
        (np.arange(seq)[:, None] // NSA_BLOCK == np.arange(NBLK_PAD)[None, :]) * MASK_NEG, BF16)
    ex = np.zeros((3, LANE, NSA_HEADS * HEAD_DIM), np.float32)
    for c in range(3):
        for h in range(NSA_HEADS):
            ex[c, c * NSA_HEADS + h, h * HEAD_DIM:(h + 1) * HEAD_DIM] = 1.0
    expand = jnp.asarray(ex, BF16)
    slopes8 = jnp.asarray(_slopes(NSA_HEADS), F32)

    wg_all = exp_w_gate.astype(BF16)
    wu_all = exp_w_up.astype(BF16)
    wd_all = exp_w_down.astype(BF16)
    xf = x.reshape(t, d)
    xb = xf.astype(BF16)
    for i in range(depth):
        w_slots = _slot_weights(w_in[i])
        proj2d = _project(xb, w_slots, aux, seq, BF16, 1408)
        proj = proj2d.reshape(bsz, seq, lay.width)

        def blocks(off):
            c = proj[:, :, off:off + LANE].reshape(bsz, nb, NSA_BLOCK, NSA_KV, HEAD_DIM)
            return c.transpose(0, 1, 3, 2, 4).reshape(bsz * nb * NSA_KV, NSA_BLOCK * HEAD_DIM)
        cmp = _compress(
            jnp.stack([blocks(lay.kc), blocks(lay.vc)]),
            jnp.stack([nsa_pe_k[i].reshape(1, -1), nsa_pe_v[i].reshape(1, -1)]),
            jnp.stack([nsa_w1_k[i], nsa_w1_v[i]]).astype(BF16),
            jnp.stack([nsa_w2_k[i], nsa_w2_v[i]]).astype(BF16))
        cmp = cmp.reshape(2, bsz, nb, NSA_KV, HEAD_DIM)
        kca = _compressed_slots(cmp[0], True)
        vca = _compressed_slots(cmp[1], False)

        o_cmp, selc, touch = _nsa_compressed(proj, kca, vca, seq, n_sel)
        ks = proj[:, :, lay.ks:lay.ks + NSA_KV * LANE].reshape(bsz, seq, NSA_KV, LANE).transpose(0, 2, 1, 3)
        ksel = jnp.concatenate(
            [ks, jnp.broadcast_to(blk_mask[None, None], (bsz, NSA_KV, seq, NBLK_PAD))], axis=-1)
        o_sel = _nsa_selected(proj, selc, touch, ksel, seq)
        hp_a = jnp.stack([slopes8, jnp.zeros_like(slopes8)])
        o_win = _banded(proj, hp_a, seq, NSA_WINDOW, lay.qa, lay.kw, lay.vw, False, F32, "nsa_window")
        hp_b = jnp.stack([slopes8, swa_sinks[i].astype(F32)])
        o_b = _banded(proj, hp_b, seq, SWA_WINDOW, lay.qb, lay.kb, lay.vb, True, BF16, "swa_sinks")

        lambda_init = 0.8 - 0.6 * math.exp(-0.3 * i)
        lam_vecs = jnp.stack([diff_lq1[i], diff_lk1[i], diff_lq2[i], diff_lk2[i]]).astype(F32)
        o_d = _diff_attention(proj, lam_vecs, diff_norm_g[i].reshape(1, -1), seq, lambda_init)

        ha = NSA_HEADS * HEAD_DIM
        xf, xb = _merge(
            xf, xb, o_cmp.reshape(t, ha), o_sel.reshape(t, ha), o_win.reshape(t, ha), proj2d,
            o_b.reshape(t, ha), o_d.reshape(t, ha), w_in[i][:, GM_OFF:].astype(BF16), expand,
            w_branch_a[i].astype(BF16), w_branch_b[i].astype(BF16), w_branch_c[i].astype(BF16),
            w_out[i].astype(BF16), ln1_g[i].reshape(1, d), ln1_b[i].reshape(1, d), alpha)

        cw_t = _router(xf, router_w[i].T, router_bias[i].reshape(-1, 1))
        y = _moe_dense(xb, cw_t.T, wg_all, wu_all, wd_all, i)
        xf, xb = _tail(xf, xb, y, p[i].reshape(t, -1),
                       sh_w_gate[i].astype(BF16), sh_w_up[i].astype(BF16), sh_w_down[i].astype(BF16),
                       ple_w_gate[i].astype(BF16), ple_w_proj[i].astype(BF16),
                       ln2_g[i].reshape(1, d), ln2_b[i].reshape(1, d), alpha)
    return xf.reshape(bsz, seq, d)
```

```python
import functools
import math

import numpy as np
import jax
import jax.numpy as jnp
from jax import lax
from jax.experimental import pallas as pl
from jax.experimental.pallas import tpu as pltpu

F32 = jnp.float32
BF16 = jnp.bfloat16

HEAD_DIM = 64
NSA_HEADS = 8
NSA_KV = 2
NSA_BLOCK = 64
NSA_TOPN = 16
NSA_LOCAL = 2
NSA_WINDOW = 512
SWA_HEADS = 8
SWA_KV = 2
SWA_WINDOW = 128
DIFF_HEADS = 4
N_EXPERTS = 64
TOP_K = 8
N_GROUPS = 8
TOPK_GROUPS = 4
ROUTED_SCALE = 2.5
LN_EPS = 1e-5
NEG = -1e30
FORCE = 1e4
MASK_NEG = -(2.0 ** 100)

LANE = 128
GROUP = NSA_HEADS // NSA_KV
NBLK_PAD = 128
VMEM_LIMIT = 56 * 1024 * 1024

IN_SIZES = (NSA_HEADS * HEAD_DIM,) + (NSA_KV * HEAD_DIM,) * 6 + (NSA_HEADS * 3,) + \
    (SWA_HEADS * HEAD_DIM, SWA_KV * HEAD_DIM, SWA_KV * HEAD_DIM) + \
    (DIFF_HEADS * 2 * HEAD_DIM,) * 3
IN_OFFS = np.concatenate([[0], np.cumsum(IN_SIZES)]).tolist()
GM_OFF = IN_OFFS[-1]


def _slopes(n):
    return [2.0 ** (-8.0 * (h + 1) / n) for h in range(n)]


class _Layout:
    def __init__(self):
        src, scale, bias, pa, pb = [], [], [], [], []

        def slot(cols, sc=1.0, consts=(), pos=False, width=LANE):
            s = [-1] * width
            c = [0.0] * width
            a = [0.0] * width
            b = [0.0] * width
            s[:len(cols)] = cols
            for off, val in consts:
                c[off] = val
            if pos:
                a[HEAD_DIM] = 1.0
                b[HEAD_DIM + 1] = 1.0
            start = len(src)
            src.extend(s)
            scale.extend([sc] * width)
            bias.extend(c)
            pa.extend(a)
            pb.extend(b)
            return start

        def rng(base, n):
            return list(range(base, base + n))

        qs = HEAD_DIM ** -0.5
        o = IN_OFFS
        sl8 = _slopes(NSA_HEADS)
        sl4 = _slopes(DIFF_HEADS)
        self.qa = len(src)
        for h in range(NSA_HEADS):
            slot(rng(o[0] + h * HEAD_DIM, HEAD_DIM), qs,
                 [(HEAD_DIM, sl8[h] * LANE), (HEAD_DIM + 1, sl8[h])])
        self.qb = len(src)
        for h in range(SWA_HEADS):
            slot(rng(o[8] + h * HEAD_DIM, HEAD_DIM), qs,
                 [(HEAD_DIM, sl8[h] * LANE), (HEAD_DIM + 1, sl8[h])])
        self.qd = len(src)
        for h in range(DIFF_HEADS):
            for r in range(2):
                slot(rng(o[11] + (h * 2 + r) * HEAD_DIM, HEAD_DIM), qs,
                     [(HEAD_DIM, sl4[h] * LANE), (HEAD_DIM + 1, sl4[h])])
        self.kd = len(src)
        for h in range(DIFF_HEADS):
            for r in range(2):
                slot(rng(o[12] + (h * 2 + r) * HEAD_DIM, HEAD_DIM), pos=True)
        self.vd = len(src)
        for h in range(DIFF_HEADS):
            slot(rng(o[13] + h * 2 * HEAD_DIM, 2 * HEAD_DIM),
                 consts=[(2 * HEAD_DIM, 1.0)], width=2 * LANE)

        def kv_slots(k_src, v_src):
            k_off = len(src)
            for k in range(NSA_KV):
                slot(rng(k_src + k * HEAD_DIM, HEAD_DIM), pos=True)
            v_off = len(src)
            for k in range(NSA_KV):
                slot(rng(v_src + k * HEAD_DIM, HEAD_DIM), consts=[(HEAD_DIM, 1.0)])
            return k_off, v_off

        self.ks, self.vs = kv_slots(o[3], o[4])
        self.kw, self.vw = kv_slots(o[5], o[6])
        self.kb, self.vb = kv_slots(o[9], o[10])
        self.kc = slot(rng(o[1], NSA_KV * HEAD_DIM))
        self.vc = slot(rng(o[2], NSA_KV * HEAD_DIM))
        self.ga = slot([o[7] + h * 3 + c for c in range(3) for h in range(NSA_HEADS)])
        self.width = len(src)
        self.src = np.asarray(src, np.int32)
        self.scale = np.asarray(scale, np.float32)
        aux = np.zeros((8, self.width), np.float32)
        aux[0] = bias
        aux[1] = pa
        aux[2] = pb
        self.aux = aux


_LAYOUT = _Layout()


def _cparams(*sem):
    return pltpu.CompilerParams(dimension_semantics=sem, vmem_limit_bytes=VMEM_LIMIT)


def _tile(n, pref):
    t = min(n, pref)
    assert n % t == 0, (n, pref)
    return t


def _iota(shape, dim):
    return lax.broadcasted_iota(jnp.int32, shape, dim)


def _dot(a, b):
    return jnp.dot(a, b, preferred_element_type=F32)


def _dot_nt(a, b):
    return lax.dot_general(a, b, (((1,), (1,)), ((), ())), preferred_element_type=F32)


def _proj_kernel(x_ref, w_ref, aux_ref, o_ref, *, seq, tm):
    acc = _dot(x_ref[...], w_ref[...])
    pos = (pl.program_id(0) * tm) % seq + _iota((tm, 1), 0)
    a = (pos >> 7).astype(F32)
    b = (pos & (LANE - 1)).astype(F32)
    aux = aux_ref[...]
    o_ref[...] = (acc + aux[0:1] + a * aux[1:2] + b * aux[2:3]).astype(o_ref.dtype)


def _project(xb, w, aux, seq, out_dtype, tn_pref):
    t, d = xb.shape
    n = w.shape[1]
    tm = _tile(t, 1024)
    tn = _tile(n, tn_pref)
    return pl.pallas_call(
        functools.partial(_proj_kernel, seq=seq, tm=tm),
        grid=(t // tm, n // tn),
        in_specs=[pl.BlockSpec((tm, d), lambda i, j: (i, 0)),
                  pl.BlockSpec((d, tn), lambda i, j: (0, j)),
                  pl.BlockSpec((8, tn), lambda i, j: (0, j))],
        out_specs=pl.BlockSpec((tm, tn), lambda i, j: (i, j)),
        out_shape=jax.ShapeDtypeStruct((t, n), out_dtype),
        compiler_params=_cparams("parallel", "parallel"),
        name="proj",
    )(xb, w, aux)


def _compress_kernel(x_ref, pe_ref, w1_ref, w2_ref, o_ref):
    xb = (x_ref[0].astype(F32) + pe_ref[0]).astype(BF16)
    h = jax.nn.gelu(_dot(xb, w1_ref[0]))
    o_ref[0] = _dot(h.astype(BF16), w2_ref[0])


def _compress(xs, pes, w1s, w2s):
    _, r, kdim = xs.shape
    hid = w1s.shape[2]
    dh = w2s.shape[2]
    return pl.pallas_call(
        _compress_kernel,
        grid=(2,),
        in_specs=[pl.BlockSpec((1, r, kdim), lambda i: (i, 0, 0)),
                  pl.BlockSpec((1, 1, kdim), lambda i: (i, 0, 0)),
                  pl.BlockSpec((1, kdim, hid), lambda i: (i, 0, 0)),
                  pl.BlockSpec((1, hid, dh), lambda i: (i, 0, 0))],
        out_specs=pl.BlockSpec((1, r, dh), lambda i: (i, 0, 0)),
        out_shape=jax.ShapeDtypeStruct((2, r, dh), F32),
        compiler_params=_cparams("parallel"),
        name="nsa_compress",
    )(xs, pes, w1s, w2s)


def _stack_heads(q, n):
    return jnp.concatenate([q[:, g * LANE:(g + 1) * LANE] for g in range(n)], axis=0)


def _unstack_heads(o, n, tq, width):
    return jnp.concatenate([o[g * tq:(g + 1) * tq, :width] for g in range(n)], axis=1)


def _cmp_kernel(q_ref, kc_ref, vc_ref, oc_ref, sel_ref, touch_ref, *, tq, n_sel):
    i = pl.program_id(2)
    qs = _stack_heads(q_ref[0], GROUP)
    s = _dot_nt(qs, kc_ref[0, 0])
    t1 = i * tq + _iota((tq, 1), 0)
    t = jnp.concatenate([t1] * GROUP, axis=0)
    n = _iota((1, NBLK_PAD), 1)
    valid = (n * NSA_BLOCK + (NSA_BLOCK - 1)) <= t
    s = jnp.where(valid, s, NEG)
    mx = jnp.max(s, axis=-1, keepdims=True)
    e = jnp.where(valid, jnp.exp(s - mx), 0.0)
    p = e / jnp.maximum(jnp.sum(e, axis=-1, keepdims=True), 1e-30)
    o = _dot(p.astype(BF16), vc_ref[0, 0])
    oc_ref[0] = _unstack_heads(o, GROUP, tq, HEAD_DIM)

    imp = p[0:tq]
    for g in range(1, GROUP):
        imp = imp + p[g * tq:(g + 1) * tq]
    nb = _iota((NBLK_PAD, 1), 0)
    rel = ((i * tq + _iota((1, tq), 1)) >> 6) - nb
    forced = (nb == 0) | ((rel >= 0) & (rel < NSA_LOCAL))
    val = jnp.where(forced, FORCE, jnp.where(rel >= 0, imp.T, -1.0))
    sel = jnp.zeros((NBLK_PAD, tq), F32)
    nf = nb.astype(F32)
    for _ in range(n_sel):
        mx = jnp.max(val, axis=0, keepdims=True)
        idx = jnp.min(jnp.where(val == mx, nf, float(NBLK_PAD)), axis=0, keepdims=True)
        pick = nf == idx
        sel = jnp.where(pick & (mx >= 0.0), 1.0, sel)
        val = jnp.where(pick, -jnp.inf, val)
    sel = sel.T
    sel_ref[0, 0] = (1.0 - sel).astype(BF16)
    touch_ref[0, 0, 0] = jnp.broadcast_to(jnp.max(sel, axis=0, keepdims=True), (8, NBLK_PAD))


def _nsa_compressed(proj, kca, vca, seq, n_sel):
    b = proj.shape[0]
    tq = _tile(seq, 256)
    lay = _LAYOUT
    gw = GROUP * LANE
    return pl.pallas_call(
        functools.partial(_cmp_kernel, tq=tq, n_sel=n_sel),
        grid=(b, NSA_KV, seq // tq),
        in_specs=[pl.BlockSpec((1, tq, gw), lambda b_, k, i: (b_, i, lay.qa // gw + k)),
                  pl.BlockSpec((1, 1, NBLK_PAD, LANE), lambda b_, k, i: (b_, k, 0, 0)),
                  pl.BlockSpec((1, 1, NBLK_PAD, LANE), lambda b_, k, i: (b_, k, 0, 0))],
        out_specs=[pl.BlockSpec((1, tq, GROUP * HEAD_DIM), lambda b_, k, i: (b_, i, k)),
                   pl.BlockSpec((1, 1, tq, NBLK_PAD), lambda b_, k, i: (b_, k, i, 0)),
                   pl.BlockSpec((1, 1, 1, 8, NBLK_PAD), lambda b_, k, i: (b_, k, i, 0, 0))],
        out_shape=[jax.ShapeDtypeStruct((b, seq, NSA_HEADS * HEAD_DIM), F32),
                   jax.ShapeDtypeStruct((b, NSA_KV, seq, NBLK_PAD), BF16),
                   jax.ShapeDtypeStruct((b, NSA_KV, seq // tq, 8, NBLK_PAD), F32)],
        compiler_params=_cparams("parallel", "parallel", "parallel"),
        name="nsa_compressed",
    )(proj, kca, vca)


def _flash(chunk_fns, v_ref, i, tq, tk, m_ref, acc_ref, s_refs, n_active=None, tile_at=None):
    m_ref[...] = jnp.full(m_ref.shape, NEG, F32)
    acc_ref[...] = jnp.zeros(acc_ref.shape, F32)
    acc_reps = acc_ref.shape[1] // LANE
    n_full = (i * tq) // tk
    if n_active is None:
        n_active = n_full
        tile_at = lambda n: n

    def nth_tile(n):
        return jnp.where(n < n_active, tile_at(n), n_full)

    def issue(n, s_ref):
        start = pl.multiple_of(nth_tile(n) * tk, tk)
        for c, fn in enumerate(chunk_fns):
            s_ref[c * tq:(c + 1) * tq, :] = fn(start)

    def accumulate(j, s_ref, visible):
        start = pl.multiple_of(j * tk, tk)
        vt = v_ref[pl.ds(start, tk), :]
        for c in range(len(chunk_fns)):
            rows = slice(c * tq, (c + 1) * tq)
            s = s_ref[rows, :]
            if visible is not None:
                s = jnp.where(visible, s, NEG)
            m_old = m_ref[rows, :]
            m_new = jnp.maximum(m_old, jnp.max(s, axis=-1, keepdims=True))
            p = jnp.exp((s - jnp.concatenate([m_new] * (tk // LANE), axis=1)).astype(BF16))
            alpha = jnp.concatenate([jnp.exp(m_old - m_new)] * acc_reps, axis=1)
            acc_ref[rows, :] = alpha * acc_ref[rows, :] + _dot(p, vt)
            m_ref[rows, :] = m_new

    s0, s1 = s_refs
    issue(0, s0)

    def body(pair, carry):
        n = 2 * pair
        issue(n + 1, s1)
        accumulate(nth_tile(n), s0, None)
        issue(n + 2, s0)
        accumulate(nth_tile(n + 1), s1, None)
        return carry

    pairs = n_active // 2
    lax.fori_loop(0, pairs, body, 0)
    visible = (_iota((tq, tk), 1) - _iota((tq, tk), 0)) <= i * tq - n_full * tk
    leftover = n_active - 2 * pairs

    @pl.when(leftover == 1)
    def _():
        issue(n_active, s1)
        accumulate(nth_tile(n_active - 1), s0, None)
        accumulate(n_full, s1, visible)

    @pl.when(leftover == 0)
    def _():
        accumulate(n_full, s0, visible)


def _sel_kernel(order_ref, count_ref, q_ref, sel_ref, k_ref, v_ref, o_ref, qa_ref, m_ref, acc_ref,
                s0_ref, s1_ref, *, tq, tk, n_tiles):
    i = pl.program_id(2)
    tile_id = (pl.program_id(0) * pl.num_programs(1) + pl.program_id(1)) * pl.num_programs(2) + i
    q = q_ref[0]
    sc = sel_ref[0, 0]
    for g in range(GROUP):
        qa_ref[g * tq:(g + 1) * tq, 0:LANE] = q[:, g * LANE:(g + 1) * LANE]
        qa_ref[g * tq:(g + 1) * tq, LANE:2 * LANE] = sc
    kv = k_ref.at[0, 0]

    def head_scores(g):
        return lambda start: _dot_nt(qa_ref[g * tq:(g + 1) * tq, :], kv[pl.ds(start, tk), :])

    _flash([head_scores(g) for g in range(GROUP)], v_ref.at[0], i, tq, tk, m_ref, acc_ref, (s0_ref, s1_ref),
           n_active=count_ref[tile_id], tile_at=lambda n: order_ref[tile_id * n_tiles + n])
    acc = acc_ref[...]
    o = acc / acc[:, HEAD_DIM:HEAD_DIM + 1]
    o_ref[0] = _unstack_heads(o, GROUP, tq, HEAD_DIM)


def _nsa_selected(proj, selc, touch, ksel, seq):
    b = proj.shape[0]
    tq = _tile(seq, 256)
    tk = _tile(seq, 512)
    n_tiles = seq // tk
    lay = _LAYOUT
    gw = GROUP * LANE
    per_tile = touch[:, :, :, 0, :seq // NSA_BLOCK].reshape(b, NSA_KV, seq // tq, n_tiles, tk // NSA_BLOCK)
    n_full = (np.arange(seq // tq) * tq) // tk
    active = (jnp.max(per_tile, axis=-1) > 0.0) & jnp.asarray(np.arange(n_tiles)[None, :] < n_full[:, None])
    order = jnp.argsort(jnp.where(active, 0, 1), axis=-1, stable=True).astype(jnp.int32).reshape(-1)
    count = jnp.sum(active, axis=-1).astype(jnp.int32).reshape(-1)
    grid_spec = pltpu.PrefetchScalarGridSpec(
        num_scalar_prefetch=2,
        grid=(b, NSA_KV, seq // tq),
        in_specs=[pl.BlockSpec((1, tq, gw), lambda b_, k, i, o, c: (b_, i, lay.qa // gw + k)),
                  pl.BlockSpec((1, 1, tq, NBLK_PAD), lambda b_, k, i, o, c: (b_, k, i, 0)),
                  pl.BlockSpec((1, 1, seq, 2 * LANE), lambda b_, k, i, o, c: (b_, k, 0, 0)),
                  pl.BlockSpec((1, seq, LANE), lambda b_, k, i, o, c: (b_, 0, lay.vs // LANE + k))],
        out_specs=pl.BlockSpec((1, tq, GROUP * HEAD_DIM), lambda b_, k, i, o, c: (b_, i, k)),
        scratch_shapes=[pltpu.VMEM((GROUP * tq, 2 * LANE), BF16),
                        pltpu.VMEM((GROUP * tq, LANE), F32),
                        pltpu.VMEM((GROUP * tq, LANE), F32),
                        pltpu.VMEM((GROUP * tq, tk), F32),
                        pltpu.VMEM((GROUP * tq, tk), F32)])
    return pl.pallas_call(
        functools.partial(_sel_kernel, tq=tq, tk=tk, n_tiles=n_tiles),
        grid_spec=grid_spec,
        out_shape=jax.ShapeDtypeStruct((b, seq, NSA_HEADS * HEAD_DIM), F32),
        compiler_params=_cparams("parallel", "parallel", "arbitrary"),
        name="nsa_selected",
    )(order, count, proj, selc, ksel, proj)


def _diff_kernel(lam_ref, q_ref, k_ref, v_ref, g_ref, o_ref, m_ref, acc_ref, s0_ref, s1_ref,
                 *, tq, tk, lambda_init):
    i = pl.program_id(2)
    kv = k_ref.at[0]

    def map_scores(r):
        cols = slice(r * LANE, (r + 1) * LANE)
        return lambda start: _dot_nt(q_ref[0, :, cols], kv[pl.ds(start, tk), cols])

    _flash([map_scores(0), map_scores(1)], v_ref.at[0], i, tq, tk, m_ref, acc_ref, (s0_ref, s1_ref))
    acc = acc_ref[...]
    dv = 2 * HEAD_DIM
    a = acc[:, :dv] / acc[:, dv:dv + 1]
    lv = lam_ref[...]
    lam = (jnp.exp(jnp.sum(lv[0:1] * lv[1:2], axis=-1, keepdims=True))
           - jnp.exp(jnp.sum(lv[2:3] * lv[3:4], axis=-1, keepdims=True)) + lambda_init)
    o = a[0:tq] - lam * a[tq:2 * tq]
    o = o * lax.rsqrt(jnp.mean(o * o, axis=-1, keepdims=True) + LN_EPS) * g_ref[...]
    o_ref[0] = (o * (1.0 - lambda_init)).astype(o_ref.dtype)


def _diff_attention(proj, lam_vecs, norm_g, seq, lambda_init):
    b = proj.shape[0]
    tq = _tile(seq, 512)
    tk = _tile(seq, 512)
    lay = _LAYOUT
    w2 = 2 * LANE
    return pl.pallas_call(
        functools.partial(_diff_kernel, tq=tq, tk=tk, lambda_init=lambda_init),
        grid=(b, DIFF_HEADS, seq // tq),
        in_specs=[pl.BlockSpec((4, HEAD_DIM), lambda b_, h, i: (0, 0)),
                  pl.BlockSpec((1, tq, w2), lambda b_, h, i: (b_, i, lay.qd // w2 + h)),
                  pl.BlockSpec((1, seq, w2), lambda b_, h, i: (b_, 0, lay.kd // w2 + h)),
                  pl.BlockSpec((1, seq, w2), lambda b_, h, i: (b_, 0, lay.vd // w2 + h)),
                  pl.BlockSpec((1, 2 * HEAD_DIM), lambda b_, h, i: (0, 0))],
        out_specs=pl.BlockSpec((1, tq, 2 * HEAD_DIM), lambda b_, h, i: (b_, i, h)),
        out_shape=jax.ShapeDtypeStruct((b, seq, DIFF_HEADS * 2 * HEAD_DIM), BF16),
        scratch_shapes=[pltpu.VMEM((2 * tq, LANE), F32),
                        pltpu.VMEM((2 * tq, w2), F32),
                        pltpu.VMEM((2 * tq, tk), F32),
                        pltpu.VMEM((2 * tq, tk), F32)],
        compiler_params=_cparams("parallel", "parallel", "arbitrary"),
        name="diff_attention",
    )(lam_vecs, proj, proj, proj, norm_g)


def _band_kernel(hp_ref, q_ref, k_ref, v_ref, o_ref, *, tq, window, band, use_sinks):
    kvh = pl.program_id(1)
    i = pl.program_id(2)
    start = pl.multiple_of(jnp.maximum(i * tq - window, 0), LANE)
    kb = k_ref[0, pl.ds(start, band), :]
    vb = v_ref[0, pl.ds(start, band), :]
    t1 = i * tq + _iota((tq, 1), 0)
    dist = t1 - (start + _iota((1, band), 1))
    visible = (dist >= 0) & (dist < window)
    outs = []
    scores = [_dot_nt(q_ref[0, :, g * LANE:(g + 1) * LANE], kb) for g in range(GROUP)]
    for g in range(GROUP):
        s = jnp.where(visible, scores[g], NEG)
        mx = jnp.max(s, axis=-1, keepdims=True)
        if use_sinks:
            h = kvh * GROUP + g
            sink = hp_ref[1, h] + hp_ref[0, h] * t1.astype(F32)
            mx = jnp.maximum(mx, sink)
        acc = _dot(jnp.exp((s - mx).astype(BF16)), vb)
        den = acc[:, HEAD_DIM:HEAD_DIM + 1]
        if use_sinks:
            den = den + jnp.exp(sink - mx)
        outs.append((acc / den)[:, :HEAD_DIM])
    o_ref[0] = jnp.concatenate(outs, axis=1).astype(o_ref.dtype)


def _banded(proj, head_params, seq, window, q_off, k_off, v_off, use_sinks, out_dtype, name):
    b = proj.shape[0]
    tq = _tile(seq, 256)
    band = min(window + tq, seq)
    gw = GROUP * LANE
    return pl.pallas_call(
        functools.partial(_band_kernel, tq=tq, window=window, band=band, use_sinks=use_sinks),
        grid=(b, NSA_KV, seq // tq),
        in_specs=[pl.BlockSpec(memory_space=pltpu.SMEM),
                  pl.BlockSpec((1, tq, gw), lambda b_, k, i: (b_, i, q_off // gw + k)),
                  pl.BlockSpec((1, seq, LANE), lambda b_, k, i: (b_, 0, k_off // LANE + k)),
                  pl.BlockSpec((1, seq, LANE), lambda b_, k, i: (b_, 0, v_off // LANE + k))],
        out_specs=pl.BlockSpec((1, tq, GROUP * HEAD_DIM), lambda b_, k, i: (b_, i, k)),
        out_shape=jax.ShapeDtypeStruct((b, seq, NSA_HEADS * HEAD_DIM), out_dtype),
        compiler_params=_cparams("parallel", "parallel", "parallel"),
        name=name,
    )(head_params, proj, proj, proj)


def _layer_norm(z, g, b):
    mu = jnp.mean(z, axis=-1, keepdims=True)
    zc = z - mu
    var = jnp.mean(zc * zc, axis=-1, keepdims=True)
    return zc * lax.rsqrt(var + LN_EPS) * g + b


def _merge_kernel(x_ref, xin_ref, oc_ref, os_ref, ow_ref, ga_ref, ob_ref, od_ref, wgm_ref, ex_ref,
                  wa_ref, wb_ref, wc_ref, wo_ref, lng_ref, lnb_ref, xo_ref, xb_ref, *, alpha):
    d = x_ref.shape[1]
    ga = jax.nn.sigmoid(ga_ref[...].astype(F32)).astype(BF16)
    oa = (_dot(ga, ex_ref[0]) * oc_ref[...] + _dot(ga, ex_ref[1]) * os_ref[...]
          + _dot(ga, ex_ref[2]) * ow_ref[...])
    gm = jax.nn.sigmoid(_dot(xin_ref[...], wgm_ref[...]))
    merged = (gm[:, 0:d] * _dot(oa.astype(BF16), wa_ref[...])
              + gm[:, d:2 * d] * _dot(ob_ref[...], wb_ref[...])
              + gm[:, 2 * d:3 * d] * _dot(od_ref[...], wc_ref[...]))
    z = alpha * x_ref[...] + _dot(merged.astype(BF16), wo_ref[...])
    y = _layer_norm(z, lng_ref[...], lnb_ref[...])
    xo_ref[...] = y
    xb_ref[...] = y.astype(BF16)


def _merge(x, xb, oc, os_, ow, proj2d, ob, od, wgm, expand, wa, wb, wc, wo, lng, lnb, alpha):
    t, d = x.shape
    tm = _tile(t, 256)
    ha = NSA_HEADS * HEAD_DIM
    row = lambda w: pl.BlockSpec((tm, w), lambda i: (i, 0))
    full = lambda a: pl.BlockSpec(a.shape, lambda i: (0,) * a.ndim)
    return pl.pallas_call(
        functools.partial(_merge_kernel, alpha=alpha),
        grid=(t // tm,),
        in_specs=[row(d), row(d), row(ha), row(ha), row(ha),
                  pl.BlockSpec((tm, LANE), lambda i: (i, _LAYOUT.ga // LANE)),
                  row(ha), row(ha), full(wgm), full(expand),
                  full(wa), full(wb), full(wc), full(wo), full(lng), full(lnb)],
        out_specs=[row(d), row(d)],
        out_shape=[jax.ShapeDtypeStruct((t, d), F32), jax.ShapeDtypeStruct((t, d), BF16)],
        compiler_params=_cparams("parallel"),
        name="merge_norm",
    )(x, xb, oc, os_, ow, proj2d, ob, od, wgm, expand, wa, wb, wc, wo, lng, lnb)


def _first_max(v, idx, n):
    mx = jnp.max(v, axis=0, keepdims=True)
    first = jnp.min(jnp.where(v == mx, idx, n), axis=0, keepdims=True)
    return mx, idx == first


def _router_kernel(x_ref, rw_ref, rb_ref, cw_ref):
    gsz = N_EXPERTS // N_GROUPS
    logits = lax.dot_general(rw_ref[...], x_ref[...], (((1,), (1,)), ((), ())),
                             precision=lax.Precision.HIGHEST,
                             preferred_element_type=F32)
    tm = logits.shape[1]
    scores = jax.nn.sigmoid(logits)
    biased = scores + rb_ref[...]
    iw = _iota((gsz, tm), 0)
    gs = []
    for g in range(N_GROUPS):
        blk = biased[g * gsz:(g + 1) * gsz]
        m1, pick = _first_max(blk, iw, gsz)
        m2 = jnp.max(jnp.where(pick, -jnp.inf, blk), axis=0, keepdims=True)
        gs.append(m1 + m2)
    gs = jnp.concatenate(gs, axis=0)
    ig = _iota((N_GROUPS, tm), 0)
    gmask = jnp.zeros((N_GROUPS, tm), F32)
    for _ in range(TOPK_GROUPS):
        _, pick = _first_max(gs, ig, N_GROUPS)
        gmask = jnp.where(pick, 1.0, gmask)
        gs = jnp.where(pick, -jnp.inf, gs)
    emask = jnp.concatenate(
        [jnp.broadcast_to(gmask[g:g + 1], (gsz, tm)) for g in range(N_GROUPS)], axis=0)
    cand = jnp.where(emask > 0.5, biased, NEG)
    ie = _iota((N_EXPERTS, tm), 0)
    chosen = jnp.zeros((N_EXPERTS, tm), F32)
    for _ in range(TOP_K):
        _, pick = _first_max(cand, ie, N_EXPERTS)
        chosen = jnp.where(pick, 1.0, chosen)
        cand = jnp.where(pick, -jnp.inf, cand)
    w = scores * chosen
    cw_ref[...] = w / jnp.sum(w, axis=0, keepdims=True) * ROUTED_SCALE


def _router(x, rw_t, rb):
    t, d = x.shape
    tm = _tile(t, 512)
    return pl.pallas_call(
        _router_kernel,
        grid=(t // tm,),
        in_specs=[pl.BlockSpec((tm, d), lambda i: (i, 0)),
                  pl.BlockSpec((N_EXPERTS, d), lambda i: (0, 0)),
                  pl.BlockSpec((N_EXPERTS, 1), lambda i: (0, 0))],
        out_specs=pl.BlockSpec((N_EXPERTS, tm), lambda i: (0, i)),
        out_shape=jax.ShapeDtypeStruct((N_EXPERTS, t), F32),
        compiler_params=_cparams("parallel"),
        name="router",
    )(x, rw_t, rb)


def _moe_kernel(x_ref, cw_ref, wg_ref, wu_ref, wd_ref, y_ref, *, ne):
    c = pl.program_id(1)
    x = x_ref[...]
    edim = wg_ref.shape[2]
    cw = cw_ref[...]
    hs = []
    for e in range(ne):
        h = jax.nn.silu(_dot(x, wg_ref[e])) * _dot(x, wu_ref[e]) * cw[:, e:e + 1]
        hs.append(h.astype(BF16))
    y = _dot(jnp.concatenate(hs, axis=1), wd_ref[...].reshape(ne * edim, wd_ref.shape[2]))

    @pl.when(c == 0)
    def _():
        y_ref[...] = y

    @pl.when(c != 0)
    def _():
        y_ref[...] += y


def _moe_dense(xb, cw_t, wg, wu, wd, layer):
    t, d = xb.shape
    edim = wg.shape[3]
    tm = _tile(t, 512)
    ne = 8
    cw = cw_t.reshape(N_EXPERTS // ne, ne, t).transpose(0, 2, 1)
    return pl.pallas_call(
        functools.partial(_moe_kernel, ne=ne),
        grid=(t // tm, N_EXPERTS // ne),
        in_specs=[pl.BlockSpec((tm, d), lambda i, c: (i, 0)),
                  pl.BlockSpec((None, tm, ne), lambda i, c: (c, i, 0)),
                  pl.BlockSpec((None, ne, d, edim), lambda i, c: (layer, c, 0, 0)),
                  pl.BlockSpec((None, ne, d, edim), lambda i, c: (layer, c, 0, 0)),
                  pl.BlockSpec((None, ne, edim, d), lambda i, c: (layer, c, 0, 0))],
        out_specs=pl.BlockSpec((tm, d), lambda i, c: (i, 0)),
        out_shape=jax.ShapeDtypeStruct((t, d), F32),
        compiler_params=_cparams("parallel", "arbitrary"),
        name="moe_experts",
    )(xb, cw, wg, wu, wd)


def _tail_kernel(x_ref, xb_ref, y_ref, p_ref, sg_ref, su_ref, sd_ref, pg_ref, pp_ref,
                 lng_ref, lnb_ref, xo_ref, xbo_ref, *, alpha):
    xb = xb_ref[...]
    h = jax.nn.silu(_dot(xb, sg_ref[...])) * _dot(xb, su_ref[...])
    shared = _dot(h.astype(BF16), sd_ref[...])
    ple = jax.nn.sigmoid(_dot(xb, pg_ref[...])) * _dot(p_ref[...].astype(BF16), pp_ref[...])
    z = alpha * x_ref[...] + y_ref[...] + shared + ple
    y = _layer_norm(z, lng_ref[...], lnb_ref[...])
    xo_ref[...] = y
    xbo_ref[...] = y.astype(BF16)


def _tail(x, xb, y, p, sg, su, sd, pg, pp, lng, lnb, alpha):
    t, d = x.shape
    tm = _tile(t, 256)
    row = lambda w: pl.BlockSpec((tm, w), lambda i: (i, 0))
    full = lambda a: pl.BlockSpec(a.shape, lambda i: (0,) * a.ndim)
    return pl.pallas_call(
        functools.partial(_tail_kernel, alpha=alpha),
        grid=(t // tm,),
        in_specs=[row(d), row(d), row(d), row(p.shape[1]), full(sg), full(su), full(sd),
                  full(pg), full(pp), full(lng), full(lnb)],
        out_specs=[row(d), row(d)],
        out_shape=[jax.ShapeDtypeStruct((t, d), F32), jax.ShapeDtypeStruct((t, d), BF16)],
        compiler_params=_cparams("parallel"),
        name="ffn_tail_norm",
    )(x, xb, y, p, sg, su, sd, pg, pp, lng, lnb)


def _slot_weights(w):
    d = w.shape[0]
    o = IN_OFFS
    qs = HEAD_DIM ** -0.5

    def heads(off, n, width=HEAD_DIM, slot=LANE, scale=1.0):
        blk = w[:, off:off + n * width].reshape(d, n, width) * scale
        return jnp.pad(blk, ((0, 0), (0, 0), (0, slot - width))).reshape(d, n * slot)

    gates = w[:, o[7]:o[8]].reshape(d, NSA_HEADS, 3).transpose(0, 2, 1).reshape(d, 3 * NSA_HEADS)
    parts = [heads(o[0], NSA_HEADS, scale=qs), heads(o[8], SWA_HEADS, scale=qs),
             heads(o[11], 2 * DIFF_HEADS, scale=qs), heads(o[12], 2 * DIFF_HEADS),
             heads(o[13], DIFF_HEADS, 2 * HEAD_DIM, 2 * LANE),
             heads(o[3], NSA_KV), heads(o[4], NSA_KV), heads(o[5], NSA_KV), heads(o[6], NSA_KV),
             heads(o[9], SWA_KV), heads(o[10], SWA_KV),
             w[:, o[1]:o[2]], w[:, o[2]:o[3]],
             jnp.pad(gates, ((0, 0), (0, LANE - 3 * NSA_HEADS)))]
    out = jnp.concatenate(parts, axis=1)
    assert out.shape[1] == _LAYOUT.width
    return out.astype(BF16)


def _compressed_slots(c, pos_cols):
    b, nb, hkv, dh = c.shape
    c = c.transpose(0, 2, 1, 3)
    extra = jnp.zeros((b, hkv, nb, LANE - dh), F32)
    if pos_cols:
        cpos = np.arange(nb) * NSA_BLOCK + NSA_BLOCK - 1
        cols = np.zeros((nb, LANE - dh), np.float32)
        cols[:, 0] = cpos // LANE
        cols[:, 1] = cpos % LANE
        extra = extra + jnp.asarray(cols)
    c = jnp.concatenate([c, extra], axis=-1)
    c = jnp.pad(c, ((0, 0), (0, 0), (0, NBLK_PAD - nb), (0, 0)))
    return c.astype(BF16)


def kernel(x, p, w_in, nsa_pe_k, nsa_w1_k, nsa_w2_k, nsa_pe_v, nsa_w1_v, nsa_w2_v, swa_sinks,
           diff_lq1, diff_lk1, diff_lq2, diff_lk2, diff_norm_g, w_branch_a, w_branch_b, w_branch_c,
           w_out, ln1_g, ln1_b, router_w, router_bias, exp_w_gate, exp_w_up, exp_w_down,
           sh_w_gate, sh_w_up, sh_w_down, ple_w_proj, ple_w_gate, ln2_g, ln2_b):
    bsz, seq, d = x.shape
    depth = w_in.shape[0]
    t = bsz * seq
    alpha = (2 * depth) ** 0.25
    lay = _LAYOUT
    nb = seq // NSA_BLOCK
    assert nb <= NBLK_PAD and seq % LANE == 0
    n_sel = min(NSA_TOPN, nb)

    aux = jnp.asarray(lay.aux)
    blk_mask = jnp.asarray(
        (np.arange(seq)[:, None] // NSA_BLOCK == np.arange(NBLK_PAD)[None, :]) * MASK_NEG, BF16)
    ex = np.zeros((3, LANE, NSA_HEADS * HEAD_DIM), np.float32)
    for c in range(3):
        for h in range(NSA_HEADS):
            ex[c, c * NSA_HEADS + h, h * HEAD_DIM:(h + 1) * HEAD_DIM] = 1.0
    expand = jnp.asarray(ex, BF16)
    slopes8 = jnp.asarray(_slopes(NSA_HEADS), F32)

    wg_all = exp_w_gate.astype(BF16)
    wu_all = exp_w_up.astype(BF16)
    wd_all = exp_w_down.astype(BF16)
    xf = x.reshape(t, d)
    xb = xf.astype(BF16)
    for i in range(depth):
        w_slots = _slot_weights(w_in[i])
        proj2d = _project(xb, w_slots, aux, seq, BF16, 1408)
        proj = proj2d.reshape(bsz, seq, lay.width)

        def blocks(off):
            c = proj[:, :, off:off + LANE].reshape(bsz, nb, NSA_BLOCK, NSA_KV, HEAD_DIM)
            return c.transpose(0, 1, 3, 2, 4).reshape(bsz * nb * NSA_KV, NSA_BLOCK * HEAD_DIM)
        cmp = _compress(
            jnp.stack([blocks(lay.kc), blocks(lay.vc)]),
            jnp.stack([nsa_pe_k[i].reshape(1, -1), nsa_pe_v[i].reshape(1, -1)]),
            jnp.stack([nsa_w1_k[i], nsa_w1_v[i]]).astype(BF16),
            jnp.stack([nsa_w2_k[i], nsa_w2_v[i]]).astype(BF16))
        cmp = cmp.reshape(2, bsz, nb, NSA_KV, HEAD_DIM)
        kca = _compressed_slots(cmp[0], True)
        vca = _compressed_slots(cmp[1], False)

        o_cmp, selc, touch = _nsa_compressed(proj, kca, vca, seq, n_sel)
        ks = proj[:, :, lay.ks:lay.ks + NSA_KV * LANE].reshape(bsz, seq, NSA_KV, LANE).transpose(0, 2, 1, 3)
        ksel = jnp.concatenate(
            [ks, jnp.broadcast_to(blk_mask[None, None], (bsz, NSA_KV, seq, NBLK_PAD))], axis=-1)
        o_sel = _nsa_selected(proj, selc, touch, ksel, seq)
        hp_a = jnp.stack([slopes8, jnp.zeros_like(slopes8)])
        o_win = _banded(proj, hp_a, seq, NSA_WINDOW, lay.qa, lay.kw, lay.vw, False, F32, "nsa_window")
        hp_b = jnp.stack([slopes8, swa_sinks[i].astype(F32)])
        o_b = _banded(proj, hp_b, seq, SWA_WINDOW, lay.qb, lay.kb, lay.vb, True, BF16, "swa_sinks")

        lambda_init = 0.8 - 0.6 * math.exp(-0.3 * i)
        lam_vecs = jnp.stack([diff_lq1[i], diff_lk1[i], diff_lq2[i], diff_lk2[i]]).astype(F32)
        o_d = _diff_attention(proj, lam_vecs, diff_norm_g[i].reshape(1, -1), seq, lambda_init)

        ha = NSA_HEADS * HEAD_DIM
        xf, xb = _merge(
            xf, xb, o_cmp.reshape(t, ha), o_sel.reshape(t, ha), o_win.reshape(t, ha), proj2d,
            o_b.reshape(t, ha), o_d.reshape(t, ha), w_in[i][:, GM_OFF:].astype(BF16), expand,
            w_branch_a[i].astype(BF16), w_branch_b[i].astype(BF16), w_branch_c[i].astype(BF16),
            w_out[i].astype(BF16), ln1_g[i].reshape(1, d), ln1_b[i].reshape(1, d), alpha)

        cw_t = _router(xf, router_w[i].T, router_bias[i].reshape(-1, 1))
        y = _moe_dense(xb, cw_t, wg_all, wu_all, wd_all, i)
        xf, xb = _tail(xf, xb, y, p[i].reshape(t, -1),
                       sh_w_gate[i].astype(BF16), sh_w_up[i].astype(BF16), sh_w_down[i].astype(BF16),
                       ple_w_gate[i].astype(BF16), ple_w_proj[i].astype(BF16),
                       ln2_g[i].reshape(1, d), ln2_b[i].reshape(1, d), alpha)
    return xf.reshape(bsz, seq, d)
```

```python
import functools
import math

import numpy as np
import jax
import jax.numpy as jnp
from jax import lax
from jax.experimental import pallas as pl
from jax.experimental.pallas import tpu as pltpu

F32 = jnp.float32
BF16 = jnp.bfloat16

HEAD_DIM = 64
NSA_HEADS = 8
NSA_KV = 2
NSA_BLOCK = 64
NSA_TOPN = 16
NSA_LOCAL = 2
NSA_WINDOW = 512
SWA_HEADS = 8
SWA_KV = 2
SWA_WINDOW = 128
DIFF_HEADS = 4
N_EXPERTS = 64
TOP_K = 8
N_GROUPS = 8
TOPK_GROUPS = 4
ROUTED_SCALE = 2.5
LN_EPS = 1e-5
NEG = -1e30
FORCE = 1e4
MASK_NEG = -(2.0 ** 100)

LANE = 128
GROUP = NSA_HEADS // NSA_KV
NBLK_PAD = 128
VMEM_LIMIT = 56 * 1024 * 1024

IN_SIZES = (NSA_HEADS * HEAD_DIM,) + (NSA_KV * HEAD_DIM,) * 6 + (NSA_HEADS * 3,) + \
    (SWA_HEADS * HEAD_DIM, SWA_KV * HEAD_DIM, SWA_KV * HEAD_DIM) + \
    (DIFF_HEADS * 2 * HEAD_DIM,) * 3
IN_OFFS = np.concatenate([[0], np.cumsum(IN_SIZES)]).tolist()
GM_OFF = IN_OFFS[-1]


def _slopes(n):
    return [2.0 ** (-8.0 * (h + 1) / n) for h in range(n)]


class _Layout:
    def __init__(self):
        src, scale, bias, pa, pb = [], [], [], [], []

        def slot(cols, sc=1.0, consts=(), pos=False, width=LANE):
            s = [-1] * width
            c = [0.0] * width
            a = [0.0] * width
            b = [0.0] * width
            s[:len(cols)] = cols
            for off, val in consts:
                c[off] = val
            if pos:
                a[HEAD_DIM] = 1.0
                b[HEAD_DIM + 1] = 1.0
            start = len(src)
            src.extend(s)
            scale.extend([sc] * width)
            bias.extend(c)
            pa.extend(a)
            pb.extend(b)
            return start

        def rng(base, n):
            return list(range(base, base + n))

        qs = HEAD_DIM ** -0.5
        o = IN_OFFS
        sl8 = _slopes(NSA_HEADS)
        sl4 = _slopes(DIFF_HEADS)
        self.qa = len(src)
        for h in range(NSA_HEADS):
            slot(rng(o[0] + h * HEAD_DIM, HEAD_DIM), qs,
                 [(HEAD_DIM, sl8[h] * LANE), (HEAD_DIM + 1, sl8[h])])
        self.qb = len(src)
        for h in range(SWA_HEADS):
            slot(rng(o[8] + h * HEAD_DIM, HEAD_DIM), qs,
                 [(HEAD_DIM, sl8[h] * LANE), (HEAD_DIM + 1, sl8[h])])
        self.qd = len(src)
        for h in range(DIFF_HEADS):
            for r in range(2):
                slot(rng(o[11] + (h * 2 + r) * HEAD_DIM, HEAD_DIM), qs,
                     [(HEAD_DIM, sl4[h] * LANE), (HEAD_DIM + 1, sl4[h])])
        self.kd = len(src)
        for h in range(DIFF_HEADS):
            for r in range(2):
                slot(rng(o[12] + (h * 2 + r) * HEAD_DIM, HEAD_DIM), pos=True)
        self.vd = len(src)
        for h in range(DIFF_HEADS):
            slot(rng(o[13] + h * 2 * HEAD_DIM, 2 * HEAD_DIM),
                 consts=[(2 * HEAD_DIM, 1.0)], width=2 * LANE)

        def kv_slots(k_src, v_src):
            k_off = len(src)
            for k in range(NSA_KV):
                slot(rng(k_src + k * HEAD_DIM, HEAD_DIM), pos=True)
            v_off = len(src)
            for k in range(NSA_KV):
                slot(rng(v_src + k * HEAD_DIM, HEAD_DIM), consts=[(HEAD_DIM, 1.0)])
            return k_off, v_off

        self.ks, self.vs = kv_slots(o[3], o[4])
        self.kw, self.vw = kv_slots(o[5], o[6])
        self.kb, self.vb = kv_slots(o[9], o[10])
        self.kc = slot(rng(o[1], NSA_KV * HEAD_DIM))
        self.vc = slot(rng(o[2], NSA_KV * HEAD_DIM))
        self.ga = slot([o[7] + h * 3 + c for c in range(3) for h in range(NSA_HEADS)])
        self.width = len(src)
        self.src = np.asarray(src, np.int32)
        self.scale = np.asarray(scale, np.float32)
        aux = np.zeros((8, self.width), np.float32)
        aux[0] = bias
        aux[1] = pa
        aux[2] = pb
        self.aux = aux


_LAYOUT = _Layout()


def _cparams(*sem):
    return pltpu.CompilerParams(dimension_semantics=sem, vmem_limit_bytes=VMEM_LIMIT)


def _tile(n, pref):
    t = min(n, pref)
    assert n % t == 0, (n, pref)
    return t


def _iota(shape, dim):
    return lax.broadcasted_iota(jnp.int32, shape, dim)


def _dot(a, b):
    return jnp.dot(a, b, preferred_element_type=F32)


def _dot_nt(a, b):
    return lax.dot_general(a, b, (((1,), (1,)), ((), ())), preferred_element_type=F32)


def _proj_kernel(x_ref, w_ref, aux_ref, o_ref, *, seq, tm):
    acc = _dot(x_ref[...], w_ref[...])
    pos = (pl.program_id(0) * tm) % seq + _iota((tm, 1), 0)
    a = (pos >> 7).astype(F32)
    b = (pos & (LANE - 1)).astype(F32)
    aux = aux_ref[...]
    o_ref[...] = (acc + aux[0:1] + a * aux[1:2] + b * aux[2:3]).astype(o_ref.dtype)


def _project(xb, w, aux, seq, out_dtype, tn_pref):
    t, d = xb.shape
    n = w.shape[1]
    tm = _tile(t, 1024)
    tn = _tile(n, tn_pref)
    return pl.pallas_call(
        functools.partial(_proj_kernel, seq=seq, tm=tm),
        grid=(t // tm, n // tn),
        in_specs=[pl.BlockSpec((tm, d), lambda i, j: (i, 0)),
                  pl.BlockSpec((d, tn), lambda i, j: (0, j)),
                  pl.BlockSpec((8, tn), lambda i, j: (0, j))],
        out_specs=pl.BlockSpec((tm, tn), lambda i, j: (i, j)),
        out_shape=jax.ShapeDtypeStruct((t, n), out_dtype),
        compiler_params=_cparams("parallel", "parallel"),
        name="proj",
    )(xb, w, aux)


def _compress_kernel(x_ref, pe_ref, w1_ref, w2_ref, o_ref):
    xb = (x_ref[0].astype(F32) + pe_ref[0]).astype(BF16)
    h = jax.nn.gelu(_dot(xb, w1_ref[0]))
    o_ref[0] = _dot(h.astype(BF16), w2_ref[0])


def _compress(xs, pes, w1s, w2s):
    _, r, kdim = xs.shape
    hid = w1s.shape[2]
    dh = w2s.shape[2]
    return pl.pallas_call(
        _compress_kernel,
        grid=(2,),
        in_specs=[pl.BlockSpec((1, r, kdim), lambda i: (i, 0, 0)),
                  pl.BlockSpec((1, 1, kdim), lambda i: (i, 0, 0)),
                  pl.BlockSpec((1, kdim, hid), lambda i: (i, 0, 0)),
                  pl.BlockSpec((1, hid, dh), lambda i: (i, 0, 0))],
        out_specs=pl.BlockSpec((1, r, dh), lambda i: (i, 0, 0)),
        out_shape=jax.ShapeDtypeStruct((2, r, dh), F32),
        compiler_params=_cparams("parallel"),
        name="nsa_compress",
    )(xs, pes, w1s, w2s)


def _stack_heads(q, n):
    return jnp.concatenate([q[:, g * LANE:(g + 1) * LANE] for g in range(n)], axis=0)


def _unstack_heads(o, n, tq, width):
    return jnp.concatenate([o[g * tq:(g + 1) * tq, :width] for g in range(n)], axis=1)


def _cmp_kernel(q_ref, kc_ref, vc_ref, oc_ref, sel_ref, touch_ref, *, tq, n_sel):
    i = pl.program_id(2)
    qs = _stack_heads(q_ref[0], GROUP)
    s = _dot_nt(qs, kc_ref[0, 0])
    t1 = i * tq + _iota((tq, 1), 0)
    t = jnp.concatenate([t1] * GROUP, axis=0)
    n = _iota((1, NBLK_PAD), 1)
    valid = (n * NSA_BLOCK + (NSA_BLOCK - 1)) <= t
    s = jnp.where(valid, s, NEG)
    mx = jnp.max(s, axis=-1, keepdims=True)
    e = jnp.where(valid, jnp.exp(s - mx), 0.0)
    p = e / jnp.maximum(jnp.sum(e, axis=-1, keepdims=True), 1e-30)
    o = _dot(p.astype(BF16), vc_ref[0, 0])
    oc_ref[0] = _unstack_heads(o, GROUP, tq, HEAD_DIM)

    imp = p[0:tq]
    for g in range(1, GROUP):
        imp = imp + p[g * tq:(g + 1) * tq]
    nb = _iota((NBLK_PAD, 1), 0)
    rel = ((i * tq + _iota((1, tq), 1)) >> 6) - nb
    forced = (nb == 0) | ((rel >= 0) & (rel < NSA_LOCAL))
    val = jnp.where(forced, FORCE, jnp.where(rel >= 0, imp.T, -1.0))
    sel = jnp.zeros((NBLK_PAD, tq), F32)
    nf = nb.astype(F32)
    for _ in range(n_sel):
        mx = jnp.max(val, axis=0, keepdims=True)
        idx = jnp.min(jnp.where(val == mx, nf, float(NBLK_PAD)), axis=0, keepdims=True)
        pick = nf == idx
        sel = jnp.where(pick & (mx >= 0.0), 1.0, sel)
        val = jnp.where(pick, -jnp.inf, val)
    sel = sel.T
    sel_ref[0, 0] = (1.0 - sel).astype(BF16)
    touch_ref[0, 0, 0] = jnp.broadcast_to(jnp.max(sel, axis=0, keepdims=True), (8, NBLK_PAD))


def _nsa_compressed(proj, kca, vca, seq, n_sel):
    b = proj.shape[0]
    tq = _tile(seq, 256)
    lay = _LAYOUT
    gw = GROUP * LANE
    return pl.pallas_call(
        functools.partial(_cmp_kernel, tq=tq, n_sel=n_sel),
        grid=(b, NSA_KV, seq // tq),
        in_specs=[pl.BlockSpec((1, tq, gw), lambda b_, k, i: (b_, i, lay.qa // gw + k)),
                  pl.BlockSpec((1, 1, NBLK_PAD, LANE), lambda b_, k, i: (b_, k, 0, 0)),
                  pl.BlockSpec((1, 1, NBLK_PAD, LANE), lambda b_, k, i: (b_, k, 0, 0))],
        out_specs=[pl.BlockSpec((1, tq, GROUP * HEAD_DIM), lambda b_, k, i: (b_, i, k)),
                   pl.BlockSpec((1, 1, tq, NBLK_PAD), lambda b_, k, i: (b_, k, i, 0)),
                   pl.BlockSpec((1, 1, 1, 8, NBLK_PAD), lambda b_, k, i: (b_, k, i, 0, 0))],
        out_shape=[jax.ShapeDtypeStruct((b, seq, NSA_HEADS * HEAD_DIM), F32),
                   jax.ShapeDtypeStruct((b, NSA_KV, seq, NBLK_PAD), BF16),
                   jax.ShapeDtypeStruct((b, NSA_KV, seq // tq, 8, NBLK_PAD), F32)],
        compiler_params=_cparams("parallel", "parallel", "parallel"),
        name="nsa_compressed",
    )(proj, kca, vca)


def _flash(chunk_fns, v_ref, i, tq, tk, m_ref, acc_ref, s_refs, n_active=None, tile_at=None):
    m_ref[...] = jnp.full(m_ref.shape, NEG, F32)
    acc_ref[...] = jnp.zeros(acc_ref.shape, F32)
    acc_reps = acc_ref.shape[1] // LANE
    n_full = (i * tq) // tk
    if n_active is None:
        n_active = n_full
        tile_at = lambda n: n

    def nth_tile(n):
        return jnp.where(n < n_active, tile_at(n), n_full)

    def issue(n, s_ref):
        start = pl.multiple_of(nth_tile(n) * tk, tk)
        for c, fn in enumerate(chunk_fns):
            s_ref[c * tq:(c + 1) * tq, :] = fn(start)

    def accumulate(j, s_ref, visible):
        start = pl.multiple_of(j * tk, tk)
        vt = v_ref[pl.ds(start, tk), :]
        for c in range(len(chunk_fns)):
            rows = slice(c * tq, (c + 1) * tq)
            s = s_ref[rows, :]
            if visible is not None:
                s = jnp.where(visible, s, NEG)
            m_old = m_ref[rows, :]
            m_new = jnp.maximum(m_old, jnp.max(s, axis=-1, keepdims=True))
            p = jnp.exp((s - jnp.concatenate([m_new] * (tk // LANE), axis=1)).astype(BF16))
            alpha = jnp.concatenate([jnp.exp(m_old - m_new)] * acc_reps, axis=1)
            acc_ref[rows, :] = alpha * acc_ref[rows, :] + _dot(p, vt)
            m_ref[rows, :] = m_new

    s0, s1 = s_refs
    issue(0, s0)

    def body(pair, carry):
        n = 2 * pair
        issue(n + 1, s1)
        accumulate(nth_tile(n), s0, None)
        issue(n + 2, s0)
        accumulate(nth_tile(n + 1), s1, None)
        return carry

    pairs = n_active // 2
    lax.fori_loop(0, pairs, body, 0)
    visible = (_iota((tq, tk), 1) - _iota((tq, tk), 0)) <= i * tq - n_full * tk
    leftover = n_active - 2 * pairs

    @pl.when(leftover == 1)
    def _():
        issue(n_active, s1)
        accumulate(nth_tile(n_active - 1), s0, None)
        accumulate(n_full, s1, visible)

    @pl.when(leftover == 0)
    def _():
        accumulate(n_full, s0, visible)


def _sel_kernel(order_ref, count_ref, q_ref, sel_ref, k_ref, v_ref, o_ref, qa_ref, m_ref, acc_ref,
                s0_ref, s1_ref, *, tq, tk, n_tiles):
    i = pl.program_id(2)
    tile_id = (pl.program_id(0) * pl.num_programs(1) + pl.program_id(1)) * pl.num_programs(2) + i
    q = q_ref[0]
    sc = sel_ref[0, 0]
    for g in range(GROUP):
        qa_ref[g * tq:(g + 1) * tq, 0:LANE] = q[:, g * LANE:(g + 1) * LANE]
        qa_ref[g * tq:(g + 1) * tq, LANE:2 * LANE] = sc
    kv = k_ref.at[0, 0]

    def head_scores(g):
        return lambda start: _dot_nt(qa_ref[g * tq:(g + 1) * tq, :], kv[pl.ds(start, tk), :])

    _flash([head_scores(g) for g in range(GROUP)], v_ref.at[0], i, tq, tk, m_ref, acc_ref, (s0_ref, s1_ref),
           n_active=count_ref[tile_id], tile_at=lambda n: order_ref[tile_id * n_tiles + n])
    acc = acc_ref[...]
    o = acc / acc[:, HEAD_DIM:HEAD_DIM + 1]
    o_ref[0] = _unstack_heads(o, GROUP, tq, HEAD_DIM)


def _nsa_selected(proj, selc, touch, ksel, seq):
    b = proj.shape[0]
    tq = _tile(seq, 256)
    tk = _tile(seq, 512)
    n_tiles = seq // tk
    lay = _LAYOUT
    gw = GROUP * LANE
    per_tile = touch[:, :, :, 0, :seq // NSA_BLOCK].reshape(b, NSA_KV, seq // tq, n_tiles, tk // NSA_BLOCK)
    n_full = (np.arange(seq // tq) * tq) // tk
    active = (jnp.max(per_tile, axis=-1) > 0.0) & jnp.asarray(np.arange(n_tiles)[None, :] < n_full[:, None])
    order = jnp.argsort(jnp.where(active, 0, 1), axis=-1, stable=True).astype(jnp.int32).reshape(-1)
    count = jnp.sum(active, axis=-1).astype(jnp.int32).reshape(-1)
    grid_spec = pltpu.PrefetchScalarGridSpec(
        num_scalar_prefetch=2,
        grid=(b, NSA_KV, seq // tq),
        in_specs=[pl.BlockSpec((1, tq, gw), lambda b_, k, i, o, c: (b_, i, lay.qa // gw + k)),
                  pl.BlockSpec((1, 1, tq, NBLK_PAD), lambda b_, k, i, o, c: (b_, k, i, 0)),
                  pl.BlockSpec((1, 1, seq, 2 * LANE), lambda b_, k, i, o, c: (b_, k, 0, 0)),
                  pl.BlockSpec((1, seq, LANE), lambda b_, k, i, o, c: (b_, 0, lay.vs // LANE + k))],
        out_specs=pl.BlockSpec((1, tq, GROUP * HEAD_DIM), lambda b_, k, i, o, c: (b_, i, k)),
        scratch_shapes=[pltpu.VMEM((GROUP * tq, 2 * LANE), BF16),
                        pltpu.VMEM((GROUP * tq, LANE), F32),
                        pltpu.VMEM((GROUP * tq, LANE), F32),
                        pltpu.VMEM((GROUP * tq, tk), F32),
                        pltpu.VMEM((GROUP * tq, tk), F32)])
    return pl.pallas_call(
        functools.partial(_sel_kernel, tq=tq, tk=tk, n_tiles=n_tiles),
        grid_spec=grid_spec,
        out_shape=jax.ShapeDtypeStruct((b, seq, NSA_HEADS * HEAD_DIM), F32),
        compiler_params=_cparams("parallel", "parallel", "arbitrary"),
        name="nsa_selected",
    )(order, count, proj, selc, ksel, proj)


def _diff_kernel(lam_ref, q_ref, k_ref, v_ref, g_ref, o_ref, m_ref, acc_ref, s0_ref, s1_ref,
                 *, tq, tk, lambda_init):
    i = pl.program_id(2)
    kv = k_ref.at[0]

    def map_scores(r):
        cols = slice(r * LANE, (r + 1) * LANE)
        return lambda start: _dot_nt(q_ref[0, :, cols], kv[pl.ds(start, tk), cols])

    _flash([map_scores(0), map_scores(1)], v_ref.at[0], i, tq, tk, m_ref, acc_ref, (s0_ref, s1_ref))
    acc = acc_ref[...]
    dv = 2 * HEAD_DIM
    a = acc[:, :dv] / acc[:, dv:dv + 1]
    lv = lam_ref[...]
    lam = (jnp.exp(jnp.sum(lv[0:1] * lv[1:2], axis=-1, keepdims=True))
           - jnp.exp(jnp.sum(lv[2:3] * lv[3:4], axis=-1, keepdims=True)) + lambda_init)
    o = a[0:tq] - lam * a[tq:2 * tq]
    o = o * lax.rsqrt(jnp.mean(o * o, axis=-1, keepdims=True) + LN_EPS) * g_ref[...]
    o_ref[0] = (o * (1.0 - lambda_init)).astype(o_ref.dtype)


def _diff_attention(proj, lam_vecs, norm_g, seq, lambda_init):
    b = proj.shape[0]
    tq = _tile(seq, 1024)
    tk = _tile(seq, 1024)
    lay = _LAYOUT
    w2 = 2 * LANE
    return pl.pallas_call(
        functools.partial(_diff_kernel, tq=tq, tk=tk, lambda_init=lambda_init),
        grid=(b, DIFF_HEADS, seq // tq),
        in_specs=[pl.BlockSpec((4, HEAD_DIM), lambda b_, h, i: (0, 0)),
                  pl.BlockSpec((1, tq, w2), lambda b_, h, i: (b_, i, lay.qd // w2 + h)),
                  pl.BlockSpec((1, seq, w2), lambda b_, h, i: (b_, 0, lay.kd // w2 + h)),
                  pl.BlockSpec((1, seq, w2), lambda b_, h, i: (b_, 0, lay.vd // w2 + h)),
                  pl.BlockSpec((1, 2 * HEAD_DIM), lambda b_, h, i: (0, 0))],
        out_specs=pl.BlockSpec((1, tq, 2 * HEAD_DIM), lambda b_, h, i: (b_, i, h)),
        out_shape=jax.ShapeDtypeStruct((b, seq, DIFF_HEADS * 2 * HEAD_DIM), BF16),
        scratch_shapes=[pltpu.VMEM((2 * tq, LANE), F32),
                        pltpu.VMEM((2 * tq, w2), F32),
                        pltpu.VMEM((2 * tq, tk), F32),
                        pltpu.VMEM((2 * tq, tk), F32)],
        compiler_params=_cparams("parallel", "parallel", "arbitrary"),
        name="diff_attention",
    )(lam_vecs, proj, proj, proj, norm_g)


def _band_kernel(hp_ref, q_ref, k_ref, v_ref, o_ref, *, tq, window, band, use_sinks):
    kvh = pl.program_id(1)
    i = pl.program_id(2)
    start = pl.multiple_of(jnp.maximum(i * tq - window, 0), LANE)
    kb = k_ref[0, pl.ds(start, band), :]
    vb = v_ref[0, pl.ds(start, band), :]
    t1 = i * tq + _iota((tq, 1), 0)
    dist = t1 - (start + _iota((1, band), 1))
    visible = (dist >= 0) & (dist < window)
    outs = []
    scores = [_dot_nt(q_ref[0, :, g * LANE:(g + 1) * LANE], kb) for g in range(GROUP)]
    for g in range(GROUP):
        s = jnp.where(visible, scores[g], NEG)
        mx = jnp.max(s, axis=-1, keepdims=True)
        if use_sinks:
            h = kvh * GROUP + g
            sink = hp_ref[1, h] + hp_ref[0, h] * t1.astype(F32)
            mx = jnp.maximum(mx, sink)
        acc = _dot(jnp.exp((s - mx).astype(BF16)), vb)
        den = acc[:, HEAD_DIM:HEAD_DIM + 1]
        if use_sinks:
            den = den + jnp.exp(sink - mx)
        outs.append((acc / den)[:, :HEAD_DIM])
    o_ref[0] = jnp.concatenate(outs, axis=1).astype(o_ref.dtype)


def _banded(proj, head_params, seq, window, q_off, k_off, v_off, use_sinks, out_dtype, name):
    b = proj.shape[0]
    tq = _tile(seq, 256)
    band = min(window + tq, seq)
    gw = GROUP * LANE
    return pl.pallas_call(
        functools.partial(_band_kernel, tq=tq, window=window, band=band, use_sinks=use_sinks),
        grid=(b, NSA_KV, seq // tq),
        in_specs=[pl.BlockSpec(memory_space=pltpu.SMEM),
                  pl.BlockSpec((1, tq, gw), lambda b_, k, i: (b_, i, q_off // gw + k)),
                  pl.BlockSpec((1, seq, LANE), lambda b_, k, i: (b_, 0, k_off // LANE + k)),
                  pl.BlockSpec((1, seq, LANE), lambda b_, k, i: (b_, 0, v_off // LANE + k))],
        out_specs=pl.BlockSpec((1, tq, GROUP * HEAD_DIM), lambda b_, k, i: (b_, i, k)),
        out_shape=jax.ShapeDtypeStruct((b, seq, NSA_HEADS * HEAD_DIM), out_dtype),
        compiler_params=_cparams("parallel", "parallel", "parallel"),
        name=name,
    )(head_params, proj, proj, proj)


def _layer_norm(z, g, b):
    mu = jnp.mean(z, axis=-1, keepdims=True)
    zc = z - mu
    var = jnp.mean(zc * zc, axis=-1, keepdims=True)
    return zc * lax.rsqrt(var + LN_EPS) * g + b


def _merge_kernel(x_ref, xin_ref, oc_ref, os_ref, ow_ref, ga_ref, ob_ref, od_ref, wgm_ref, ex_ref,
                  wa_ref, wb_ref, wc_ref, wo_ref, lng_ref, lnb_ref, xo_ref, xb_ref, *, alpha):
    d = x_ref.shape[1]
    ga = jax.nn.sigmoid(ga_ref[...].astype(F32)).astype(BF16)
    oa = (_dot(ga, ex_ref[0]) * oc_ref[...] + _dot(ga, ex_ref[1]) * os_ref[...]
          + _dot(ga, ex_ref[2]) * ow_ref[...])
    gm = jax.nn.sigmoid(_dot(xin_ref[...], wgm_ref[...]))
    merged = (gm[:, 0:d] * _dot(oa.astype(BF16), wa_ref[...])
              + gm[:, d:2 * d] * _dot(ob_ref[...], wb_ref[...])
              + gm[:, 2 * d:3 * d] * _dot(od_ref[...], wc_ref[...]))
    z = alpha * x_ref[...] + _dot(merged.astype(BF16), wo_ref[...])
    y = _layer_norm(z, lng_ref[...], lnb_ref[...])
    xo_ref[...] = y
    xb_ref[...] = y.astype(BF16)


def _merge(x, xb, oc, os_, ow, proj2d, ob, od, wgm, expand, wa, wb, wc, wo, lng, lnb, alpha):
    t, d = x.shape
    tm = _tile(t, 256)
    ha = NSA_HEADS * HEAD_DIM
    row = lambda w: pl.BlockSpec((tm, w), lambda i: (i, 0))
    full = lambda a: pl.BlockSpec(a.shape, lambda i: (0,) * a.ndim)
    return pl.pallas_call(
        functools.partial(_merge_kernel, alpha=alpha),
        grid=(t // tm,),
        in_specs=[row(d), row(d), row(ha), row(ha), row(ha),
                  pl.BlockSpec((tm, LANE), lambda i: (i, _LAYOUT.ga // LANE)),
                  row(ha), row(ha), full(wgm), full(expand),
                  full(wa), full(wb), full(wc), full(wo), full(lng), full(lnb)],
        out_specs=[row(d), row(d)],
        out_shape=[jax.ShapeDtypeStruct((t, d), F32), jax.ShapeDtypeStruct((t, d), BF16)],
        compiler_params=_cparams("parallel"),
        name="merge_norm",
    )(x, xb, oc, os_, ow, proj2d, ob, od, wgm, expand, wa, wb, wc, wo, lng, lnb)


def _first_max(v, idx, n):
    mx = jnp.max(v, axis=0, keepdims=True)
    first = jnp.min(jnp.where(v == mx, idx, n), axis=0, keepdims=True)
    return mx, idx == first


def _router_kernel(x_ref, rw_ref, rb_ref, cw_ref):
    gsz = N_EXPERTS // N_GROUPS
    logits = lax.dot_general(rw_ref[...], x_ref[...], (((1,), (1,)), ((), ())),
                             precision=lax.Precision.HIGHEST,
                             preferred_element_type=F32)
    tm = logits.shape[1]
    scores = jax.nn.sigmoid(logits)
    biased = scores + rb_ref[...]
    iw = _iota((gsz, tm), 0)
    gs = []
    for g in range(N_GROUPS):
        blk = biased[g * gsz:(g + 1) * gsz]
        m1, pick = _first_max(blk, iw, gsz)
        m2 = jnp.max(jnp.where(pick, -jnp.inf, blk), axis=0, keepdims=True)
        gs.append(m1 + m2)
    gs = jnp.concatenate(gs, axis=0)
    ig = _iota((N_GROUPS, tm), 0)
    gmask = jnp.zeros((N_GROUPS, tm), F32)
    for _ in range(TOPK_GROUPS):
        _, pick = _first_max(gs, ig, N_GROUPS)
        gmask = jnp.where(pick, 1.0, gmask)
        gs = jnp.where(pick, -jnp.inf, gs)
    emask = jnp.concatenate(
        [jnp.broadcast_to(gmask[g:g + 1], (gsz, tm)) for g in range(N_GROUPS)], axis=0)
    cand = jnp.where(emask > 0.5, biased, NEG)
    ie = _iota((N_EXPERTS, tm), 0)
    chosen = jnp.zeros((N_EXPERTS, tm), F32)
    for _ in range(TOP_K):
        _, pick = _first_max(cand, ie, N_EXPERTS)
        chosen = jnp.where(pick, 1.0, chosen)
        cand = jnp.where(pick, -jnp.inf, cand)
    w = scores * chosen
    cw_ref[...] = w / jnp.sum(w, axis=0, keepdims=True) * ROUTED_SCALE


def _router(x, rw_t, rb):
    t, d = x.shape
    tm = _tile(t, 512)
    return pl.pallas_call(
        _router_kernel,
        grid=(t // tm,),
        in_specs=[pl.BlockSpec((tm, d), lambda i: (i, 0)),
                  pl.BlockSpec((N_EXPERTS, d), lambda i: (0, 0)),
                  pl.BlockSpec((N_EXPERTS, 1), lambda i: (0, 0))],
        out_specs=pl.BlockSpec((N_EXPERTS, tm), lambda i: (0, i)),
        out_shape=jax.ShapeDtypeStruct((N_EXPERTS, t), F32),
        compiler_params=_cparams("parallel"),
        name="router",
    )(x, rw_t, rb)


def _moe_kernel(x_ref, cw_ref, wg_ref, wu_ref, wd_ref, y_ref, *, ne):
    c = pl.program_id(1)
    x = x_ref[...]
    edim = wg_ref.shape[2]
    cw = cw_ref[...]
    hs = []
    for e in range(ne):
        h = jax.nn.silu(_dot(x, wg_ref[e])) * _dot(x, wu_ref[e]) * cw[:, e:e + 1]
        hs.append(h.astype(BF16))
    y = _dot(jnp.concatenate(hs, axis=1), wd_ref[...].reshape(ne * edim, wd_ref.shape[2]))

    @pl.when(c == 0)
    def _():
        y_ref[...] = y

    @pl.when(c != 0)
    def _():
        y_ref[...] += y


def _moe_dense(xb, cw_t, wg, wu, wd, layer):
    t, d = xb.shape
    edim = wg.shape[3]
    tm = _tile(t, 512)
    ne = 8
    cw = cw_t.reshape(N_EXPERTS // ne, ne, t).transpose(0, 2, 1)
    return pl.pallas_call(
        functools.partial(_moe_kernel, ne=ne),
        grid=(t // tm, N_EXPERTS // ne),
        in_specs=[pl.BlockSpec((tm, d), lambda i, c: (i, 0)),
                  pl.BlockSpec((None, tm, ne), lambda i, c: (c, i, 0)),
                  pl.BlockSpec((None, ne, d, edim), lambda i, c: (layer, c, 0, 0)),
                  pl.BlockSpec((None, ne, d, edim), lambda i, c: (layer, c, 0, 0)),
                  pl.BlockSpec((None, ne, edim, d), lambda i, c: (layer, c, 0, 0))],
        out_specs=pl.BlockSpec((tm, d), lambda i, c: (i, 0)),
        out_shape=jax.ShapeDtypeStruct((t, d), F32),
        compiler_params=_cparams("parallel", "arbitrary"),
        name="moe_experts",
    )(xb, cw, wg, wu, wd)


def _tail_kernel(x_ref, xb_ref, y_ref, p_ref, sg_ref, su_ref, sd_ref, pg_ref, pp_ref,
                 lng_ref, lnb_ref, xo_ref, xbo_ref, *, alpha):
    xb = xb_ref[...]
    h = jax.nn.silu(_dot(xb, sg_ref[...])) * _dot(xb, su_ref[...])
    shared = _dot(h.astype(BF16), sd_ref[...])
    ple = jax.nn.sigmoid(_dot(xb, pg_ref[...])) * _dot(p_ref[...].astype(BF16), pp_ref[...])
    z = alpha * x_ref[...] + y_ref[...] + shared + ple
    y = _layer_norm(z, lng_ref[...], lnb_ref[...])
    xo_ref[...] = y
    xbo_ref[...] = y.astype(BF16)


def _tail(x, xb, y, p, sg, su, sd, pg, pp, lng, lnb, alpha):
    t, d = x.shape
    tm = _tile(t, 256)
    row = lambda w: pl.BlockSpec((tm, w), lambda i: (i, 0))
    full = lambda a: pl.BlockSpec(a.shape, lambda i: (0,) * a.ndim)
    return pl.pallas_call(
        functools.partial(_tail_kernel, alpha=alpha),
        grid=(t // tm,),
        in_specs=[row(d), row(d), row(d), row(p.shape[1]), full(sg), full(su), full(sd),
                  full(pg), full(pp), full(lng), full(lnb)],
        out_specs=[row(d), row(d)],
        out_shape=[jax.ShapeDtypeStruct((t, d), F32), jax.ShapeDtypeStruct((t, d), BF16)],
        compiler_params=_cparams("parallel"),
        name="ffn_tail_norm",
    )(x, xb, y, p, sg, su, sd, pg, pp, lng, lnb)


def _slot_weights(w):
    d = w.shape[0]
    o = IN_OFFS
    qs = HEAD_DIM ** -0.5

    def heads(off, n, width=HEAD_DIM, slot=LANE, scale=1.0):
        blk = w[:, off:off + n * width].reshape(d, n, width) * scale
        return jnp.pad(blk, ((0, 0), (0, 0), (0, slot - width))).reshape(d, n * slot)

    gates = w[:, o[7]:o[8]].reshape(d, NSA_HEADS, 3).transpose(0, 2, 1).reshape(d, 3 * NSA_HEADS)
    parts = [heads(o[0], NSA_HEADS, scale=qs), heads(o[8], SWA_HEADS, scale=qs),
             heads(o[11], 2 * DIFF_HEADS, scale=qs), heads(o[12], 2 * DIFF_HEADS),
             heads(o[13], DIFF_HEADS, 2 * HEAD_DIM, 2 * LANE),
             heads(o[3], NSA_KV), heads(o[4], NSA_KV), heads(o[5], NSA_KV), heads(o[6], NSA_KV),
             heads(o[9], SWA_KV), heads(o[10], SWA_KV),
             w[:, o[1]:o[2]], w[:, o[2]:o[3]],
             jnp.pad(gates, ((0, 0), (0, LANE - 3 * NSA_HEADS)))]
    out = jnp.concatenate(parts, axis=1)
    assert out.shape[1] == _LAYOUT.width
    return out.astype(BF16)


def _compressed_slots(c, pos_cols):
    b, nb, hkv, dh = c.shape
    c = c.transpose(0, 2, 1, 3)
    extra = jnp.zeros((b, hkv, nb, LANE - dh), F32)
    if pos_cols:
        cpos = np.arange(nb) * NSA_BLOCK + NSA_BLOCK - 1
        cols = np.zeros((nb, LANE - dh), np.float32)
        cols[:, 0] = cpos // LANE
        cols[:, 1] = cpos % LANE
        extra = extra + jnp.asarray(cols)
    c = jnp.concatenate([c, extra], axis=-1)
    c = jnp.pad(c, ((0, 0), (0, 0), (0, NBLK_PAD - nb), (0, 0)))
    return c.astype(BF16)


def kernel(x, p, w_in, nsa_pe_k, nsa_w1_k, nsa_w2_k, nsa_pe_v, nsa_w1_v, nsa_w2_v, swa_sinks,
           diff_lq1, diff_lk1, diff_lq2, diff_lk2, diff_norm_g, w_branch_a, w_branch_b, w_branch_c,
           w_out, ln1_g, ln1_b, router_w, router_bias, exp_w_gate, exp_w_up, exp_w_down,
           sh_w_gate, sh_w_up, sh_w_down, ple_w_proj, ple_w_gate, ln2_g, ln2_b):
    bsz, seq, d = x.shape
    depth = w_in.shape[0]
    t = bsz * seq
    alpha = (2 * depth) ** 0.25
    lay = _LAYOUT
    nb = seq // NSA_BLOCK
    assert nb <= NBLK_PAD and seq % LANE == 0
    n_sel = min(NSA_TOPN, nb)

    aux = jnp.asarray(lay.aux)
    blk_mask = jnp.asarray(
        (np.arange(seq)[:, None] // NSA_BLOCK == np.arange(NBLK_PAD)[None, :]) * MASK_NEG, BF16)
    ex = np.zeros((3, LANE, NSA_HEADS * HEAD_DIM), np.float32)
    for c in range(3):
        for h in range(NSA_HEADS):
            ex[c, c * NSA_HEADS + h, h * HEAD_DIM:(h + 1) * HEAD_DIM] = 1.0
    expand = jnp.asarray(ex, BF16)
    slopes8 = jnp.asarray(_slopes(NSA_HEADS), F32)

    wg_all = exp_w_gate.astype(BF16)
    wu_all = exp_w_up.astype(BF16)
    wd_all = exp_w_down.astype(BF16)
    xf = x.reshape(t, d)
    xb = xf.astype(BF16)
    for i in range(depth):
        w_slots = _slot_weights(w_in[i])
        proj2d = _project(xb, w_slots, aux, seq, BF16, 1408)
        proj = proj2d.reshape(bsz, seq, lay.width)

        def blocks(off):
            c = proj[:, :, off:off + LANE].reshape(bsz, nb, NSA_BLOCK, NSA_KV, HEAD_DIM)
            return c.transpose(0, 1, 3, 2, 4).reshape(bsz * nb * NSA_KV, NSA_BLOCK * HEAD_DIM)
        cmp = _compress(
            jnp.stack([blocks(lay.kc), blocks(lay.vc)]),
            jnp.stack([nsa_pe_k[i].reshape(1, -1), nsa_pe_v[i].reshape(1, -1)]),
            jnp.stack([nsa_w1_k[i], nsa_w1_v[i]]).astype(BF16),
            jnp.stack([nsa_w2_k[i], nsa_w2_v[i]]).astype(BF16))
        cmp = cmp.reshape(2, bsz, nb, NSA_KV, HEAD_DIM)
        kca = _compressed_slots(cmp[0], True)
        vca = _compressed_slots(cmp[1], False)

        o_cmp, selc, touch = _nsa_compressed(proj, kca, vca, seq, n_sel)
        ks = proj[:, :, lay.ks:lay.ks + NSA_KV * LANE].reshape(bsz, seq, NSA_KV, LANE).transpose(0, 2, 1, 3)
        ksel = jnp.concatenate(
            [ks, jnp.broadcast_to(blk_mask[None, None], (bsz, NSA_KV, seq, NBLK_PAD))], axis=-1)
        o_sel = _nsa_selected(proj, selc, touch, ksel, seq)
        hp_a = jnp.stack([slopes8, jnp.zeros_like(slopes8)])
        o_win = _banded(proj, hp_a, seq, NSA_WINDOW, lay.qa, lay.kw, lay.vw, False, F32, "nsa_window")
        hp_b = jnp.stack([slopes8, swa_sinks[i].astype(F32)])
        o_b = _banded(proj, hp_b, seq, SWA_WINDOW, lay.qb, lay.kb, lay.vb, True, BF16, "swa_sinks")

        lambda_init = 0.8 - 0.6 * math.exp(-0.3 * i)
        lam_vecs = jnp.stack([diff_lq1[i], diff_lk1[i], diff_lq2[i], diff_lk2[i]]).astype(F32)
        o_d = _diff_attention(proj, lam_vecs, diff_norm_g[i].reshape(1, -1), seq, lambda_init)

        ha = NSA_HEADS * HEAD_DIM
        xf, xb = _merge(
            xf, xb, o_cmp.reshape(t, ha), o_sel.reshape(t, ha), o_win.reshape(t, ha), proj2d,
            o_b.reshape(t, ha), o_d.reshape(t, ha), w_in[i][:, GM_OFF:].astype(BF16), expand,
            w_branch_a[i].astype(BF16), w_branch_b[i].astype(BF16), w_branch_c[i].astype(BF16),
            w_out[i].astype(BF16), ln1_g[i].reshape(1, d), ln1_b[i].reshape(1, d), alpha)

        cw_t = _router(xf, router_w[i].T, router_bias[i].reshape(-1, 1))
        y = _moe_dense(xb, cw_t, wg_all, wu_all, wd_all, i)
        xf, xb = _tail(xf, xb, y, p[i].reshape(t, -1),
                       sh_w_gate[i].astype(BF16), sh_w_up[i].astype(BF16), sh_w_down[i].astype(BF16),
                       ple_w_gate[i].astype(BF16), ple_w_proj[i].astype(BF16),
                       ln2_g[i].reshape(1, d), ln2_b[i].reshape(1, d), alpha)
    return xf.reshape(bsz, seq, d)
```

```python
import functools
import math

import numpy as np
import jax
import jax.numpy as jnp
from jax import lax
from jax.experimental import pallas as pl
from jax.experimental.pallas import tpu as pltpu

F32 = jnp.float32
BF16 = jnp.bfloat16

HEAD_DIM = 64
NSA_HEADS = 8
NSA_KV = 2
NSA_BLOCK = 64
NSA_TOPN = 16
NSA_LOCAL = 2
NSA_WINDOW = 512
SWA_HEADS = 8
SWA_KV = 2
SWA_WINDOW = 128
DIFF_HEADS = 4
N_EXPERTS = 64
TOP_K = 8
N_GROUPS = 8
TOPK_GROUPS = 4
ROUTED_SCALE = 2.5
LN_EPS = 1e-5
NEG = -1e30
FORCE = 1e4
MASK_NEG = -(2.0 ** 100)

LANE = 128
SUBLANE = 8
LANE_SHIFT = LANE.bit_length() - 1
BLOCK_SHIFT = NSA_BLOCK.bit_length() - 1
MXU_WIDTH = 256
PROJ_TN = 7 * MXU_WIDTH
GROUP = NSA_HEADS // NSA_KV
NBLK_PAD = 128
VMEM_LIMIT = 56 * 1024 * 1024

IN_SIZES = (NSA_HEADS * HEAD_DIM,) + (NSA_KV * HEAD_DIM,) * 6 + (NSA_HEADS * 3,) + \
    (SWA_HEADS * HEAD_DIM, SWA_KV * HEAD_DIM, SWA_KV * HEAD_DIM) + \
    (DIFF_HEADS * 2 * HEAD_DIM,) * 3
IN_OFFS = np.concatenate([[0], np.cumsum(IN_SIZES)]).tolist()
GM_OFF = IN_OFFS[-1]


def _slopes(n):
    return [2.0 ** (-8.0 * (h + 1) / n) for h in range(n)]


class _Layout:
    def __init__(self):
        src, scale, bias, pa, pb = [], [], [], [], []

        def slot(cols, sc=1.0, consts=(), pos=False, width=LANE):
            s = [-1] * width
            c = [0.0] * width
            a = [0.0] * width
            b = [0.0] * width
            s[:len(cols)] = cols
            for off, val in consts:
                c[off] = val
            if pos:
                a[HEAD_DIM] = 1.0
                b[HEAD_DIM + 1] = 1.0
            start = len(src)
            src.extend(s)
            scale.extend([sc] * width)
            bias.extend(c)
            pa.extend(a)
            pb.extend(b)
            return start

        def rng(base, n):
            return list(range(base, base + n))

        qs = HEAD_DIM ** -0.5
        o = IN_OFFS
        sl8 = _slopes(NSA_HEADS)
        sl4 = _slopes(DIFF_HEADS)
        self.qa = len(src)
        for h in range(NSA_HEADS):
            slot(rng(o[0] + h * HEAD_DIM, HEAD_DIM), qs,
                 [(HEAD_DIM, sl8[h] * LANE), (HEAD_DIM + 1, sl8[h])])
        self.qb = len(src)
        for h in range(SWA_HEADS):
            slot(rng(o[8] + h * HEAD_DIM, HEAD_DIM), qs,
                 [(HEAD_DIM, sl8[h] * LANE), (HEAD_DIM + 1, sl8[h])])
        self.qd = len(src)
        for h in range(DIFF_HEADS):
            for r in range(2):
                slot(rng(o[11] + (h * 2 + r) * HEAD_DIM, HEAD_DIM), qs,
                     [(HEAD_DIM, sl4[h] * LANE), (HEAD_DIM + 1, sl4[h])])
        self.kd = len(src)
        for h in range(DIFF_HEADS):
            for r in range(2):
                slot(rng(o[12] + (h * 2 + r) * HEAD_DIM, HEAD_DIM), pos=True)
        self.vd = len(src)
        for h in range(DIFF_HEADS):
            slot(rng(o[13] + h * 2 * HEAD_DIM, 2 * HEAD_DIM),
                 consts=[(2 * HEAD_DIM, 1.0)], width=2 * LANE)

        def kv_slots(k_src, v_src):
            k_off = len(src)
            for k in range(NSA_KV):
                slot(rng(k_src + k * HEAD_DIM, HEAD_DIM), pos=True)
            v_off = len(src)
            for k in range(NSA_KV):
                slot(rng(v_src + k * HEAD_DIM, HEAD_DIM), consts=[(HEAD_DIM, 1.0)])
            return k_off, v_off

        self.ks, self.vs = kv_slots(o[3], o[4])
        self.kw, self.vw = kv_slots(o[5], o[6])
        self.kb, self.vb = kv_slots(o[9], o[10])
        self.kc = slot(rng(o[1], NSA_KV * HEAD_DIM))
        self.vc = slot(rng(o[2], NSA_KV * HEAD_DIM))
        self.ga = slot([o[7] + h * 3 + c for c in range(3) for h in range(NSA_HEADS)])
        while len(src) % PROJ_TN:
            slot([])
        self.width = len(src)
        aux = np.zeros((SUBLANE, self.width), np.float32)
        aux[0] = bias
        aux[1] = pa
        aux[2] = pb
        self.aux = aux


_LAYOUT = _Layout()


def _cparams(*sem):
    return pltpu.CompilerParams(dimension_semantics=sem, vmem_limit_bytes=VMEM_LIMIT)


def _tile(n, pref):
    t = min(n, pref)
    assert n % t == 0, (n, pref)
    return t


def _iota(shape, dim):
    return lax.broadcasted_iota(jnp.int32, shape, dim)


def _dot(a, b):
    return jnp.dot(a, b, preferred_element_type=F32)


def _dot_nt(a, b):
    return lax.dot_general(a, b, (((1,), (1,)), ((), ())), preferred_element_type=F32)


def _proj_kernel(x_ref, w_ref, aux_ref, o_ref, *, seq, tm):
    acc = _dot(x_ref[...], w_ref[...])
    pos = (pl.program_id(0) * tm) % seq + _iota((tm, 1), 0)
    a = (pos >> LANE_SHIFT).astype(F32)
    b = (pos & (LANE - 1)).astype(F32)
    aux = aux_ref[...]
    o_ref[...] = (acc + aux[0:1] + a * aux[1:2] + b * aux[2:3]).astype(o_ref.dtype)


def _project(xb, w, aux, seq, out_dtype, tn_pref):
    t, d = xb.shape
    n = w.shape[1]
    tm = _tile(t, 1024)
    tn = _tile(n, tn_pref)
    return pl.pallas_call(
        functools.partial(_proj_kernel, seq=seq, tm=tm),
        grid=(t // tm, n // tn),
        in_specs=[pl.BlockSpec((tm, d), lambda i, j: (i, 0)),
                  pl.BlockSpec((d, tn), lambda i, j: (0, j)),
                  pl.BlockSpec((SUBLANE, tn), lambda i, j: (0, j))],
        out_specs=pl.BlockSpec((tm, tn), lambda i, j: (i, j)),
        out_shape=jax.ShapeDtypeStruct((t, n), out_dtype),
        compiler_params=_cparams("parallel", "parallel"),
        name="proj",
    )(xb, w, aux)


def _compress_kernel(x_ref, pe_ref, w1_ref, w2_ref, o_ref):
    xb = (x_ref[0].astype(F32) + pe_ref[0]).astype(BF16)
    h = jax.nn.gelu(_dot(xb, w1_ref[0]))
    o_ref[0] = _dot(h.astype(BF16), w2_ref[0])


def _compress(xs, pes, w1s, w2s):
    _, r, kdim = xs.shape
    hid = w1s.shape[2]
    dh = w2s.shape[2]
    return pl.pallas_call(
        _compress_kernel,
        grid=(2,),
        in_specs=[pl.BlockSpec((1, r, kdim), lambda i: (i, 0, 0)),
                  pl.BlockSpec((1, 1, kdim), lambda i: (i, 0, 0)),
                  pl.BlockSpec((1, kdim, hid), lambda i: (i, 0, 0)),
                  pl.BlockSpec((1, hid, dh), lambda i: (i, 0, 0))],
        out_specs=pl.BlockSpec((1, r, dh), lambda i: (i, 0, 0)),
        out_shape=jax.ShapeDtypeStruct((2, r, dh), F32),
        compiler_params=_cparams("parallel"),
        name="nsa_compress",
    )(xs, pes, w1s, w2s)


def _stack_heads(q, n):
    return jnp.concatenate([q[:, g * LANE:(g + 1) * LANE] for g in range(n)], axis=0)


def _unstack_heads(o, n, tq, width):
    return jnp.concatenate([o[g * tq:(g + 1) * tq, :width] for g in range(n)], axis=1)


def _cmp_kernel(q_ref, kc_ref, vc_ref, oc_ref, sel_ref, touch_ref, *, tq, n_sel):
    i = pl.program_id(2)
    qs = _stack_heads(q_ref[0], GROUP)
    s = _dot_nt(qs, kc_ref[0, 0])
    t1 = i * tq + _iota((tq, 1), 0)
    t = jnp.concatenate([t1] * GROUP, axis=0)
    n = _iota((1, NBLK_PAD), 1)
    valid = (n * NSA_BLOCK + (NSA_BLOCK - 1)) <= t
    s = jnp.where(valid, s, NEG)
    mx = jnp.max(s, axis=-1, keepdims=True)
    e = jnp.where(valid, jnp.exp(s - mx), 0.0)
    p = e / jnp.maximum(jnp.sum(e, axis=-1, keepdims=True), 1e-30)
    o = _dot(p.astype(BF16), vc_ref[0, 0])
    oc_ref[0] = _unstack_heads(o, GROUP, tq, HEAD_DIM)

    imp = p[0:tq]
    for g in range(1, GROUP):
        imp = imp + p[g * tq:(g + 1) * tq]
    nb = _iota((NBLK_PAD, 1), 0)
    rel = ((i * tq + _iota((1, tq), 1)) >> BLOCK_SHIFT) - nb
    forced = (nb == 0) | ((rel >= 0) & (rel < NSA_LOCAL))
    val = jnp.where(forced, FORCE, jnp.where(rel >= 0, imp.T, -1.0))
    sel = jnp.zeros((NBLK_PAD, tq), F32)
    nf = nb.astype(F32)
    for _ in range(n_sel):
        mx = jnp.max(val, axis=0, keepdims=True)
        idx = jnp.min(jnp.where(val == mx, nf, float(NBLK_PAD)), axis=0, keepdims=True)
        pick = nf == idx
        sel = jnp.where(pick & (mx >= 0.0), 1.0, sel)
        val = jnp.where(pick, -jnp.inf, val)
    sel = sel.T
    sel_ref[0, 0] = (1.0 - sel).astype(BF16)
    touch_ref[0, 0, 0] = jnp.broadcast_to(jnp.max(sel, axis=0, keepdims=True), (SUBLANE, NBLK_PAD))


def _nsa_compressed(proj, kca, vca, seq, n_sel):
    b = proj.shape[0]
    tq = _tile(seq, 256)
    lay = _LAYOUT
    gw = GROUP * LANE
    return pl.pallas_call(
        functools.partial(_cmp_kernel, tq=tq, n_sel=n_sel),
        grid=(b, NSA_KV, seq // tq),
        in_specs=[pl.BlockSpec((1, tq, gw), lambda b_, k, i: (b_, i, lay.qa // gw + k)),
                  pl.BlockSpec((1, 1, NBLK_PAD, LANE), lambda b_, k, i: (b_, k, 0, 0)),
                  pl.BlockSpec((1, 1, NBLK_PAD, LANE), lambda b_, k, i: (b_, k, 0, 0))],
        out_specs=[pl.BlockSpec((1, tq, GROUP * HEAD_DIM), lambda b_, k, i: (b_, i, k)),
                   pl.BlockSpec((1, 1, tq, NBLK_PAD), lambda b_, k, i: (b_, k, i, 0)),
                   pl.BlockSpec((1, 1, 1, SUBLANE, NBLK_PAD), lambda b_, k, i: (b_, k, i, 0, 0))],
        out_shape=[jax.ShapeDtypeStruct((b, seq, NSA_HEADS * HEAD_DIM), F32),
                   jax.ShapeDtypeStruct((b, NSA_KV, seq, NBLK_PAD), BF16),
                   jax.ShapeDtypeStruct((b, NSA_KV, seq // tq, SUBLANE, NBLK_PAD), F32)],
        compiler_params=_cparams("parallel", "parallel", "parallel"),
        name="nsa_compressed",
    )(proj, kca, vca)


def _flash(chunk_fns, v_ref, i, tq, tk, m_ref, acc_ref, s_refs, n_active=None, tile_at=None):
    m_ref[...] = jnp.full(m_ref.shape, NEG, F32)
    acc_ref[...] = jnp.zeros(acc_ref.shape, F32)
    acc_reps = acc_ref.shape[1] // LANE
    n_full = (i * tq) // tk
    if n_active is None:
        n_active = n_full
        tile_at = lambda n: n

    def nth_tile(n):
        return jnp.where(n < n_active, tile_at(n), n_full)

    def issue(n, s_ref):
        start = pl.multiple_of(nth_tile(n) * tk, tk)
        for c, fn in enumerate(chunk_fns):
            s_ref[c * tq:(c + 1) * tq, :] = fn(start)

    def accumulate(j, s_ref, visible):
        start = pl.multiple_of(j * tk, tk)
        vt = v_ref[pl.ds(start, tk), :]
        for c in range(len(chunk_fns)):
            rows = slice(c * tq, (c + 1) * tq)
            s = s_ref[rows, :]
            if visible is not None:
                s = jnp.where(visible, s, NEG)
            m_old = m_ref[rows, :]
            m_new = jnp.maximum(m_old, jnp.max(s, axis=-1, keepdims=True))
            p = jnp.exp((s - jnp.concatenate([m_new] * (tk // LANE), axis=1)).astype(BF16))
            alpha = jnp.concatenate([jnp.exp(m_old - m_new)] * acc_reps, axis=1)
            acc_ref[rows, :] = alpha * acc_ref[rows, :] + _dot(p, vt)
            m_ref[rows, :] = m_new

    s0, s1 = s_refs
    issue(0, s0)

    def body(pair, carry):
        n = 2 * pair
        issue(n + 1, s1)
        accumulate(nth_tile(n), s0, None)
        issue(n + 2, s0)
        accumulate(nth_tile(n + 1), s1, None)
        return carry

    pairs = n_active // 2
    lax.fori_loop(0, pairs, body, 0)
    visible = (_iota((tq, tk), 1) - _iota((tq, tk), 0)) <= i * tq - n_full * tk
    leftover = n_active - 2 * pairs

    @pl.when(leftover == 1)
    def _():
        issue(n_active, s1)
        accumulate(nth_tile(n_active - 1), s0, None)
        accumulate(n_full, s1, visible)

    @pl.when(leftover == 0)
    def _():
        accumulate(n_full, s0, visible)


def _sel_kernel(order_ref, count_ref, q_ref, sel_ref, k_ref, v_ref, o_ref, qa_ref, m_ref, acc_ref,
                s0_ref, s1_ref, *, tq, tk, n_tiles):
    i = pl.program_id(2)
    tile_id = (pl.program_id(0) * pl.num_programs(1) + pl.program_id(1)) * pl.num_programs(2) + i
    q = q_ref[0]
    sc = sel_ref[0, 0]
    for g in range(GROUP):
        qa_ref[g * tq:(g + 1) * tq, 0:LANE] = q[:, g * LANE:(g + 1) * LANE]
        qa_ref[g * tq:(g + 1) * tq, LANE:2 * LANE] = sc
    kv = k_ref.at[0, 0]

    def head_scores(g):
        return lambda start: _dot_nt(qa_ref[g * tq:(g + 1) * tq, :], kv[pl.ds(start, tk), :])

    _flash([head_scores(g) for g in range(GROUP)], v_ref.at[0], i, tq, tk, m_ref, acc_ref, (s0_ref, s1_ref),
           n_active=count_ref[tile_id], tile_at=lambda n: order_ref[tile_id * n_tiles + n])
    acc = acc_ref[...]
    o = acc / acc[:, HEAD_DIM:HEAD_DIM + 1]
    o_ref[0] = _unstack_heads(o, GROUP, tq, HEAD_DIM)


def _nsa_selected(proj, selc, touch, ksel, seq):
    b = proj.shape[0]
    tq = _tile(seq, 256)
    tk = _tile(seq, 512)
    n_tiles = seq // tk
    lay = _LAYOUT
    gw = GROUP * LANE
    per_tile = touch[:, :, :, 0, :seq // NSA_BLOCK].reshape(b, NSA_KV, seq // tq, n_tiles, tk // NSA_BLOCK)
    n_full = (np.arange(seq // tq) * tq) // tk
    active = (jnp.max(per_tile, axis=-1) > 0.0) & jnp.asarray(np.arange(n_tiles)[None, :] < n_full[:, None])
    order = jnp.argsort(jnp.where(active, 0, 1), axis=-1, stable=True).astype(jnp.int32).reshape(-1)
    count = jnp.sum(active, axis=-1).astype(jnp.int32).reshape(-1)
    grid_spec = pltpu.PrefetchScalarGridSpec(
        num_scalar_prefetch=2,
        grid=(b, NSA_KV, seq // tq),
        in_specs=[pl.BlockSpec((1, tq, gw), lambda b_, k, i, o, c: (b_, i, lay.qa // gw + k)),
                  pl.BlockSpec((1, 1, tq, NBLK_PAD), lambda b_, k, i, o, c: (b_, k, i, 0)),
                  pl.BlockSpec((1, 1, seq, 2 * LANE), lambda b_, k, i, o, c: (b_, k, 0, 0)),
                  pl.BlockSpec((1, seq, LANE), lambda b_, k, i, o, c: (b_, 0, lay.vs // LANE + k))],
        out_specs=pl.BlockSpec((1, tq, GROUP * HEAD_DIM), lambda b_, k, i, o, c: (b_, i, k)),
        scratch_shapes=[pltpu.VMEM((GROUP * tq, 2 * LANE), BF16),
                        pltpu.VMEM((GROUP * tq, LANE), F32),
                        pltpu.VMEM((GROUP * tq, LANE), F32),
                        pltpu.VMEM((GROUP * tq, tk), F32),
                        pltpu.VMEM((GROUP * tq, tk), F32)])
    return pl.pallas_call(
        functools.partial(_sel_kernel, tq=tq, tk=tk, n_tiles=n_tiles),
        grid_spec=grid_spec,
        out_shape=jax.ShapeDtypeStruct((b, seq, NSA_HEADS * HEAD_DIM), F32),
        compiler_params=_cparams("parallel", "parallel", "arbitrary"),
        name="nsa_selected",
    )(order, count, proj, selc, ksel, proj)


def _diff_kernel(lam_ref, q_ref, k_ref, v_ref, g_ref, o_ref, m_ref, acc_ref, s0_ref, s1_ref,
                 *, tq, tk, lambda_init):
    i = pl.program_id(2)
    kv = k_ref.at[0]

    def map_scores(r):
        cols = slice(r * LANE, (r + 1) * LANE)
        return lambda start: _dot_nt(q_ref[0, :, cols], kv[pl.ds(start, tk), cols])

    _flash([map_scores(0), map_scores(1)], v_ref.at[0], i, tq, tk, m_ref, acc_ref, (s0_ref, s1_ref))
    acc = acc_ref[...]
    dv = 2 * HEAD_DIM
    a = acc[:, :dv] / acc[:, dv:dv + 1]
    lv = lam_ref[...]
    lam = (jnp.exp(jnp.sum(lv[0:1] * lv[1:2], axis=-1, keepdims=True))
           - jnp.exp(jnp.sum(lv[2:3] * lv[3:4], axis=-1, keepdims=True)) + lambda_init)
    o = a[0:tq] - lam * a[tq:2 * tq]
    o = o * lax.rsqrt(jnp.mean(o * o, axis=-1, keepdims=True) + LN_EPS) * g_ref[...]
    o_ref[0] = (o * (1.0 - lambda_init)).astype(o_ref.dtype)


def _diff_attention(proj, lam_vecs, norm_g, seq, lambda_init):
    b = proj.shape[0]
    tq = _tile(seq, 1024)
    tk = _tile(seq, 1024)
    lay = _LAYOUT
    w2 = 2 * LANE
    return pl.pallas_call(
        functools.partial(_diff_kernel, tq=tq, tk=tk, lambda_init=lambda_init),
        grid=(b, DIFF_HEADS, seq // tq),
        in_specs=[pl.BlockSpec((4, HEAD_DIM), lambda b_, h, i: (0, 0)),
                  pl.BlockSpec((1, tq, w2), lambda b_, h, i: (b_, i, lay.qd // w2 + h)),
                  pl.BlockSpec((1, seq, w2), lambda b_, h, i: (b_, 0, lay.kd // w2 + h)),
                  pl.BlockSpec((1, seq, w2), lambda b_, h, i: (b_, 0, lay.vd // w2 + h)),
                  pl.BlockSpec((1, 2 * HEAD_DIM), lambda b_, h, i: (0, 0))],
        out_specs=pl.BlockSpec((1, tq, 2 * HEAD_DIM), lambda b_, h, i: (b_, i, h)),
        out_shape=jax.ShapeDtypeStruct((b, seq, DIFF_HEADS * 2 * HEAD_DIM), BF16),
        scratch_shapes=[pltpu.VMEM((2 * tq, LANE), F32),
                        pltpu.VMEM((2 * tq, w2), F32),
                        pltpu.VMEM((2 * tq, tk), F32),
                        pltpu.VMEM((2 * tq, tk), F32)],
        compiler_params=_cparams("parallel", "parallel", "arbitrary"),
        name="diff_attention",
    )(lam_vecs, proj, proj, proj, norm_g)


def _band_kernel(hp_ref, q_ref, k_ref, v_ref, o_ref, *, tq, window, band, use_sinks):
    kvh = pl.program_id(1)
    i = pl.program_id(2)
    start = pl.multiple_of(jnp.maximum(i * tq - window, 0), LANE)
    kb = k_ref[0, pl.ds(start, band), :]
    vb = v_ref[0, pl.ds(start, band), :]
    t1 = i * tq + _iota((tq, 1), 0)
    dist = t1 - (start + _iota((1, band), 1))
    visible = (dist >= 0) & (dist < window)
    outs = []
    scores = [_dot_nt(q_ref[0, :, g * LANE:(g + 1) * LANE], kb) for g in range(GROUP)]
    for g in range(GROUP):
        s = jnp.where(visible, scores[g], NEG)
        mx = jnp.max(s, axis=-1, keepdims=True)
        if use_sinks:
            h = kvh * GROUP + g
            sink = hp_ref[1, h] + hp_ref[0, h] * t1.astype(F32)
            mx = jnp.maximum(mx, sink)
        acc = _dot(jnp.exp((s - mx).astype(BF16)), vb)
        den = acc[:, HEAD_DIM:HEAD_DIM + 1]
        if use_sinks:
            den = den + jnp.exp(sink - mx)
        outs.append((acc / den)[:, :HEAD_DIM])
    o_ref[0] = jnp.concatenate(outs, axis=1).astype(o_ref.dtype)


def _banded(proj, head_params, seq, window, q_off, k_off, v_off, use_sinks, out_dtype, name):
    b = proj.shape[0]
    tq = _tile(seq, 256)
    band = min(window + tq, seq)
    gw = GROUP * LANE
    return pl.pallas_call(
        functools.partial(_band_kernel, tq=tq, window=window, band=band, use_sinks=use_sinks),
        grid=(b, NSA_KV, seq // tq),
        in_specs=[pl.BlockSpec(memory_space=pltpu.SMEM),
                  pl.BlockSpec((1, tq, gw), lambda b_, k, i: (b_, i, q_off // gw + k)),
                  pl.BlockSpec((1, seq, LANE), lambda b_, k, i: (b_, 0, k_off // LANE + k)),
                  pl.BlockSpec((1, seq, LANE), lambda b_, k, i: (b_, 0, v_off // LANE + k))],
        out_specs=pl.BlockSpec((1, tq, GROUP * HEAD_DIM), lambda b_, k, i: (b_, i, k)),
        out_shape=jax.ShapeDtypeStruct((b, seq, NSA_HEADS * HEAD_DIM), out_dtype),
        compiler_params=_cparams("parallel", "parallel", "parallel"),
        name=name,
    )(head_params, proj, proj, proj)


def _layer_norm(z, g, b):
    mu = jnp.mean(z, axis=-1, keepdims=True)
    zc = z - mu
    var = jnp.mean(zc * zc, axis=-1, keepdims=True)
    return zc * lax.rsqrt(var + LN_EPS) * g + b


def _merge_kernel(x_ref, xin_ref, oc_ref, os_ref, ow_ref, ga_ref, ob_ref, od_ref, wgm_ref, ex_ref,
                  wa_ref, wb_ref, wc_ref, wo_ref, lng_ref, lnb_ref, xo_ref, xb_ref, *, alpha):
    d = x_ref.shape[1]
    ga = jax.nn.sigmoid(ga_ref[...].astype(F32)).astype(BF16)
    oa = (_dot(ga, ex_ref[0]) * oc_ref[...] + _dot(ga, ex_ref[1]) * os_ref[...]
          + _dot(ga, ex_ref[2]) * ow_ref[...])
    gm = jax.nn.sigmoid(_dot(xin_ref[...], wgm_ref[...]))
    merged = (gm[:, 0:d] * _dot(oa.astype(BF16), wa_ref[...])
              + gm[:, d:2 * d] * _dot(ob_ref[...], wb_ref[...])
              + gm[:, 2 * d:3 * d] * _dot(od_ref[...], wc_ref[...]))
    z = alpha * x_ref[...] + _dot(merged.astype(BF16), wo_ref[...])
    y = _layer_norm(z, lng_ref[...], lnb_ref[...])
    xo_ref[...] = y
    xb_ref[...] = y.astype(BF16)


def _merge(x, xb, oc, os_, ow, proj2d, ob, od, wgm, expand, wa, wb, wc, wo, lng, lnb, alpha):
    t, d = x.shape
    tm = _tile(t, 256)
    ha = NSA_HEADS * HEAD_DIM
    row = lambda w: pl.BlockSpec((tm, w), lambda i: (i, 0))
    full = lambda a: pl.BlockSpec(a.shape, lambda i: (0,) * a.ndim)
    return pl.pallas_call(
        functools.partial(_merge_kernel, alpha=alpha),
        grid=(t // tm,),
        in_specs=[row(d), row(d), row(ha), row(ha), row(ha),
                  pl.BlockSpec((tm, LANE), lambda i: (i, _LAYOUT.ga // LANE)),
                  row(ha), row(ha), full(wgm), full(expand),
                  full(wa), full(wb), full(wc), full(wo), full(lng), full(lnb)],
        out_specs=[row(d), row(d)],
        out_shape=[jax.ShapeDtypeStruct((t, d), F32), jax.ShapeDtypeStruct((t, d), BF16)],
        compiler_params=_cparams("parallel"),
        name="merge_norm",
    )(x, xb, oc, os_, ow, proj2d, ob, od, wgm, expand, wa, wb, wc, wo, lng, lnb)


def _first_max(v, idx, n):
    mx = jnp.max(v, axis=0, keepdims=True)
    first = jnp.min(jnp.where(v == mx, idx, n), axis=0, keepdims=True)
    return mx, idx == first


def _router_kernel(x_ref, rw_ref, rb_ref, cw_ref):
    gsz = N_EXPERTS // N_GROUPS
    logits = lax.dot_general(rw_ref[...], x_ref[...], (((1,), (1,)), ((), ())),
                             precision=lax.Precision.HIGHEST,
                             preferred_element_type=F32)
    tm = logits.shape[1]
    scores = jax.nn.sigmoid(logits)
    biased = scores + rb_ref[...]
    iw = _iota((gsz, tm), 0)
    gs = []
    for g in range(N_GROUPS):
        blk = biased[g * gsz:(g + 1) * gsz]
        m1, pick = _first_max(blk, iw, gsz)
        m2 = jnp.max(jnp.where(pick, -jnp.inf, blk), axis=0, keepdims=True)
        gs.append(m1 + m2)
    gs = jnp.concatenate(gs, axis=0)
    ig = _iota((N_GROUPS, tm), 0)
    gmask = jnp.zeros((N_GROUPS, tm), F32)
    for _ in range(TOPK_GROUPS):
        _, pick = _first_max(gs, ig, N_GROUPS)
        gmask = jnp.where(pick, 1.0, gmask)
        gs = jnp.where(pick, -jnp.inf, gs)
    emask = jnp.concatenate(
        [jnp.broadcast_to(gmask[g:g + 1], (gsz, tm)) for g in range(N_GROUPS)], axis=0)
    cand = jnp.where(emask > 0.5, biased, NEG)
    ie = _iota((N_EXPERTS, tm), 0)
    chosen = jnp.zeros((N_EXPERTS, tm), F32)
    for _ in range(TOP_K):
        _, pick = _first_max(cand, ie, N_EXPERTS)
        chosen = jnp.where(pick, 1.0, chosen)
        cand = jnp.where(pick, -jnp.inf, cand)
    w = scores * chosen
    cw_ref[...] = w / jnp.sum(w, axis=0, keepdims=True) * ROUTED_SCALE


def _router(x, rw_t, rb):
    t, d = x.shape
    tm = _tile(t, 512)
    return pl.pallas_call(
        _router_kernel,
        grid=(t // tm,),
        in_specs=[pl.BlockSpec((tm, d), lambda i: (i, 0)),
                  pl.BlockSpec((N_EXPERTS, d), lambda i: (0, 0)),
                  pl.BlockSpec((N_EXPERTS, 1), lambda i: (0, 0))],
        out_specs=pl.BlockSpec((N_EXPERTS, tm), lambda i: (0, i)),
        out_shape=jax.ShapeDtypeStruct((N_EXPERTS, t), F32),
        compiler_params=_cparams("parallel"),
        name="router",
    )(x, rw_t, rb)


def _moe_kernel(x_ref, cw_ref, wg_ref, wu_ref, wd_ref, y_ref, *, ne):
    c = pl.program_id(1)
    x = x_ref[...]
    edim = wg_ref.shape[2]
    cw = cw_ref[...]
    hs = []
    for e in range(ne):
        h = jax.nn.silu(_dot(x, wg_ref[e])) * _dot(x, wu_ref[e]) * cw[:, e:e + 1]
        hs.append(h.astype(BF16))
    y = _dot(jnp.concatenate(hs, axis=1), wd_ref[...].reshape(ne * edim, wd_ref.shape[2]))

    @pl.when(c == 0)
    def _():
        y_ref[...] = y

    @pl.when(c != 0)
    def _():
        y_ref[...] += y


def _moe_dense(xb, cw_t, wg, wu, wd, layer):
    t, d = xb.shape
    edim = wg.shape[3]
    tm = _tile(t, 512)
    ne = 8
    cw = cw_t.reshape(N_EXPERTS // ne, ne, t).transpose(0, 2, 1)
    return pl.pallas_call(
        functools.partial(_moe_kernel, ne=ne),
        grid=(t // tm, N_EXPERTS // ne),
        in_specs=[pl.BlockSpec((tm, d), lambda i, c: (i, 0)),
                  pl.BlockSpec((None, tm, ne), lambda i, c: (c, i, 0)),
                  pl.BlockSpec((None, ne, d, edim), lambda i, c: (layer, c, 0, 0)),
                  pl.BlockSpec((None, ne, d, edim), lambda i, c: (layer, c, 0, 0)),
                  pl.BlockSpec((None, ne, edim, d), lambda i, c: (layer, c, 0, 0))],
        out_specs=pl.BlockSpec((tm, d), lambda i, c: (i, 0)),
        out_shape=jax.ShapeDtypeStruct((t, d), F32),
        compiler_params=_cparams("parallel", "arbitrary"),
        name="moe_experts",
    )(xb, cw, wg, wu, wd)


def _tail_kernel(x_ref, xb_ref, y_ref, p_ref, sg_ref, su_ref, sd_ref, pg_ref, pp_ref,
                 lng_ref, lnb_ref, xo_ref, xbo_ref, *, alpha):
    xb = xb_ref[...]
    h = jax.nn.silu(_dot(xb, sg_ref[...])) * _dot(xb, su_ref[...])
    shared = _dot(h.astype(BF16), sd_ref[...])
    ple = jax.nn.sigmoid(_dot(xb, pg_ref[...])) * _dot(p_ref[...].astype(BF16), pp_ref[...])
    z = alpha * x_ref[...] + y_ref[...] + shared + ple
    y = _layer_norm(z, lng_ref[...], lnb_ref[...])
    xo_ref[...] = y
    xbo_ref[...] = y.astype(BF16)


def _tail(x, xb, y, p, sg, su, sd, pg, pp, lng, lnb, alpha):
    t, d = x.shape
    tm = _tile(t, 256)
    row = lambda w: pl.BlockSpec((tm, w), lambda i: (i, 0))
    full = lambda a: pl.BlockSpec(a.shape, lambda i: (0,) * a.ndim)
    return pl.pallas_call(
        functools.partial(_tail_kernel, alpha=alpha),
        grid=(t // tm,),
        in_specs=[row(d), row(d), row(d), row(p.shape[1]), full(sg), full(su), full(sd),
                  full(pg), full(pp), full(lng), full(lnb)],
        out_specs=[row(d), row(d)],
        out_shape=[jax.ShapeDtypeStruct((t, d), F32), jax.ShapeDtypeStruct((t, d), BF16)],
        compiler_params=_cparams("parallel"),
        name="ffn_tail_norm",
    )(x, xb, y, p, sg, su, sd, pg, pp, lng, lnb)


def _slot_weights(w):
    d = w.shape[0]
    o = IN_OFFS
    qs = HEAD_DIM ** -0.5

    def heads(off, n, width=HEAD_DIM, slot=LANE, scale=1.0):
        blk = w[:, off:off + n * width].reshape(d, n, width) * scale
        return jnp.pad(blk, ((0, 0), (0, 0), (0, slot - width))).reshape(d, n * slot)

    gates = w[:, o[7]:o[8]].reshape(d, NSA_HEADS, 3).transpose(0, 2, 1).reshape(d, 3 * NSA_HEADS)
    parts = [heads(o[0], NSA_HEADS, scale=qs), heads(o[8], SWA_HEADS, scale=qs),
             heads(o[11], 2 * DIFF_HEADS, scale=qs), heads(o[12], 2 * DIFF_HEADS),
             heads(o[13], DIFF_HEADS, 2 * HEAD_DIM, 2 * LANE),
             heads(o[3], NSA_KV), heads(o[4], NSA_KV), heads(o[5], NSA_KV), heads(o[6], NSA_KV),
             heads(o[9], SWA_KV), heads(o[10], SWA_KV),
             w[:, o[1]:o[2]], w[:, o[2]:o[3]],
             jnp.pad(gates, ((0, 0), (0, LANE - 3 * NSA_HEADS)))]
    used = sum(part.shape[1] for part in parts)
    parts.append(jnp.zeros((d, _LAYOUT.width - used), F32))
    out = jnp.concatenate(parts, axis=1)
    assert out.shape[1] == _LAYOUT.width
    return out.astype(BF16)


def _compressed_slots(c, pos_cols):
    b, nb, hkv, dh = c.shape
    c = c.transpose(0, 2, 1, 3)
    extra = jnp.zeros((b, hkv, nb, LANE - dh), F32)
    if pos_cols:
        cpos = np.arange(nb) * NSA_BLOCK + NSA_BLOCK - 1
        cols = np.zeros((nb, LANE - dh), np.float32)
        cols[:, 0] = cpos // LANE
        cols[:, 1] = cpos % LANE
        extra = extra + jnp.asarray(cols)
    c = jnp.concatenate([c, extra], axis=-1)
    c = jnp.pad(c, ((0, 0), (0, 0), (0, NBLK_PAD - nb), (0, 0)))
    return c.astype(BF16)


def kernel(x, p, w_in, nsa_pe_k, nsa_w1_k, nsa_w2_k, nsa_pe_v, nsa_w1_v, nsa_w2_v, swa_sinks,
           diff_lq1, diff_lk1, diff_lq2, diff_lk2, diff_norm_g, w_branch_a, w_branch_b, w_branch_c,
           w_out, ln1_g, ln1_b, router_w, router_bias, exp_w_gate, exp_w_up, exp_w_down,
           sh_w_gate, sh_w_up, sh_w_down, ple_w_proj, ple_w_gate, ln2_g, ln2_b):
    bsz, seq, d = x.shape
    depth = w_in.shape[0]
    t = bsz * seq
    alpha = (2 * depth) ** 0.25
    lay = _LAYOUT
    nb = seq // NSA_BLOCK
    assert nb <= NBLK_PAD and seq % LANE == 0
    n_sel = min(NSA_TOPN, nb)

    aux = jnp.asarray(lay.aux)
    blk_mask = jnp.asarray(
        (np.arange(seq)[:, None] // NSA_BLOCK == np.arange(NBLK_PAD)[None, :]) * MASK_NEG, BF16)
    ex = np.zeros((3, LANE, NSA_HEADS * HEAD_DIM), np.float32)
    for c in range(3):
        for h in range(NSA_HEADS):
            ex[c, c * NSA_HEADS + h, h * HEAD_DIM:(h + 1) * HEAD_DIM] = 1.0
    expand = jnp.asarray(ex, BF16)
    slopes8 = jnp.asarray(_slopes(NSA_HEADS), F32)

    wg_all = exp_w_gate.astype(BF16)
    wu_all = exp_w_up.astype(BF16)
    wd_all = exp_w_down.astype(BF16)
    xf = x.reshape(t, d)
    xb = xf.astype(BF16)
    for i in range(depth):
        w_slots = _slot_weights(w_in[i])
        proj2d = _project(xb, w_slots, aux, seq, BF16, PROJ_TN)
        proj = proj2d.reshape(bsz, seq, lay.width)

        def blocks(off):
            c = proj[:, :, off:off + LANE].reshape(bsz, nb, NSA_BLOCK, NSA_KV, HEAD_DIM)
            return c.transpose(0, 1, 3, 2, 4).reshape(bsz * nb * NSA_KV, NSA_BLOCK * HEAD_DIM)
        cmp = _compress(
            jnp.stack([blocks(lay.kc), blocks(lay.vc)]),
            jnp.stack([nsa_pe_k[i].reshape(1, -1), nsa_pe_v[i].reshape(1, -1)]),
            jnp.stack([nsa_w1_k[i], nsa_w1_v[i]]).astype(BF16),
            jnp.stack([nsa_w2_k[i], nsa_w2_v[i]]).astype(BF16))
        cmp = cmp.reshape(2, bsz, nb, NSA_KV, HEAD_DIM)
        kca = _compressed_slots(cmp[0], True)
        vca = _compressed_slots(cmp[1], False)

        o_cmp, selc, touch = _nsa_compressed(proj, kca, vca, seq, n_sel)
        ks = proj[:, :, lay.ks:lay.ks + NSA_KV * LANE].reshape(bsz, seq, NSA_KV, LANE).transpose(0, 2, 1, 3)
        ksel = jnp.concatenate(
            [ks, jnp.broadcast_to(blk_mask[None, None], (bsz, NSA_KV, seq, NBLK_PAD))], axis=-1)
        o_sel = _nsa_selected(proj, selc, touch, ksel, seq)
        hp_a = jnp.stack([slopes8, jnp.zeros_like(slopes8)])
        o_win = _banded(proj, hp_a, seq, NSA_WINDOW, lay.qa, lay.kw, lay.vw, False, F32, "nsa_window")
        hp_b = jnp.stack([slopes8, swa_sinks[i].astype(F32)])
        o_b = _banded(proj, hp_b, seq, SWA_WINDOW, lay.qb, lay.kb, lay.vb, True, BF16, "swa_sinks")

        lambda_init = 0.8 - 0.6 * math.exp(-0.3 * i)
        lam_vecs = jnp.stack([diff_lq1[i], diff_lk1[i], diff_lq2[i], diff_lk2[i]]).astype(F32)
        o_d = _diff_attention(proj, lam_vecs, diff_norm_g[i].reshape(1, -1), seq, lambda_init)

        ha = NSA_HEADS * HEAD_DIM
        xf, xb = _merge(
            xf, xb, o_cmp.reshape(t, ha), o_sel.reshape(t, ha), o_win.reshape(t, ha), proj2d,
            o_b.reshape(t, ha), o_d.reshape(t, ha), w_in[i][:, GM_OFF:].astype(BF16), expand,
            w_branch_a[i].astype(BF16), w_branch_b[i].astype(BF16), w_branch_c[i].astype(BF16),
            w_out[i].astype(BF16), ln1_g[i].reshape(1, d), ln1_b[i].reshape(1, d), alpha)

        cw_t = _router(xf, router_w[i].T, router_bias[i].reshape(-1, 1))
        y = _moe_dense(xb, cw_t, wg_all, wu_all, wd_all, i)
        xf, xb = _tail(xf, xb, y, p[i].reshape(t, -1),
                       sh_w_gate[i].astype(BF16), sh_w_up[i].astype(BF16), sh_w_down[i].astype(BF16),
                       ple_w_gate[i].astype(BF16), ple_w_proj[i].astype(BF16),
                       ln2_g[i].reshape(1, d), ln2_b[i].reshape(1, d), alpha)
    return xf.reshape(bsz, seq, d)
```

```python
import functools
import math

import numpy as np
import jax
import jax.numpy as jnp
from jax import lax
from jax.experimental import pallas as pl
from jax.experimental.pallas import tpu as pltpu

F32 = jnp.float32
BF16 = jnp.bfloat16

HEAD_DIM = 64
NSA_HEADS = 8
NSA_KV = 2
NSA_BLOCK = 64
NSA_TOPN = 16
NSA_LOCAL = 2
NSA_WINDOW = 512
SWA_HEADS = 8
SWA_KV = 2
SWA_WINDOW = 128
DIFF_HEADS = 4
N_EXPERTS = 64
TOP_K = 8
N_GROUPS = 8
TOPK_GROUPS = 4
ROUTED_SCALE = 2.5
LN_EPS = 1e-5
NEG = -1e30
FORCE = 1e4
MASK_NEG = -(2.0 ** 100)

LANE = 128
SUBLANE = 8
LANE_SHIFT = LANE.bit_length() - 1
BLOCK_SHIFT = NSA_BLOCK.bit_length() - 1
MXU_WIDTH = 256
PROJ_TN = 7 * MXU_WIDTH
GROUP = NSA_HEADS // NSA_KV
NBLK_PAD = 128
VMEM_LIMIT = 56 * 1024 * 1024

IN_SIZES = (NSA_HEADS * HEAD_DIM,) + (NSA_KV * HEAD_DIM,) * 6 + (NSA_HEADS * 3,) + \
    (SWA_HEADS * HEAD_DIM, SWA_KV * HEAD_DIM, SWA_KV * HEAD_DIM) + \
    (DIFF_HEADS * 2 * HEAD_DIM,) * 3
IN_OFFS = np.concatenate([[0], np.cumsum(IN_SIZES)]).tolist()
GM_OFF = IN_OFFS[-1]


def _slopes(n):
    return [2.0 ** (-8.0 * (h + 1) / n) for h in range(n)]


class _Layout:
    def __init__(self):
        src, scale, bias, pa, pb = [], [], [], [], []

        def slot(cols, sc=1.0, consts=(), pos=False, width=LANE):
            s = [-1] * width
            c = [0.0] * width
            a = [0.0] * width
            b = [0.0] * width
            s[:len(cols)] = cols
            for off, val in consts:
                c[off] = val
            if pos:
                a[HEAD_DIM] = 1.0
                b[HEAD_DIM + 1] = 1.0
            start = len(src)
            src.extend(s)
            scale.extend([sc] * width)
            bias.extend(c)
            pa.extend(a)
            pb.extend(b)
            return start

        def rng(base, n):
            return list(range(base, base + n))

        qs = HEAD_DIM ** -0.5
        o = IN_OFFS
        sl8 = _slopes(NSA_HEADS)
        sl4 = _slopes(DIFF_HEADS)
        self.qa = len(src)
        for h in range(NSA_HEADS):
            slot(rng(o[0] + h * HEAD_DIM, HEAD_DIM), qs,
                 [(HEAD_DIM, sl8[h] * LANE), (HEAD_DIM + 1, sl8[h])])
        self.qb = len(src)
        for h in range(SWA_HEADS):
            slot(rng(o[8] + h * HEAD_DIM, HEAD_DIM), qs,
                 [(HEAD_DIM, sl8[h] * LANE), (HEAD_DIM + 1, sl8[h])])
        self.qd = len(src)
        for h in range(DIFF_HEADS):
            for r in range(2):
                slot(rng(o[11] + (h * 2 + r) * HEAD_DIM, HEAD_DIM), qs,
                     [(HEAD_DIM, sl4[h] * LANE), (HEAD_DIM + 1, sl4[h])])
        self.kd = len(src)
        for h in range(DIFF_HEADS):
            for r in range(2):
                slot(rng(o[12] + (h * 2 + r) * HEAD_DIM, HEAD_DIM), pos=True)
        self.vd = len(src)
        for h in range(DIFF_HEADS):
            slot(rng(o[13] + h * 2 * HEAD_DIM, 2 * HEAD_DIM),
                 consts=[(2 * HEAD_DIM, 1.0)], width=2 * LANE)

        def kv_slots(k_src, v_src):
            k_off = len(src)
            for k in range(NSA_KV):
                slot(rng(k_src + k * HEAD_DIM, HEAD_DIM), pos=True)
            v_off = len(src)
            for k in range(NSA_KV):
                slot(rng(v_src + k * HEAD_DIM, HEAD_DIM), consts=[(HEAD_DIM, 1.0)])
            return k_off, v_off

        self.ks, self.vs = kv_slots(o[3], o[4])
        self.kw, self.vw = kv_slots(o[5], o[6])
        self.kb, self.vb = kv_slots(o[9], o[10])
        self.kc = slot(rng(o[1], NSA_KV * HEAD_DIM))
        self.vc = slot(rng(o[2], NSA_KV * HEAD_DIM))
        self.ga = slot([o[7] + h * 3 + c for c in range(3) for h in range(NSA_HEADS)])
        while len(src) % PROJ_TN:
            slot([])
        self.width = len(src)
        aux = np.zeros((SUBLANE, self.width), np.float32)
        aux[0] = bias
        aux[1] = pa
        aux[2] = pb
        self.aux = aux


_LAYOUT = _Layout()


def _cparams(*sem):
    return pltpu.CompilerParams(dimension_semantics=sem, vmem_limit_bytes=VMEM_LIMIT)


def _tile(n, pref):
    t = min(n, pref)
    assert n % t == 0, (n, pref)
    return t


def _iota(shape, dim):
    return lax.broadcasted_iota(jnp.int32, shape, dim)


def _dot(a, b):
    return jnp.dot(a, b, preferred_element_type=F32)


def _dot_nt(a, b):
    return lax.dot_general(a, b, (((1,), (1,)), ((), ())), preferred_element_type=F32)


def _proj_kernel(x_ref, w_ref, aux_ref, o_ref, *, seq, tm):
    acc = _dot(x_ref[...], w_ref[...])
    pos = (pl.program_id(0) * tm) % seq + _iota((tm, 1), 0)
    a = (pos >> LANE_SHIFT).astype(F32)
    b = (pos & (LANE - 1)).astype(F32)
    aux = aux_ref[...]
    o_ref[...] = (acc + aux[0:1] + a * aux[1:2] + b * aux[2:3]).astype(o_ref.dtype)


def _project(xb, w, aux, seq, out_dtype, tn_pref):
    t, d = xb.shape
    n = w.shape[1]
    tm = _tile(t, 1024)
    tn = _tile(n, tn_pref)
    return pl.pallas_call(
        functools.partial(_proj_kernel, seq=seq, tm=tm),
        grid=(t // tm, n // tn),
        in_specs=[pl.BlockSpec((tm, d), lambda i, j: (i, 0)),
                  pl.BlockSpec((d, tn), lambda i, j: (0, j)),
                  pl.BlockSpec((SUBLANE, tn), lambda i, j: (0, j))],
        out_specs=pl.BlockSpec((tm, tn), lambda i, j: (i, j)),
        out_shape=jax.ShapeDtypeStruct((t, n), out_dtype),
        compiler_params=_cparams("parallel", "parallel"),
        name="proj",
    )(xb, w, aux)


def _compress_kernel(x_ref, pe_ref, w1_ref, w2_ref, o_ref):
    xb = (x_ref[0].astype(F32) + pe_ref[0]).astype(BF16)
    h = jax.nn.gelu(_dot(xb, w1_ref[0]))
    o_ref[0] = _dot(h.astype(BF16), w2_ref[0])


def _compress(xs, pes, w1s, w2s):
    _, r, kdim = xs.shape
    hid = w1s.shape[2]
    dh = w2s.shape[2]
    return pl.pallas_call(
        _compress_kernel,
        grid=(2,),
        in_specs=[pl.BlockSpec((1, r, kdim), lambda i: (i, 0, 0)),
                  pl.BlockSpec((1, 1, kdim), lambda i: (i, 0, 0)),
                  pl.BlockSpec((1, kdim, hid), lambda i: (i, 0, 0)),
                  pl.BlockSpec((1, hid, dh), lambda i: (i, 0, 0))],
        out_specs=pl.BlockSpec((1, r, dh), lambda i: (i, 0, 0)),
        out_shape=jax.ShapeDtypeStruct((2, r, dh), F32),
        compiler_params=_cparams("parallel"),
        name="nsa_compress",
    )(xs, pes, w1s, w2s)


def _stack_heads(q, n):
    return jnp.concatenate([q[:, g * LANE:(g + 1) * LANE] for g in range(n)], axis=0)


def _unstack_heads(o, n, tq, width):
    return jnp.concatenate([o[g * tq:(g + 1) * tq, :width] for g in range(n)], axis=1)


def _cmp_kernel(q_ref, kc_ref, vc_ref, oc_ref, sel_ref, touch_ref, *, tq, n_sel):
    i = pl.program_id(2)
    qs = _stack_heads(q_ref[0], GROUP)
    s = _dot_nt(qs, kc_ref[0, 0])
    t1 = i * tq + _iota((tq, 1), 0)
    t = jnp.concatenate([t1] * GROUP, axis=0)
    n = _iota((1, NBLK_PAD), 1)
    valid = (n * NSA_BLOCK + (NSA_BLOCK - 1)) <= t
    s = jnp.where(valid, s, NEG)
    mx = jnp.max(s, axis=-1, keepdims=True)
    e = jnp.where(valid, jnp.exp(s - mx), 0.0)
    p = e / jnp.maximum(jnp.sum(e, axis=-1, keepdims=True), 1e-30)
    o = _dot(p.astype(BF16), vc_ref[0, 0])
    oc_ref[0] = _unstack_heads(o, GROUP, tq, HEAD_DIM)

    imp = p[0:tq]
    for g in range(1, GROUP):
        imp = imp + p[g * tq:(g + 1) * tq]
    nb = _iota((NBLK_PAD, 1), 0)
    rel = ((i * tq + _iota((1, tq), 1)) >> BLOCK_SHIFT) - nb
    forced = (nb == 0) | ((rel >= 0) & (rel < NSA_LOCAL))
    val = jnp.where(forced, FORCE, jnp.where(rel >= 0, imp.T, -1.0))
    sel = jnp.zeros((NBLK_PAD, tq), F32)
    nf = nb.astype(F32)
    for _ in range(n_sel):
        mx = jnp.max(val, axis=0, keepdims=True)
        idx = jnp.min(jnp.where(val == mx, nf, float(NBLK_PAD)), axis=0, keepdims=True)
        pick = nf == idx
        sel = jnp.where(pick & (mx >= 0.0), 1.0, sel)
        val = jnp.where(pick, -jnp.inf, val)
    sel = sel.T
    sel_ref[0, 0] = (1.0 - sel).astype(BF16)
    touch_ref[0, 0, 0] = jnp.broadcast_to(jnp.max(sel, axis=0, keepdims=True), (SUBLANE, NBLK_PAD))


def _nsa_compressed(proj, kca, vca, seq, n_sel):
    b = proj.shape[0]
    tq = _tile(seq, 256)
    lay = _LAYOUT
    gw = GROUP * LANE
    return pl.pallas_call(
        functools.partial(_cmp_kernel, tq=tq, n_sel=n_sel),
        grid=(b, NSA_KV, seq // tq),
        in_specs=[pl.BlockSpec((1, tq, gw), lambda b_, k, i: (b_, i, lay.qa // gw + k)),
                  pl.BlockSpec((1, 1, NBLK_PAD, LANE), lambda b_, k, i: (b_, k, 0, 0)),
                  pl.BlockSpec((1, 1, NBLK_PAD, LANE), lambda b_, k, i: (b_, k, 0, 0))],
        out_specs=[pl.BlockSpec((1, tq, GROUP * HEAD_DIM), lambda b_, k, i: (b_, i, k)),
                   pl.BlockSpec((1, 1, tq, NBLK_PAD), lambda b_, k, i: (b_, k, i, 0)),
                   pl.BlockSpec((1, 1, 1, SUBLANE, NBLK_PAD), lambda b_, k, i: (b_, k, i, 0, 0))],
        out_shape=[jax.ShapeDtypeStruct((b, seq, NSA_HEADS * HEAD_DIM), F32),
                   jax.ShapeDtypeStruct((b, NSA_KV, seq, NBLK_PAD), BF16),
                   jax.ShapeDtypeStruct((b, NSA_KV, seq // tq, SUBLANE, NBLK_PAD), F32)],
        compiler_params=_cparams("parallel", "parallel", "parallel"),
        name="nsa_compressed",
    )(proj, kca, vca)


def _flash(chunk_fns, v_ref, i, tq, tk, m_ref, acc_ref, s_refs, n_active=None, tile_at=None):
    m_ref[...] = jnp.full(m_ref.shape, NEG, F32)
    acc_ref[...] = jnp.zeros(acc_ref.shape, F32)
    acc_reps = acc_ref.shape[1] // LANE
    n_full = (i * tq) // tk
    if n_active is None:
        n_active = n_full
        tile_at = lambda n: n

    def nth_tile(n):
        return jnp.where(n < n_active, tile_at(n), n_full)

    def issue(n, s_ref):
        start = pl.multiple_of(nth_tile(n) * tk, tk)
        for c, fn in enumerate(chunk_fns):
            s_ref[c * tq:(c + 1) * tq, :] = fn(start)

    def accumulate(j, s_ref, visible):
        start = pl.multiple_of(j * tk, tk)
        vt = v_ref[pl.ds(start, tk), :]
        for c in range(len(chunk_fns)):
            rows = slice(c * tq, (c + 1) * tq)
            s = s_ref[rows, :]
            if visible is not None:
                s = jnp.where(visible, s, NEG)
            m_old = m_ref[rows, :]
            m_new = jnp.maximum(m_old, jnp.max(s, axis=-1, keepdims=True))
            p = jnp.exp((s - jnp.concatenate([m_new] * (tk // LANE), axis=1)).astype(BF16))
            alpha = jnp.concatenate([jnp.exp(m_old - m_new)] * acc_reps, axis=1)
            acc_ref[rows, :] = alpha * acc_ref[rows, :] + _dot(p, vt)
            m_ref[rows, :] = m_new

    s0, s1 = s_refs
    issue(0, s0)

    def body(pair, carry):
        n = 2 * pair
        issue(n + 1, s1)
        accumulate(nth_tile(n), s0, None)
        issue(n + 2, s0)
        accumulate(nth_tile(n + 1), s1, None)
        return carry

    pairs = n_active // 2
    lax.fori_loop(0, pairs, body, 0)
    visible = (_iota((tq, tk), 1) - _iota((tq, tk), 0)) <= i * tq - n_full * tk
    leftover = n_active - 2 * pairs

    @pl.when(leftover == 1)
    def _():
        issue(n_active, s1)
        accumulate(nth_tile(n_active - 1), s0, None)
        accumulate(n_full, s1, visible)

    @pl.when(leftover == 0)
    def _():
        accumulate(n_full, s0, visible)


def _sel_kernel(order_ref, count_ref, q_ref, sel_ref, k_ref, v_ref, o_ref, qa_ref, m_ref, acc_ref,
                s0_ref, s1_ref, *, tq, tk, n_tiles):
    i = pl.program_id(2)
    tile_id = (pl.program_id(0) * pl.num_programs(1) + pl.program_id(1)) * pl.num_programs(2) + i
    q = q_ref[0]
    sc = sel_ref[0, 0]
    for g in range(GROUP):
        qa_ref[g * tq:(g + 1) * tq, 0:LANE] = q[:, g * LANE:(g + 1) * LANE]
        qa_ref[g * tq:(g + 1) * tq, LANE:2 * LANE] = sc
    kv = k_ref.at[0, 0]

    def head_scores(g):
        return lambda start: _dot_nt(qa_ref[g * tq:(g + 1) * tq, :], kv[pl.ds(start, tk), :])

    _flash([head_scores(g) for g in range(GROUP)], v_ref.at[0], i, tq, tk, m_ref, acc_ref, (s0_ref, s1_ref),
           n_active=count_ref[tile_id], tile_at=lambda n: order_ref[tile_id * n_tiles + n])
    acc = acc_ref[...]
    o = acc / acc[:, HEAD_DIM:HEAD_DIM + 1]
    o_ref[0] = _unstack_heads(o, GROUP, tq, HEAD_DIM)


def _nsa_selected(proj, selc, touch, ksel, seq):
    b = proj.shape[0]
    tq = _tile(seq, 256)
    tk = _tile(seq, 1024)
    n_tiles = seq // tk
    lay = _LAYOUT
    gw = GROUP * LANE
    per_tile = touch[:, :, :, 0, :seq // NSA_BLOCK].reshape(b, NSA_KV, seq // tq, n_tiles, tk // NSA_BLOCK)
    n_full = (np.arange(seq // tq) * tq) // tk
    active = (jnp.max(per_tile, axis=-1) > 0.0) & jnp.asarray(np.arange(n_tiles)[None, :] < n_full[:, None])
    order = jnp.argsort(jnp.where(active, 0, 1), axis=-1, stable=True).astype(jnp.int32).reshape(-1)
    count = jnp.sum(active, axis=-1).astype(jnp.int32).reshape(-1)
    grid_spec = pltpu.PrefetchScalarGridSpec(
        num_scalar_prefetch=2,
        grid=(b, NSA_KV, seq // tq),
        in_specs=[pl.BlockSpec((1, tq, gw), lambda b_, k, i, o, c: (b_, i, lay.qa // gw + k)),
                  pl.BlockSpec((1, 1, tq, NBLK_PAD), lambda b_, k, i, o, c: (b_, k, i, 0)),
                  pl.BlockSpec((1, 1, seq, 2 * LANE), lambda b_, k, i, o, c: (b_, k, 0, 0)),
                  pl.BlockSpec((1, seq, LANE), lambda b_, k, i, o, c: (b_, 0, lay.vs // LANE + k))],
        out_specs=pl.BlockSpec((1, tq, GROUP * HEAD_DIM), lambda b_, k, i, o, c: (b_, i, k)),
        scratch_shapes=[pltpu.VMEM((GROUP * tq, 2 * LANE), BF16),
                        pltpu.VMEM((GROUP * tq, LANE), F32),
                        pltpu.VMEM((GROUP * tq, LANE), F32),
                        pltpu.VMEM((GROUP * tq, tk), F32),
                        pltpu.VMEM((GROUP * tq, tk), F32)])
    return pl.pallas_call(
        functools.partial(_sel_kernel, tq=tq, tk=tk, n_tiles=n_tiles),
        grid_spec=grid_spec,
        out_shape=jax.ShapeDtypeStruct((b, seq, NSA_HEADS * HEAD_DIM), F32),
        compiler_params=_cparams("parallel", "parallel", "arbitrary"),
        name="nsa_selected",
    )(order, count, proj, selc, ksel, proj)


def _diff_kernel(lam_ref, q_ref, k_ref, v_ref, g_ref, o_ref, m_ref, acc_ref, s0_ref, s1_ref,
                 *, tq, tk, lambda_init):
    i = pl.program_id(2)
    kv = k_ref.at[0]

    def map_scores(r):
        cols = slice(r * LANE, (r + 1) * LANE)
        return lambda start: _dot_nt(q_ref[0, :, cols], kv[pl.ds(start, tk), cols])

    _flash([map_scores(0), map_scores(1)], v_ref.at[0], i, tq, tk, m_ref, acc_ref, (s0_ref, s1_ref))
    acc = acc_ref[...]
    dv = 2 * HEAD_DIM
    a = acc[:, :dv] / acc[:, dv:dv + 1]
    lv = lam_ref[...]
    lam = (jnp.exp(jnp.sum(lv[0:1] * lv[1:2], axis=-1, keepdims=True))
           - jnp.exp(jnp.sum(lv[2:3] * lv[3:4], axis=-1, keepdims=True)) + lambda_init)
    o = a[0:tq] - lam * a[tq:2 * tq]
    o = o * lax.rsqrt(jnp.mean(o * o, axis=-1, keepdims=True) + LN_EPS) * g_ref[...]
    o_ref[0] = (o * (1.0 - lambda_init)).astype(o_ref.dtype)


def _diff_attention(proj, lam_vecs, norm_g, seq, lambda_init):
    b = proj.shape[0]
    tq = _tile(seq, 1024)
    tk = _tile(seq, 1024)
    lay = _LAYOUT
    w2 = 2 * LANE
    return pl.pallas_call(
        functools.partial(_diff_kernel, tq=tq, tk=tk, lambda_init=lambda_init),
        grid=(b, DIFF_HEADS, seq // tq),
        in_specs=[pl.BlockSpec((4, HEAD_DIM), lambda b_, h, i: (0, 0)),
                  pl.BlockSpec((1, tq, w2), lambda b_, h, i: (b_, i, lay.qd // w2 + h)),
                  pl.BlockSpec((1, seq, w2), lambda b_, h, i: (b_, 0, lay.kd // w2 + h)),
                  pl.BlockSpec((1, seq, w2), lambda b_, h, i: (b_, 0, lay.vd // w2 + h)),
                  pl.BlockSpec((1, 2 * HEAD_DIM), lambda b_, h, i: (0, 0))],
        out_specs=pl.BlockSpec((1, tq, 2 * HEAD_DIM), lambda b_, h, i: (b_, i, h)),
        out_shape=jax.ShapeDtypeStruct((b, seq, DIFF_HEADS * 2 * HEAD_DIM), BF16),
        scratch_shapes=[pltpu.VMEM((2 * tq, LANE), F32),
                        pltpu.VMEM((2 * tq, w2), F32),
                        pltpu.VMEM((2 * tq, tk), F32),
                        pltpu.VMEM((2 * tq, tk), F32)],
        compiler_params=_cparams("parallel", "parallel", "arbitrary"),
        name="diff_attention",
    )(lam_vecs, proj, proj, proj, norm_g)


def _band_kernel(hp_ref, q_ref, k_ref, v_ref, o_ref, *, tq, window, band, use_sinks):
    kvh = pl.program_id(1)
    i = pl.program_id(2)
    start = pl.multiple_of(jnp.maximum(i * tq - window, 0), LANE)
    kb = k_ref[0, pl.ds(start, band), :]
    vb = v_ref[0, pl.ds(start, band), :]
    t1 = i * tq + _iota((tq, 1), 0)
    dist = t1 - (start + _iota((1, band), 1))
    visible = (dist >= 0) & (dist < window)
    outs = []
    scores = [_dot_nt(q_ref[0, :, g * LANE:(g + 1) * LANE], kb) for g in range(GROUP)]
    for g in range(GROUP):
        s = jnp.where(visible, scores[g], NEG)
        mx = jnp.max(s, axis=-1, keepdims=True)
        if use_sinks:
            h = kvh * GROUP + g
            sink = hp_ref[1, h] + hp_ref[0, h] * t1.astype(F32)
            mx = jnp.maximum(mx, sink)
        acc = _dot(jnp.exp((s - mx).astype(BF16)), vb)
        den = acc[:, HEAD_DIM:HEAD_DIM + 1]
        if use_sinks:
            den = den + jnp.exp(sink - mx)
        outs.append((acc / den)[:, :HEAD_DIM])
    o_ref[0] = jnp.concatenate(outs, axis=1).astype(o_ref.dtype)


def _banded(proj, head_params, seq, window, q_off, k_off, v_off, use_sinks, out_dtype, name):
    b = proj.shape[0]
    tq = _tile(seq, 256)
    band = min(window + tq, seq)
    gw = GROUP * LANE
    return pl.pallas_call(
        functools.partial(_band_kernel, tq=tq, window=window, band=band, use_sinks=use_sinks),
        grid=(b, NSA_KV, seq // tq),
        in_specs=[pl.BlockSpec(memory_space=pltpu.SMEM),
                  pl.BlockSpec((1, tq, gw), lambda b_, k, i: (b_, i, q_off // gw + k)),
                  pl.BlockSpec((1, seq, LANE), lambda b_, k, i: (b_, 0, k_off // LANE + k)),
                  pl.BlockSpec((1, seq, LANE), lambda b_, k, i: (b_, 0, v_off // LANE + k))],
        out_specs=pl.BlockSpec((1, tq, GROUP * HEAD_DIM), lambda b_, k, i: (b_, i, k)),
        out_shape=jax.ShapeDtypeStruct((b, seq, NSA_HEADS * HEAD_DIM), out_dtype),
        compiler_params=_cparams("parallel", "parallel", "parallel"),
        name=name,
    )(head_params, proj, proj, proj)


def _layer_norm(z, g, b):
    mu = jnp.mean(z, axis=-1, keepdims=True)
    zc = z - mu
    var = jnp.mean(zc * zc, axis=-1, keepdims=True)
    return zc * lax.rsqrt(var + LN_EPS) * g + b


def _merge_kernel(x_ref, xin_ref, oc_ref, os_ref, ow_ref, ga_ref, ob_ref, od_ref, wgm_ref, ex_ref,
                  wa_ref, wb_ref, wc_ref, wo_ref, lng_ref, lnb_ref, xo_ref, xb_ref, *, alpha):
    d = x_ref.shape[1]
    ga = jax.nn.sigmoid(ga_ref[...].astype(F32)).astype(BF16)
    oa = (_dot(ga, ex_ref[0]) * oc_ref[...] + _dot(ga, ex_ref[1]) * os_ref[...]
          + _dot(ga, ex_ref[2]) * ow_ref[...])
    gm = jax.nn.sigmoid(_dot(xin_ref[...], wgm_ref[...]))
    merged = (gm[:, 0:d] * _dot(oa.astype(BF16), wa_ref[...])
              + gm[:, d:2 * d] * _dot(ob_ref[...], wb_ref[...])
              + gm[:, 2 * d:3 * d] * _dot(od_ref[...], wc_ref[...]))
    z = alpha * x_ref[...] + _dot(merged.astype(BF16), wo_ref[...])
    y = _layer_norm(z, lng_ref[...], lnb_ref[...])
    xo_ref[...] = y
    xb_ref[...] = y.astype(BF16)


def _merge(x, xb, oc, os_, ow, proj2d, ob, od, wgm, expand, wa, wb, wc, wo, lng, lnb, alpha):
    t, d = x.shape
    tm = _tile(t, 256)
    ha = NSA_HEADS * HEAD_DIM
    row = lambda w: pl.BlockSpec((tm, w), lambda i: (i, 0))
    full = lambda a: pl.BlockSpec(a.shape, lambda i: (0,) * a.ndim)
    return pl.pallas_call(
        functools.partial(_merge_kernel, alpha=alpha),
        grid=(t // tm,),
        in_specs=[row(d), row(d), row(ha), row(ha), row(ha),
                  pl.BlockSpec((tm, LANE), lambda i: (i, _LAYOUT.ga // LANE)),
                  row(ha), row(ha), full(wgm), full(expand),
                  full(wa), full(wb), full(wc), full(wo), full(lng), full(lnb)],
        out_specs=[row(d), row(d)],
        out_shape=[jax.ShapeDtypeStruct((t, d), F32), jax.ShapeDtypeStruct((t, d), BF16)],
        compiler_params=_cparams("parallel"),
        name="merge_norm",
    )(x, xb, oc, os_, ow, proj2d, ob, od, wgm, expand, wa, wb, wc, wo, lng, lnb)


def _first_max(v, idx, n):
    mx = jnp.max(v, axis=0, keepdims=True)
    first = jnp.min(jnp.where(v == mx, idx, n), axis=0, keepdims=True)
    return mx, idx == first


def _router_kernel(x_ref, rw_ref, rb_ref, cw_ref):
    gsz = N_EXPERTS // N_GROUPS
    logits = lax.dot_general(rw_ref[...], x_ref[...], (((1,), (1,)), ((), ())),
                             precision=lax.Precision.HIGHEST,
                             preferred_element_type=F32)
    tm = logits.shape[1]
    scores = jax.nn.sigmoid(logits)
    biased = scores + rb_ref[...]
    iw = _iota((gsz, tm), 0)
    gs = []
    for g in range(N_GROUPS):
        blk = biased[g * gsz:(g + 1) * gsz]
        m1, pick = _first_max(blk, iw, gsz)
        m2 = jnp.max(jnp.where(pick, -jnp.inf, blk), axis=0, keepdims=True)
        gs.append(m1 + m2)
    gs = jnp.concatenate(gs, axis=0)
    ig = _iota((N_GROUPS, tm), 0)
    gmask = jnp.zeros((N_GROUPS, tm), F32)
    for _ in range(TOPK_GROUPS):
        _, pick = _first_max(gs, ig, N_GROUPS)
        gmask = jnp.where(pick, 1.0, gmask)
        gs = jnp.where(pick, -jnp.inf, gs)
    emask = jnp.concatenate(
        [jnp.broadcast_to(gmask[g:g + 1], (gsz, tm)) for g in range(N_GROUPS)], axis=0)
    cand = jnp.where(emask > 0.5, biased, NEG)
    ie = _iota((N_EXPERTS, tm), 0)
    chosen = jnp.zeros((N_EXPERTS, tm), F32)
    for _ in range(TOP_K):
        _, pick = _first_max(cand, ie, N_EXPERTS)
        chosen = jnp.where(pick, 1.0, chosen)
        cand = jnp.where(pick, -jnp.inf, cand)
    w = scores * chosen
    cw_ref[...] = w / jnp.sum(w, axis=0, keepdims=True) * ROUTED_SCALE


def _router(x, rw_t, rb):
    t, d = x.shape
    tm = _tile(t, 512)
    return pl.pallas_call(
        _router_kernel,
        grid=(t // tm,),
        in_specs=[pl.BlockSpec((tm, d), lambda i: (i, 0)),
                  pl.BlockSpec((N_EXPERTS, d), lambda i: (0, 0)),
                  pl.BlockSpec((N_EXPERTS, 1), lambda i: (0, 0))],
        out_specs=pl.BlockSpec((N_EXPERTS, tm), lambda i: (0, i)),
        out_shape=jax.ShapeDtypeStruct((N_EXPERTS, t), F32),
        compiler_params=_cparams("parallel"),
        name="router",
    )(x, rw_t, rb)


def _moe_kernel(x_ref, cw_ref, wg_ref, wu_ref, wd_ref, y_ref, *, ne):
    c = pl.program_id(1)
    x = x_ref[...]
    edim = wg_ref.shape[2]
    cw = cw_ref[...]
    hs = []
    for e in range(ne):
        h = jax.nn.silu(_dot(x, wg_ref[e])) * _dot(x, wu_ref[e]) * cw[:, e:e + 1]
        hs.append(h.astype(BF16))
    y = _dot(jnp.concatenate(hs, axis=1), wd_ref[...].reshape(ne * edim, wd_ref.shape[2]))

    @pl.when(c == 0)
    def _():
        y_ref[...] = y

    @pl.when(c != 0)
    def _():
        y_ref[...] += y


def _moe_dense(xb, cw_t, wg, wu, wd, layer):
    t, d = xb.shape
    edim = wg.shape[3]
    tm = _tile(t, 512)
    ne = 8
    cw = cw_t.reshape(N_EXPERTS // ne, ne, t).transpose(0, 2, 1)
    return pl.pallas_call(
        functools.partial(_moe_kernel, ne=ne),
        grid=(t // tm, N_EXPERTS // ne),
        in_specs=[pl.BlockSpec((tm, d), lambda i, c: (i, 0)),
                  pl.BlockSpec((None, tm, ne), lambda i, c: (c, i, 0)),
                  pl.BlockSpec((None, ne, d, edim), lambda i, c: (layer, c, 0, 0)),
                  pl.BlockSpec((None, ne, d, edim), lambda i, c: (layer, c, 0, 0)),
                  pl.BlockSpec((None, ne, edim, d), lambda i, c: (layer, c, 0, 0))],
        out_specs=pl.BlockSpec((tm, d), lambda i, c: (i, 0)),
        out_shape=jax.ShapeDtypeStruct((t, d), F32),
        compiler_params=_cparams("parallel", "arbitrary"),
        name="moe_experts",
    )(xb, cw, wg, wu, wd)


def _tail_kernel(x_ref, xb_ref, y_ref, p_ref, sg_ref, su_ref, sd_ref, pg_ref, pp_ref,
                 lng_ref, lnb_ref, xo_ref, xbo_ref, *, alpha):
    xb = xb_ref[...]
    h = jax.nn.silu(_dot(xb, sg_ref[...])) * _dot(xb, su_ref[...])
    shared = _dot(h.astype(BF16), sd_ref[...])
    ple = jax.nn.sigmoid(_dot(xb, pg_ref[...])) * _dot(p_ref[...].astype(BF16), pp_ref[...])
    z = alpha * x_ref[...] + y_ref[...] + shared + ple
    y = _layer_norm(z, lng_ref[...], lnb_ref[...])
    xo_ref[...] = y
    xbo_ref[...] = y.astype(BF16)


def _tail(x, xb, y, p, sg, su, sd, pg, pp, lng, lnb, alpha):
    t, d = x.shape
    tm = _tile(t, 256)
    row = lambda w: pl.BlockSpec((tm, w), lambda i: (i, 0))
    full = lambda a: pl.BlockSpec(a.shape, lambda i: (0,) * a.ndim)
    return pl.pallas_call(
        functools.partial(_tail_kernel, alpha=alpha),
        grid=(t // tm,),
        in_specs=[row(d), row(d), row(d), row(p.shape[1]), full(sg), full(su), full(sd),
                  full(pg), full(pp), full(lng), full(lnb)],
        out_specs=[row(d), row(d)],
        out_shape=[jax.ShapeDtypeStruct((t, d), F32), jax.ShapeDtypeStruct((t, d), BF16)],
        compiler_params=_cparams("parallel"),
        name="ffn_tail_norm",
    )(x, xb, y, p, sg, su, sd, pg, pp, lng, lnb)


def _slot_weights(w):
    d = w.shape[0]
    o = IN_OFFS
    qs = HEAD_DIM ** -0.5

    def heads(off, n, width=HEAD_DIM, slot=LANE, scale=1.0):
        blk = w[:, off:off + n * width].reshape(d, n, width) * scale
        return jnp.pad(blk, ((0, 0), (0, 0), (0, slot - width))).reshape(d, n * slot)

    gates = w[:, o[7]:o[8]].reshape(d, NSA_HEADS, 3).transpose(0, 2, 1).reshape(d, 3 * NSA_HEADS)
    parts = [heads(o[0], NSA_HEADS, scale=qs), heads(o[8], SWA_HEADS, scale=qs),
             heads(o[11], 2 * DIFF_HEADS, scale=qs), heads(o[12], 2 * DIFF_HEADS),
             heads(o[13], DIFF_HEADS, 2 * HEAD_DIM, 2 * LANE),
             heads(o[3], NSA_KV), heads(o[4], NSA_KV), heads(o[5], NSA_KV), heads(o[6], NSA_KV),
             heads(o[9], SWA_KV), heads(o[10], SWA_KV),
             w[:, o[1]:o[2]], w[:, o[2]:o[3]],
             jnp.pad(gates, ((0, 0), (0, LANE - 3 * NSA_HEADS)))]
    used = sum(part.shape[1] for part in parts)
    parts.append(jnp.zeros((d, _LAYOUT.width - used), F32))
    out = jnp.concatenate(parts, axis=1)
    assert out.shape[1] == _LAYOUT.width
    return out.astype(BF16)


def _compressed_slots(c, pos_cols):
    b, nb, hkv, dh = c.shape
    c = c.transpose(0, 2, 1, 3)
    extra = jnp.zeros((b, hkv, nb, LANE - dh), F32)
    if pos_cols:
        cpos = np.arange(nb) * NSA_BLOCK + NSA_BLOCK - 1
        cols = np.zeros((nb, LANE - dh), np.float32)
        cols[:, 0] = cpos // LANE
        cols[:, 1] = cpos % LANE
        extra = extra + jnp.asarray(cols)
    c = jnp.concatenate([c, extra], axis=-1)
    c = jnp.pad(c, ((0, 0), (0, 0), (0, NBLK_PAD - nb), (0, 0)))
    return c.astype(BF16)


def kernel(x, p, w_in, nsa_pe_k, nsa_w1_k, nsa_w2_k, nsa_pe_v, nsa_w1_v, nsa_w2_v, swa_sinks,
           diff_lq1, diff_lk1, diff_lq2, diff_lk2, diff_norm_g, w_branch_a, w_branch_b, w_branch_c,
           w_out, ln1_g, ln1_b, router_w, router_bias, exp_w_gate, exp_w_up, exp_w_down,
           sh_w_gate, sh_w_up, sh_w_down, ple_w_proj, ple_w_gate, ln2_g, ln2_b):
    bsz, seq, d = x.shape
    depth = w_in.shape[0]
    t = bsz * seq
    alpha = (2 * depth) ** 0.25
    lay = _LAYOUT
    nb = seq // NSA_BLOCK
    assert nb <= NBLK_PAD and seq % LANE == 0
    n_sel = min(NSA_TOPN, nb)

    aux = jnp.asarray(lay.aux)
    blk_mask = jnp.asarray(
        (np.arange(seq)[:, None] // NSA_BLOCK == np.arange(NBLK_PAD)[None, :]) * MASK_NEG, BF16)
    ex = np.zeros((3, LANE, NSA_HEADS * HEAD_DIM), np.float32)
    for c in range(3):
        for h in range(NSA_HEADS):
            ex[c, c * NSA_HEADS + h, h * HEAD_DIM:(h + 1) * HEAD_DIM] = 1.0
    expand = jnp.asarray(ex, BF16)
    slopes8 = jnp.asarray(_slopes(NSA_HEADS), F32)

    wg_all = exp_w_gate.astype(BF16)
    wu_all = exp_w_up.astype(BF16)
    wd_all = exp_w_down.astype(BF16)
    xf = x.reshape(t, d)
    xb = xf.astype(BF16)
    for i in range(depth):
        w_slots = _slot_weights(w_in[i])
        proj2d = _project(xb, w_slots, aux, seq, BF16, PROJ_TN)
        proj = proj2d.reshape(bsz, seq, lay.width)

        def blocks(off):
            c = proj[:, :, off:off + LANE].reshape(bsz, nb, NSA_BLOCK, NSA_KV, HEAD_DIM)
            return c.transpose(0, 1, 3, 2, 4).reshape(bsz * nb * NSA_KV, NSA_BLOCK * HEAD_DIM)
        cmp = _compress(
            jnp.stack([blocks(lay.kc), blocks(lay.vc)]),
            jnp.stack([nsa_pe_k[i].reshape(1, -1), nsa_pe_v[i].reshape(1, -1)]),
            jnp.stack([nsa_w1_k[i], nsa_w1_v[i]]).astype(BF16),
            jnp.stack([nsa_w2_k[i], nsa_w2_v[i]]).astype(BF16))
        cmp = cmp.reshape(2, bsz, nb, NSA_KV, HEAD_DIM)
        kca = _compressed_slots(cmp[0], True)
        vca = _compressed_slots(cmp[1], False)

        o_cmp, selc, touch = _nsa_compressed(proj, kca, vca, seq, n_sel)
        ks = proj[:, :, lay.ks:lay.ks + NSA_KV * LANE].reshape(bsz, seq, NSA_KV, LANE).transpose(0, 2, 1, 3)
        ksel = jnp.concatenate(
            [ks, jnp.broadcast_to(blk_mask[None, None], (bsz, NSA_KV, seq, NBLK_PAD))], axis=-1)
        o_sel = _nsa_selected(proj, selc, touch, ksel, seq)
        hp_a = jnp.stack([slopes8, jnp.zeros_like(slopes8)])
        o_win = _banded(proj, hp_a, seq, NSA_WINDOW, lay.qa, lay.kw, lay.vw, False, F32, "nsa_window")
        hp_b = jnp.stack([slopes8, swa_sinks[i].astype(F32)])
        o_b = _banded(proj, hp_b, seq, SWA_WINDOW, lay.qb, lay.kb, lay.vb, True, BF16, "swa_sinks")

        lambda_init = 0.8 - 0.6 * math.exp(-0.3 * i)
        lam_vecs = jnp.stack([diff_lq1[i], diff_lk1[i], diff_lq2[i], diff_lk2[i]]).astype(F32)
        o_d = _diff_attention(proj, lam_vecs, diff_norm_g[i].reshape(1, -1), seq, lambda_init)

        ha = NSA_HEADS * HEAD_DIM
        xf, xb = _merge(
            xf, xb, o_cmp.reshape(t, ha), o_sel.reshape(t, ha), o_win.reshape(t, ha), proj2d,
            o_b.reshape(t, ha), o_d.reshape(t, ha), w_in[i][:, GM_OFF:].astype(BF16), expand,
            w_branch_a[i].astype(BF16), w_branch_b[i].astype(BF16), w_branch_c[i].astype(BF16),
            w_out[i].astype(BF16), ln1_g[i].reshape(1, d), ln1_b[i].reshape(1, d), alpha)

        cw_t = _router(xf, router_w[i].T, router_bias[i].reshape(-1, 1))
        y = _moe_dense(xb, cw_t, wg_all, wu_all, wd_all, i)
        xf, xb = _tail(xf, xb, y, p[i].reshape(t, -1),
                       sh_w_gate[i].astype(BF16), sh_w_up[i].astype(BF16), sh_w_down[i].astype(BF16),
                       ple_w_gate[i].astype(BF16), ple_w_proj[i].astype(BF16),
                       ln2_g[i].reshape(1, d), ln2_b[i].reshape(1, d), alpha)
    return xf.reshape(bsz, seq, d)
```

```python
import functools
import math

import numpy as np
import jax
import jax.numpy as jnp
from jax import lax
from jax.experimental import pallas as pl
from jax.experimental.pallas import tpu as pltpu

F32 = jnp.float32
BF16 = jnp.bfloat16

HEAD_DIM = 64
NSA_HEADS = 8
NSA_KV = 2
NSA_BLOCK = 64
NSA_TOPN = 16
NSA_LOCAL = 2
NSA_WINDOW = 512
SWA_HEADS = 8
SWA_KV = 2
SWA_WINDOW = 128
DIFF_HEADS = 4
N_EXPERTS = 64
TOP_K = 8
N_GROUPS = 8
TOPK_GROUPS = 4
ROUTED_SCALE = 2.5
LN_EPS = 1e-5
NEG = -1e30
FORCE = 1e4
MASK_NEG = -(2.0 ** 100)

LANE = 128
SUBLANE = 8
LANE_SHIFT = LANE.bit_length() - 1
BLOCK_SHIFT = NSA_BLOCK.bit_length() - 1
MXU_WIDTH = 256
PROJ_TN = 7 * MXU_WIDTH
GROUP = NSA_HEADS // NSA_KV
NBLK_PAD = 128
VMEM_LIMIT = 56 * 1024 * 1024

IN_SIZES = (NSA_HEADS * HEAD_DIM,) + (NSA_KV * HEAD_DIM,) * 6 + (NSA_HEADS * 3,) + \
    (SWA_HEADS * HEAD_DIM, SWA_KV * HEAD_DIM, SWA_KV * HEAD_DIM) + \
    (DIFF_HEADS * 2 * HEAD_DIM,) * 3
IN_OFFS = np.concatenate([[0], np.cumsum(IN_SIZES)]).tolist()
GM_OFF = IN_OFFS[-1]


def _slopes(n):
    return [2.0 ** (-8.0 * (h + 1) / n) for h in range(n)]


class _Layout:
    def __init__(self):
        src, scale, bias, pa, pb = [], [], [], [], []

        def slot(cols, sc=1.0, consts=(), pos=False, width=LANE):
            s = [-1] * width
            c = [0.0] * width
            a = [0.0] * width
            b = [0.0] * width
            s[:len(cols)] = cols
            for off, val in consts:
                c[off] = val
            if pos:
                a[HEAD_DIM] = 1.0
                b[HEAD_DIM + 1] = 1.0
            start = len(src)
            src.extend(s)
            scale.extend([sc] * width)
            bias.extend(c)
            pa.extend(a)
            pb.extend(b)
            return start

        def rng(base, n):
            return list(range(base, base + n))

        qs = HEAD_DIM ** -0.5
        o = IN_OFFS
        sl8 = _slopes(NSA_HEADS)
        sl4 = _slopes(DIFF_HEADS)
        self.qa = len(src)
        for h in range(NSA_HEADS):
            slot(rng(o[0] + h * HEAD_DIM, HEAD_DIM), qs,
                 [(HEAD_DIM, sl8[h] * LANE), (HEAD_DIM + 1, sl8[h])])
        self.qb = len(src)
        for h in range(SWA_HEADS):
            slot(rng(o[8] + h * HEAD_DIM, HEAD_DIM), qs,
                 [(HEAD_DIM, sl8[h] * LANE), (HEAD_DIM + 1, sl8[h])])
        self.qd = len(src)
        for h in range(DIFF_HEADS):
            for r in range(2):
                slot(rng(o[11] + (h * 2 + r) * HEAD_DIM, HEAD_DIM), qs,
                     [(HEAD_DIM, sl4[h] * LANE), (HEAD_DIM + 1, sl4[h])])
        self.kd = len(src)
        for h in range(DIFF_HEADS):
            for r in range(2):
                slot(rng(o[12] + (h * 2 + r) * HEAD_DIM, HEAD_DIM), pos=True)
        self.vd = len(src)
        for h in range(DIFF_HEADS):
            slot(rng(o[13] + h * 2 * HEAD_DIM, 2 * HEAD_DIM),
                 consts=[(2 * HEAD_DIM, 1.0)], width=2 * LANE)

        def kv_slots(k_src, v_src):
            k_off = len(src)
            for k in range(NSA_KV):
                slot(rng(k_src + k * HEAD_DIM, HEAD_DIM), pos=True)
            v_off = len(src)
            for k in range(NSA_KV):
                slot(rng(v_src + k * HEAD_DIM, HEAD_DIM), consts=[(HEAD_DIM, 1.0)])
            return k_off, v_off

        self.ks, self.vs = kv_slots(o[3], o[4])
        self.kw, self.vw = kv_slots(o[5], o[6])
        self.kb, self.vb = kv_slots(o[9], o[10])
        self.kc = slot(rng(o[1], NSA_KV * HEAD_DIM))
        self.vc = slot(rng(o[2], NSA_KV * HEAD_DIM))
        self.ga = slot([o[7] + h * 3 + c for c in range(3) for h in range(NSA_HEADS)])
        while len(src) % PROJ_TN:
            slot([])
        self.width = len(src)
        aux = np.zeros((SUBLANE, self.width), np.float32)
        aux[0] = bias
        aux[1] = pa
        aux[2] = pb
        self.aux = aux


_LAYOUT = _Layout()


def _cparams(*sem):
    return pltpu.CompilerParams(dimension_semantics=sem, vmem_limit_bytes=VMEM_LIMIT)


def _tile(n, pref):
    t = min(n, pref)
    assert n % t == 0, (n, pref)
    return t


def _iota(shape, dim):
    return lax.broadcasted_iota(jnp.int32, shape, dim)


def _dot(a, b):
    return jnp.dot(a, b, preferred_element_type=F32)


def _dot_nt(a, b):
    return lax.dot_general(a, b, (((1,), (1,)), ((), ())), preferred_element_type=F32)


def _proj_kernel(x_ref, w_ref, aux_ref, o_ref, *, seq, tm):
    acc = _dot(x_ref[...], w_ref[...])
    pos = (pl.program_id(0) * tm) % seq + _iota((tm, 1), 0)
    a = (pos >> LANE_SHIFT).astype(F32)
    b = (pos & (LANE - 1)).astype(F32)
    aux = aux_ref[...]
    o_ref[...] = (acc + aux[0:1] + a * aux[1:2] + b * aux[2:3]).astype(o_ref.dtype)


def _project(xb, w, aux, seq, out_dtype, tn_pref):
    t, d = xb.shape
    n = w.shape[1]
    tm = _tile(t, 1024)
    tn = _tile(n, tn_pref)
    return pl.pallas_call(
        functools.partial(_proj_kernel, seq=seq, tm=tm),
        grid=(t // tm, n // tn),
        in_specs=[pl.BlockSpec((tm, d), lambda i, j: (i, 0)),
                  pl.BlockSpec((d, tn), lambda i, j: (0, j)),
                  pl.BlockSpec((SUBLANE, tn), lambda i, j: (0, j))],
        out_specs=pl.BlockSpec((tm, tn), lambda i, j: (i, j)),
        out_shape=jax.ShapeDtypeStruct((t, n), out_dtype),
        compiler_params=_cparams("parallel", "parallel"),
        name="proj",
    )(xb, w, aux)


def _compress_kernel(x_ref, pe_ref, w1_ref, w2_ref, o_ref):
    xb = (x_ref[0].astype(F32) + pe_ref[0]).astype(BF16)
    h = jax.nn.gelu(_dot(xb, w1_ref[0]))
    o_ref[0] = _dot(h.astype(BF16), w2_ref[0])


def _compress(xs, pes, w1s, w2s):
    _, r, kdim = xs.shape
    hid = w1s.shape[2]
    dh = w2s.shape[2]
    return pl.pallas_call(
        _compress_kernel,
        grid=(2,),
        in_specs=[pl.BlockSpec((1, r, kdim), lambda i: (i, 0, 0)),
                  pl.BlockSpec((1, 1, kdim), lambda i: (i, 0, 0)),
                  pl.BlockSpec((1, kdim, hid), lambda i: (i, 0, 0)),
                  pl.BlockSpec((1, hid, dh), lambda i: (i, 0, 0))],
        out_specs=pl.BlockSpec((1, r, dh), lambda i: (i, 0, 0)),
        out_shape=jax.ShapeDtypeStruct((2, r, dh), F32),
        compiler_params=_cparams("parallel"),
        name="nsa_compress",
    )(xs, pes, w1s, w2s)


def _stack_heads(q, n):
    return jnp.concatenate([q[:, g * LANE:(g + 1) * LANE] for g in range(n)], axis=0)


def _unstack_heads(o, n, tq, width):
    return jnp.concatenate([o[g * tq:(g + 1) * tq, :width] for g in range(n)], axis=1)


def _cmp_kernel(q_ref, kc_ref, vc_ref, oc_ref, sel_ref, touch_ref, *, tq, n_sel):
    i = pl.program_id(2)
    qs = _stack_heads(q_ref[0], GROUP)
    s = _dot_nt(qs, kc_ref[0, 0])
    t1 = i * tq + _iota((tq, 1), 0)
    t = jnp.concatenate([t1] * GROUP, axis=0)
    n = _iota((1, NBLK_PAD), 1)
    valid = (n * NSA_BLOCK + (NSA_BLOCK - 1)) <= t
    s = jnp.where(valid, s, NEG)
    mx = jnp.max(s, axis=-1, keepdims=True)
    e = jnp.where(valid, jnp.exp(s - mx), 0.0)
    p = e / jnp.maximum(jnp.sum(e, axis=-1, keepdims=True), 1e-30)
    o = _dot(p.astype(BF16), vc_ref[0, 0])
    oc_ref[0] = _unstack_heads(o, GROUP, tq, HEAD_DIM)

    imp = p[0:tq]
    for g in range(1, GROUP):
        imp = imp + p[g * tq:(g + 1) * tq]
    nb = _iota((NBLK_PAD, 1), 0)
    rel = ((i * tq + _iota((1, tq), 1)) >> BLOCK_SHIFT) - nb
    forced = (nb == 0) | ((rel >= 0) & (rel < NSA_LOCAL))
    val = jnp.where(forced, FORCE, jnp.where(rel >= 0, imp.T, -1.0))
    sel = jnp.zeros((NBLK_PAD, tq), F32)
    nf = nb.astype(F32)
    for _ in range(n_sel):
        mx = jnp.max(val, axis=0, keepdims=True)
        idx = jnp.min(jnp.where(val == mx, nf, float(NBLK_PAD)), axis=0, keepdims=True)
        pick = nf == idx
        sel = jnp.where(pick & (mx >= 0.0), 1.0, sel)
        val = jnp.where(pick, -jnp.inf, val)
    sel = sel.T
    sel_ref[0, 0] = (1.0 - sel).astype(BF16)
    touch_ref[0, 0, 0] = jnp.broadcast_to(jnp.max(sel, axis=0, keepdims=True), (SUBLANE, NBLK_PAD))


def _nsa_compressed(proj, kca, vca, seq, n_sel):
    b = proj.shape[0]
    tq = _tile(seq, 256)
    lay = _LAYOUT
    gw = GROUP * LANE
    return pl.pallas_call(
        functools.partial(_cmp_kernel, tq=tq, n_sel=n_sel),
        grid=(b, NSA_KV, seq // tq),
        in_specs=[pl.BlockSpec((1, tq, gw), lambda b_, k, i: (b_, i, lay.qa // gw + k)),
                  pl.BlockSpec((1, 1, NBLK_PAD, LANE), lambda b_, k, i: (b_, k, 0, 0)),
                  pl.BlockSpec((1, 1, NBLK_PAD, LANE), lambda b_, k, i: (b_, k, 0, 0))],
        out_specs=[pl.BlockSpec((1, tq, GROUP * HEAD_DIM), lambda b_, k, i: (b_, i, k)),
                   pl.BlockSpec((1, 1, tq, NBLK_PAD), lambda b_, k, i: (b_, k, i, 0)),
                   pl.BlockSpec((1, 1, 1, SUBLANE, NBLK_PAD), lambda b_, k, i: (b_, k, i, 0, 0))],
        out_shape=[jax.ShapeDtypeStruct((b, seq, NSA_HEADS * HEAD_DIM), F32),
                   jax.ShapeDtypeStruct((b, NSA_KV, seq, NBLK_PAD), BF16),
                   jax.ShapeDtypeStruct((b, NSA_KV, seq // tq, SUBLANE, NBLK_PAD), F32)],
        compiler_params=_cparams("parallel", "parallel", "parallel"),
        name="nsa_compressed",
    )(proj, kca, vca)


def _flash(chunk_fns, v_ref, i, tq, tk, m_ref, acc_ref, s_refs, n_active=None, tile_at=None):
    m_ref[...] = jnp.full(m_ref.shape, NEG, F32)
    acc_ref[...] = jnp.zeros(acc_ref.shape, F32)
    acc_reps = acc_ref.shape[1] // LANE
    n_full = (i * tq) // tk
    if n_active is None:
        n_active = n_full
        tile_at = lambda n: n

    def nth_tile(n):
        return jnp.where(n < n_active, tile_at(n), n_full)

    def issue(n, s_ref):
        start = pl.multiple_of(nth_tile(n) * tk, tk)
        for c, fn in enumerate(chunk_fns):
            s_ref[c * tq:(c + 1) * tq, :] = fn(start)

    def accumulate(j, s_ref, visible):
        start = pl.multiple_of(j * tk, tk)
        vt = v_ref[pl.ds(start, tk), :]
        for c in range(len(chunk_fns)):
            rows = slice(c * tq, (c + 1) * tq)
            s = s_ref[rows, :]
            if visible is not None:
                s = jnp.where(visible, s, NEG)
            m_old = m_ref[rows, :]
            m_new = jnp.maximum(m_old, jnp.max(s, axis=-1, keepdims=True))
            p = jnp.exp((s - jnp.concatenate([m_new] * (tk // LANE), axis=1)).astype(BF16))
            alpha = jnp.concatenate([jnp.exp(m_old - m_new)] * acc_reps, axis=1)
            acc_ref[rows, :] = alpha * acc_ref[rows, :] + _dot(p, vt)
            m_ref[rows, :] = m_new

    s0, s1 = s_refs
    issue(0, s0)

    def body(pair, carry):
        n = 2 * pair
        issue(n + 1, s1)
        accumulate(nth_tile(n), s0, None)
        issue(n + 2, s0)
        accumulate(nth_tile(n + 1), s1, None)
        return carry

    pairs = n_active // 2
    lax.fori_loop(0, pairs, body, 0)
    visible = (_iota((tq, tk), 1) - _iota((tq, tk), 0)) <= i * tq - n_full * tk
    leftover = n_active - 2 * pairs

    @pl.when(leftover == 1)
    def _():
        issue(n_active, s1)
        accumulate(nth_tile(n_active - 1), s0, None)
        accumulate(n_full, s1, visible)

    @pl.when(leftover == 0)
    def _():
        accumulate(n_full, s0, visible)


def _sel_kernel(order_ref, count_ref, q_ref, sel_ref, k_ref, v_ref, o_ref, qa_ref, m_ref, acc_ref,
                s0_ref, s1_ref, *, tq, tk, n_tiles):
    i = pl.program_id(2)
    tile_id = (pl.program_id(0) * pl.num_programs(1) + pl.program_id(1)) * pl.num_programs(2) + i
    q = q_ref[0]
    sc = sel_ref[0, 0]
    for g in range(GROUP):
        qa_ref[g * tq:(g + 1) * tq, 0:LANE] = q[:, g * LANE:(g + 1) * LANE]
        qa_ref[g * tq:(g + 1) * tq, LANE:2 * LANE] = sc
    kv = k_ref.at[0, 0]

    def head_scores(g):
        return lambda start: _dot_nt(qa_ref[g * tq:(g + 1) * tq, :], kv[pl.ds(start, tk), :])

    _flash([head_scores(g) for g in range(GROUP)], v_ref.at[0], i, tq, tk, m_ref, acc_ref, (s0_ref, s1_ref),
           n_active=count_ref[tile_id], tile_at=lambda n: order_ref[tile_id * n_tiles + n])
    acc = acc_ref[...]
    o = acc / acc[:, HEAD_DIM:HEAD_DIM + 1]
    o_ref[0] = _unstack_heads(o, GROUP, tq, HEAD_DIM)


def _nsa_selected(proj, selc, touch, ksel, seq):
    b = proj.shape[0]
    tq = _tile(seq, 256)
    tk = _tile(seq, 256)
    n_tiles = seq // tk
    lay = _LAYOUT
    gw = GROUP * LANE
    per_tile = touch[:, :, :, 0, :seq // NSA_BLOCK].reshape(b, NSA_KV, seq // tq, n_tiles, tk // NSA_BLOCK)
    n_full = (np.arange(seq // tq) * tq) // tk
    active = (jnp.max(per_tile, axis=-1) > 0.0) & jnp.asarray(np.arange(n_tiles)[None, :] < n_full[:, None])
    order = jnp.argsort(jnp.where(active, 0, 1), axis=-1, stable=True).astype(jnp.int32).reshape(-1)
    count = jnp.sum(active, axis=-1).astype(jnp.int32).reshape(-1)
    grid_spec = pltpu.PrefetchScalarGridSpec(
        num_scalar_prefetch=2,
        grid=(b, NSA_KV, seq // tq),
        in_specs=[pl.BlockSpec((1, tq, gw), lambda b_, k, i, o, c: (b_, i, lay.qa // gw + k)),
                  pl.BlockSpec((1, 1, tq, NBLK_PAD), lambda b_, k, i, o, c: (b_, k, i, 0)),
                  pl.BlockSpec((1, 1, seq, 2 * LANE), lambda b_, k, i, o, c: (b_, k, 0, 0)),
                  pl.BlockSpec((1, seq, LANE), lambda b_, k, i, o, c: (b_, 0, lay.vs // LANE + k))],
        out_specs=pl.BlockSpec((1, tq, GROUP * HEAD_DIM), lambda b_, k, i, o, c: (b_, i, k)),
        scratch_shapes=[pltpu.VMEM((GROUP * tq, 2 * LANE), BF16),
                        pltpu.VMEM((GROUP * tq, LANE), F32),
                        pltpu.VMEM((GROUP * tq, LANE), F32),
                        pltpu.VMEM((GROUP * tq, tk), F32),
                        pltpu.VMEM((GROUP * tq, tk), F32)])
    return pl.pallas_call(
        functools.partial(_sel_kernel, tq=tq, tk=tk, n_tiles=n_tiles),
        grid_spec=grid_spec,
        out_shape=jax.ShapeDtypeStruct((b, seq, NSA_HEADS * HEAD_DIM), F32),
        compiler_params=_cparams("parallel", "parallel", "arbitrary"),
        name="nsa_selected",
    )(order, count, proj, selc, ksel, proj)


def _diff_kernel(lam_ref, q_ref, k_ref, v_ref, g_ref, o_ref, m_ref, acc_ref, s0_ref, s1_ref,
                 *, tq, tk, lambda_init):
    i = pl.program_id(2)
    kv = k_ref.at[0]

    def map_scores(r):
        cols = slice(r * LANE, (r + 1) * LANE)
        return lambda start: _dot_nt(q_ref[0, :, cols], kv[pl.ds(start, tk), cols])

    _flash([map_scores(0), map_scores(1)], v_ref.at[0], i, tq, tk, m_ref, acc_ref, (s0_ref, s1_ref))
    acc = acc_ref[...]
    dv = 2 * HEAD_DIM
    a = acc[:, :dv] / acc[:, dv:dv + 1]
    lv = lam_ref[...]
    lam = (jnp.exp(jnp.sum(lv[0:1] * lv[1:2], axis=-1, keepdims=True))
           - jnp.exp(jnp.sum(lv[2:3] * lv[3:4], axis=-1, keepdims=True)) + lambda_init)
    o = a[0:tq] - lam * a[tq:2 * tq]
    o = o * lax.rsqrt(jnp.mean(o * o, axis=-1, keepdims=True) + LN_EPS) * g_ref[...]
    o_ref[0] = (o * (1.0 - lambda_init)).astype(o_ref.dtype)


def _diff_attention(proj, lam_vecs, norm_g, seq, lambda_init):
    b = proj.shape[0]
    tq = _tile(seq, 1024)
    tk = _tile(seq, 1024)
    lay = _LAYOUT
    w2 = 2 * LANE
    return pl.pallas_call(
        functools.partial(_diff_kernel, tq=tq, tk=tk, lambda_init=lambda_init),
        grid=(b, DIFF_HEADS, seq // tq),
        in_specs=[pl.BlockSpec((4, HEAD_DIM), lambda b_, h, i: (0, 0)),
                  pl.BlockSpec((1, tq, w2), lambda b_, h, i: (b_, i, lay.qd // w2 + h)),
                  pl.BlockSpec((1, seq, w2), lambda b_, h, i: (b_, 0, lay.kd // w2 + h)),
                  pl.BlockSpec((1, seq, w2), lambda b_, h, i: (b_, 0, lay.vd // w2 + h)),
                  pl.BlockSpec((1, 2 * HEAD_DIM), lambda b_, h, i: (0, 0))],
        out_specs=pl.BlockSpec((1, tq, 2 * HEAD_DIM), lambda b_, h, i: (b_, i, h)),
        out_shape=jax.ShapeDtypeStruct((b, seq, DIFF_HEADS * 2 * HEAD_DIM), BF16),
        scratch_shapes=[pltpu.VMEM((2 * tq, LANE), F32),
                        pltpu.VMEM((2 * tq, w2), F32),
                        pltpu.VMEM((2 * tq, tk), F32),
                        pltpu.VMEM((2 * tq, tk), F32)],
        compiler_params=_cparams("parallel", "parallel", "arbitrary"),
        name="diff_attention",
    )(lam_vecs, proj, proj, proj, norm_g)


def _band_kernel(hp_ref, q_ref, k_ref, v_ref, o_ref, *, tq, window, band, use_sinks):
    kvh = pl.program_id(1)
    i = pl.program_id(2)
    start = pl.multiple_of(jnp.maximum(i * tq - window, 0), LANE)
    kb = k_ref[0, pl.ds(start, band), :]
    vb = v_ref[0, pl.ds(start, band), :]
    t1 = i * tq + _iota((tq, 1), 0)
    dist = t1 - (start + _iota((1, band), 1))
    visible = (dist >= 0) & (dist < window)
    outs = []
    scores = [_dot_nt(q_ref[0, :, g * LANE:(g + 1) * LANE], kb) for g in range(GROUP)]
    for g in range(GROUP):
        s = jnp.where(visible, scores[g], NEG)
        mx = jnp.max(s, axis=-1, keepdims=True)
        if use_sinks:
            h = kvh * GROUP + g
            sink = hp_ref[1, h] + hp_ref[0, h] * t1.astype(F32)
            mx = jnp.maximum(mx, sink)
        acc = _dot(jnp.exp((s - mx).astype(BF16)), vb)
        den = acc[:, HEAD_DIM:HEAD_DIM + 1]
        if use_sinks:
            den = den + jnp.exp(sink - mx)
        outs.append((acc / den)[:, :HEAD_DIM])
    o_ref[0] = jnp.concatenate(outs, axis=1).astype(o_ref.dtype)


def _banded(proj, head_params, seq, window, q_off, k_off, v_off, use_sinks, out_dtype, name):
    b = proj.shape[0]
    tq = _tile(seq, 256)
    band = min(window + tq, seq)
    gw = GROUP * LANE
    return pl.pallas_call(
        functools.partial(_band_kernel, tq=tq, window=window, band=band, use_sinks=use_sinks),
        grid=(b, NSA_KV, seq // tq),
        in_specs=[pl.BlockSpec(memory_space=pltpu.SMEM),
                  pl.BlockSpec((1, tq, gw), lambda b_, k, i: (b_, i, q_off // gw + k)),
                  pl.BlockSpec((1, seq, LANE), lambda b_, k, i: (b_, 0, k_off // LANE + k)),
                  pl.BlockSpec((1, seq, LANE), lambda b_, k, i: (b_, 0, v_off // LANE + k))],
        out_specs=pl.BlockSpec((1, tq, GROUP * HEAD_DIM), lambda b_, k, i: (b_, i, k)),
        out_shape=jax.ShapeDtypeStruct((b, seq, NSA_HEADS * HEAD_DIM), out_dtype),
        compiler_params=_cparams("parallel", "parallel", "parallel"),
        name=name,
    )(head_params, proj, proj, proj)


def _layer_norm(z, g, b):
    mu = jnp.mean(z, axis=-1, keepdims=True)
    zc = z - mu
    var = jnp.mean(zc * zc, axis=-1, keepdims=True)
    return zc * lax.rsqrt(var + LN_EPS) * g + b


def _merge_kernel(x_ref, xin_ref, oc_ref, os_ref, ow_ref, ga_ref, ob_ref, od_ref, wgm_ref, ex_ref,
                  wa_ref, wb_ref, wc_ref, wo_ref, lng_ref, lnb_ref, xo_ref, xb_ref, *, alpha):
    d = x_ref.shape[1]
    ga = jax.nn.sigmoid(ga_ref[...].astype(F32)).astype(BF16)
    oa = (_dot(ga, ex_ref[0]) * oc_ref[...] + _dot(ga, ex_ref[1]) * os_ref[...]
          + _dot(ga, ex_ref[2]) * ow_ref[...])
    gm = jax.nn.sigmoid(_dot(xin_ref[...], wgm_ref[...]))
    merged = (gm[:, 0:d] * _dot(oa.astype(BF16), wa_ref[...])
              + gm[:, d:2 * d] * _dot(ob_ref[...], wb_ref[...])
              + gm[:, 2 * d:3 * d] * _dot(od_ref[...], wc_ref[...]))
    z = alpha * x_ref[...] + _dot(merged.astype(BF16), wo_ref[...])
    y = _layer_norm(z, lng_ref[...], lnb_ref[...])
    xo_ref[...] = y
    xb_ref[...] = y.astype(BF16)


def _merge(x, xb, oc, os_, ow, proj2d, ob, od, wgm, expand, wa, wb, wc, wo, lng, lnb, alpha):
    t, d = x.shape
    tm = _tile(t, 256)
    ha = NSA_HEADS * HEAD_DIM
    row = lambda w: pl.BlockSpec((tm, w), lambda i: (i, 0))
    full = lambda a: pl.BlockSpec(a.shape, lambda i: (0,) * a.ndim)
    return pl.pallas_call(
        functools.partial(_merge_kernel, alpha=alpha),
        grid=(t // tm,),
        in_specs=[row(d), row(d), row(ha), row(ha), row(ha),
                  pl.BlockSpec((tm, LANE), lambda i: (i, _LAYOUT.ga // LANE)),
                  row(ha), row(ha), full(wgm), full(expand),
                  full(wa), full(wb), full(wc), full(wo), full(lng), full(lnb)],
        out_specs=[row(d), row(d)],
        out_shape=[jax.ShapeDtypeStruct((t, d), F32), jax.ShapeDtypeStruct((t, d), BF16)],
        compiler_params=_cparams("parallel"),
        name="merge_norm",
    )(x, xb, oc, os_, ow, proj2d, ob, od, wgm, expand, wa, wb, wc, wo, lng, lnb)


def _first_max(v, idx, n):
    mx = jnp.max(v, axis=0, keepdims=True)
    first = jnp.min(jnp.where(v == mx, idx, n), axis=0, keepdims=True)
    return mx, idx == first


def _router_kernel(x_ref, rw_ref, rb_ref, cw_ref):
    gsz = N_EXPERTS // N_GROUPS
    logits = lax.dot_general(rw_ref[...], x_ref[...], (((1,), (1,)), ((), ())),
                             precision=lax.Precision.HIGHEST,
                             preferred_element_type=F32)
    tm = logits.shape[1]
    scores = jax.nn.sigmoid(logits)
    biased = scores + rb_ref[...]
    iw = _iota((gsz, tm), 0)
    gs = []
    for g in range(N_GROUPS):
        blk = biased[g * gsz:(g + 1) * gsz]
        m1, pick = _first_max(blk, iw, gsz)
        m2 = jnp.max(jnp.where(pick, -jnp.inf, blk), axis=0, keepdims=True)
        gs.append(m1 + m2)
    gs = jnp.concatenate(gs, axis=0)
    ig = _iota((N_GROUPS, tm), 0)
    gmask = jnp.zeros((N_GROUPS, tm), F32)
    for _ in range(TOPK_GROUPS):
        _, pick = _first_max(gs, ig, N_GROUPS)
        gmask = jnp.where(pick, 1.0, gmask)
        gs = jnp.where(pick, -jnp.inf, gs)
    emask = jnp.concatenate(
        [jnp.broadcast_to(gmask[g:g + 1], (gsz, tm)) for g in range(N_GROUPS)], axis=0)
    cand = jnp.where(emask > 0.5, biased, NEG)
    ie = _iota((N_EXPERTS, tm), 0)
    chosen = jnp.zeros((N_EXPERTS, tm), F32)
    for _ in range(TOP_K):
        _, pick = _first_max(cand, ie, N_EXPERTS)
        chosen = jnp.where(pick, 1.0, chosen)
        cand = jnp.where(pick, -jnp.inf, cand)
    w = scores * chosen
    cw_ref[...] = w / jnp.sum(w, axis=0, keepdims=True) * ROUTED_SCALE


def _router(x, rw_t, rb):
    t, d = x.shape
    tm = _tile(t, 512)
    return pl.pallas_call(
        _router_kernel,
        grid=(t // tm,),
        in_specs=[pl.BlockSpec((tm, d), lambda i: (i, 0)),
                  pl.BlockSpec((N_EXPERTS, d), lambda i: (0, 0)),
                  pl.BlockSpec((N_EXPERTS, 1), lambda i: (0, 0))],
        out_specs=pl.BlockSpec((N_EXPERTS, tm), lambda i: (0, i)),
        out_shape=jax.ShapeDtypeStruct((N_EXPERTS, t), F32),
        compiler_params=_cparams("parallel"),
        name="router",
    )(x, rw_t, rb)


def _moe_kernel(x_ref, cw_ref, wg_ref, wu_ref, wd_ref, y_ref, *, ne):
    c = pl.program_id(1)
    x = x_ref[...]
    edim = wg_ref.shape[2]
    cw = cw_ref[...]
    hs = []
    for e in range(ne):
        h = jax.nn.silu(_dot(x, wg_ref[e])) * _dot(x, wu_ref[e]) * cw[:, e:e + 1]
        hs.append(h.astype(BF16))
    y = _dot(jnp.concatenate(hs, axis=1), wd_ref[...].reshape(ne * edim, wd_ref.shape[2]))

    @pl.when(c == 0)
    def _():
        y_ref[...] = y

    @pl.when(c != 0)
    def _():
        y_ref[...] += y


def _moe_dense(xb, cw_t, wg, wu, wd, layer):
    t, d = xb.shape
    edim = wg.shape[3]
    tm = _tile(t, 512)
    ne = 8
    cw = cw_t.reshape(N_EXPERTS // ne, ne, t).transpose(0, 2, 1)
    return pl.pallas_call(
        functools.partial(_moe_kernel, ne=ne),
        grid=(t // tm, N_EXPERTS // ne),
        in_specs=[pl.BlockSpec((tm, d), lambda i, c: (i, 0)),
                  pl.BlockSpec((None, tm, ne), lambda i, c: (c, i, 0)),
                  pl.BlockSpec((None, ne, d, edim), lambda i, c: (layer, c, 0, 0)),
                  pl.BlockSpec((None, ne, d, edim), lambda i, c: (layer, c, 0, 0)),
                  pl.BlockSpec((None, ne, edim, d), lambda i, c: (layer, c, 0, 0))],
        out_specs=pl.BlockSpec((tm, d), lambda i, c: (i, 0)),
        out_shape=jax.ShapeDtypeStruct((t, d), F32),
        compiler_params=_cparams("parallel", "arbitrary"),
        name="moe_experts",
    )(xb, cw, wg, wu, wd)


def _tail_kernel(x_ref, xb_ref, y_ref, p_ref, sg_ref, su_ref, sd_ref, pg_ref, pp_ref,
                 lng_ref, lnb_ref, xo_ref, xbo_ref, *, alpha):
    xb = xb_ref[...]
    h = jax.nn.silu(_dot(xb, sg_ref[...])) * _dot(xb, su_ref[...])
    shared = _dot(h.astype(BF16), sd_ref[...])
    ple = jax.nn.sigmoid(_dot(xb, pg_ref[...])) * _dot(p_ref[...].astype(BF16), pp_ref[...])
    z = alpha * x_ref[...] + y_ref[...] + shared + ple
    y = _layer_norm(z, lng_ref[...], lnb_ref[...])
    xo_ref[...] = y
    xbo_ref[...] = y.astype(BF16)


def _tail(x, xb, y, p, sg, su, sd, pg, pp, lng, lnb, alpha):
    t, d = x.shape
    tm = _tile(t, 256)
    row = lambda w: pl.BlockSpec((tm, w), lambda i: (i, 0))
    full = lambda a: pl.BlockSpec(a.shape, lambda i: (0,) * a.ndim)
    return pl.pallas_call(
        functools.partial(_tail_kernel, alpha=alpha),
        grid=(t // tm,),
        in_specs=[row(d), row(d), row(d), row(p.shape[1]), full(sg), full(su), full(sd),
                  full(pg), full(pp), full(lng), full(lnb)],
        out_specs=[row(d), row(d)],
        out_shape=[jax.ShapeDtypeStruct((t, d), F32), jax.ShapeDtypeStruct((t, d), BF16)],
        compiler_params=_cparams("parallel"),
        name="ffn_tail_norm",
    )(x, xb, y, p, sg, su, sd, pg, pp, lng, lnb)


def _slot_weights(w):
    d = w.shape[0]
    o = IN_OFFS
    qs = HEAD_DIM ** -0.5

    def heads(off, n, width=HEAD_DIM, slot=LANE, scale=1.0):
        blk = w[:, off:off + n * width].reshape(d, n, width) * scale
        return jnp.pad(blk, ((0, 0), (0, 0), (0, slot - width))).reshape(d, n * slot)

    gates = w[:, o[7]:o[8]].reshape(d, NSA_HEADS, 3).transpose(0, 2, 1).reshape(d, 3 * NSA_HEADS)
    parts = [heads(o[0], NSA_HEADS, scale=qs), heads(o[8], SWA_HEADS, scale=qs),
             heads(o[11], 2 * DIFF_HEADS, scale=qs), heads(o[12], 2 * DIFF_HEADS),
             heads(o[13], DIFF_HEADS, 2 * HEAD_DIM, 2 * LANE),
             heads(o[3], NSA_KV), heads(o[4], NSA_KV), heads(o[5], NSA_KV), heads(o[6], NSA_KV),
             heads(o[9], SWA_KV), heads(o[10], SWA_KV),
             w[:, o[1]:o[2]], w[:, o[2]:o[3]],
             jnp.pad(gates, ((0, 0), (0, LANE - 3 * NSA_HEADS)))]
    used = sum(part.shape[1] for part in parts)
    parts.append(jnp.zeros((d, _LAYOUT.width - used), F32))
    out = jnp.concatenate(parts, axis=1)
    assert out.shape[1] == _LAYOUT.width
    return out.astype(BF16)


def _compressed_slots(c, pos_cols):
    b, nb, hkv, dh = c.shape
    c = c.transpose(0, 2, 1, 3)
    extra = jnp.zeros((b, hkv, nb, LANE - dh), F32)
    if pos_cols:
        cpos = np.arange(nb) * NSA_BLOCK + NSA_BLOCK - 1
        cols = np.zeros((nb, LANE - dh), np.float32)
        cols[:, 0] = cpos // LANE
        cols[:, 1] = cpos % LANE
        extra = extra + jnp.asarray(cols)
    c = jnp.concatenate([c, extra], axis=-1)
    c = jnp.pad(c, ((0, 0), (0, 0), (0, NBLK_PAD - nb), (0, 0)))
    return c.astype(BF16)


def kernel(x, p, w_in, nsa_pe_k, nsa_w1_k, nsa_w2_k, nsa_pe_v, nsa_w1_v, nsa_w2_v, swa_sinks,
           diff_lq1, diff_lk1, diff_lq2, diff_lk2, diff_norm_g, w_branch_a, w_branch_b, w_branch_c,
           w_out, ln1_g, ln1_b, router_w, router_bias, exp_w_gate, exp_w_up, exp_w_down,
           sh_w_gate, sh_w_up, sh_w_down, ple_w_proj, ple_w_gate, ln2_g, ln2_b):
    bsz, seq, d = x.shape
    depth = w_in.shape[0]
    t = bsz * seq
    alpha = (2 * depth) ** 0.25
    lay = _LAYOUT
    nb = seq // NSA_BLOCK
    assert nb <= NBLK_PAD and seq % LANE == 0
    n_sel = min(NSA_TOPN, nb)

    aux = jnp.asarray(lay.aux)
    blk_mask = jnp.asarray(
        (np.arange(seq)[:, None] // NSA_BLOCK == np.arange(NBLK_PAD)[None, :]) * MASK_NEG, BF16)
    ex = np.zeros((3, LANE, NSA_HEADS * HEAD_DIM), np.float32)
    for c in range(3):
        for h in range(NSA_HEADS):
            ex[c, c * NSA_HEADS + h, h * HEAD_DIM:(h + 1) * HEAD_DIM] = 1.0
    expand = jnp.asarray(ex, BF16)
    slopes8 = jnp.asarray(_slopes(NSA_HEADS), F32)

    wg_all = exp_w_gate.astype(BF16)
    wu_all = exp_w_up.astype(BF16)
    wd_all = exp_w_down.astype(BF16)
    xf = x.reshape(t, d)
    xb = xf.astype(BF16)
    for i in range(depth):
        w_slots = _slot_weights(w_in[i])
        proj2d = _project(xb, w_slots, aux, seq, BF16, PROJ_TN)
        proj = proj2d.reshape(bsz, seq, lay.width)

        def blocks(off):
            c = proj[:, :, off:off + LANE].reshape(bsz, nb, NSA_BLOCK, NSA_KV, HEAD_DIM)
            return c.transpose(0, 1, 3, 2, 4).reshape(bsz * nb * NSA_KV, NSA_BLOCK * HEAD_DIM)
        cmp = _compress(
            jnp.stack([blocks(lay.kc), blocks(lay.vc)]),
            jnp.stack([nsa_pe_k[i].reshape(1, -1), nsa_pe_v[i].reshape(1, -1)]),
            jnp.stack([nsa_w1_k[i], nsa_w1_v[i]]).astype(BF16),
            jnp.stack([nsa_w2_k[i], nsa_w2_v[i]]).astype(BF16))
        cmp = cmp.reshape(2, bsz, nb, NSA_KV, HEAD_DIM)
        kca = _compressed_slots(cmp[0], True)
        vca = _compressed_slots(cmp[1], False)

        o_cmp, selc, touch = _nsa_compressed(proj, kca, vca, seq, n_sel)
        ks = proj[:, :, lay.ks:lay.ks + NSA_KV * LANE].reshape(bsz, seq, NSA_KV, LANE).transpose(0, 2, 1, 3)
        ksel = jnp.concatenate(
            [ks, jnp.broadcast_to(blk_mask[None, None], (bsz, NSA_KV, seq, NBLK_PAD))], axis=-1)
        o_sel = _nsa_selected(proj, selc, touch, ksel, seq)
        hp_a = jnp.stack([slopes8, jnp.zeros_like(slopes8)])
        o_win = _banded(proj, hp_a, seq, NSA_WINDOW, lay.qa, lay.kw, lay.vw, False, F32, "nsa_window")
        hp_b = jnp.stack([slopes8, swa_sinks[i].astype(F32)])
        o_b = _banded(proj, hp_b, seq, SWA_WINDOW, lay.qb, lay.kb, lay.vb, True, BF16, "swa_sinks")

        lambda_init = 0.8 - 0.6 * math.exp(-0.3 * i)
        lam_vecs = jnp.stack([diff_lq1[i], diff_lk1[i], diff_lq2[i], diff_lk2[i]]).astype(F32)
        o_d = _diff_attention(proj, lam_vecs, diff_norm_g[i].reshape(1, -1), seq, lambda_init)

        ha = NSA_HEADS * HEAD_DIM
        xf, xb = _merge(
            xf, xb, o_cmp.reshape(t, ha), o_sel.reshape(t, ha), o_win.reshape(t, ha), proj2d,
            o_b.reshape(t, ha), o_d.reshape(t, ha), w_in[i][:, GM_OFF:].astype(BF16), expand,
            w_branch_a[i].astype(BF16), w_branch_b[i].astype(BF16), w_branch_c[i].astype(BF16),
            w_out[i].astype(BF16), ln1_g[i].reshape(1, d), ln1_b[i].reshape(1, d), alpha)

        cw_t = _router(xf, router_w[i].T, router_bias[i].reshape(-1, 1))
        y = _moe_dense(xb, cw_t, wg_all, wu_all, wd_all, i)
        xf, xb = _tail(xf, xb, y, p[i].reshape(t, -1),
                       sh_w_gate[i].astype(BF16), sh_w_up[i].astype(BF16), sh_w_down[i].astype(BF16),
                       ple_w_gate[i].astype(BF16), ple_w_proj[i].astype(BF16),
                       ln2_g[i].reshape(1, d), ln2_b[i].reshape(1, d), alpha)
    return xf.reshape(bsz, seq, d)
```

```python
import functools
import math

import numpy as np
import jax
import jax.numpy as jnp
from jax import lax
from jax.experimental import pallas as pl
from jax.experimental.pallas import tpu as pltpu

F32 = jnp.float32
BF16 = jnp.bfloat16

HEAD_DIM = 64
NSA_HEADS = 8
NSA_KV = 2
NSA_BLOCK = 64
NSA_TOPN = 16
NSA_LOCAL = 2
NSA_WINDOW = 512
SWA_HEADS = 8
SWA_KV = 2
SWA_WINDOW = 128
DIFF_HEADS = 4
N_EXPERTS = 64
TOP_K = 8
N_GROUPS = 8
TOPK_GROUPS = 4
ROUTED_SCALE = 2.5
LN_EPS = 1e-5
NEG = -1e30
FORCE = 1e4
MASK_NEG = -(2.0 ** 100)

LANE = 128
SUBLANE = 8
LANE_SHIFT = LANE.bit_length() - 1
BLOCK_SHIFT = NSA_BLOCK.bit_length() - 1
MXU_WIDTH = 256
PROJ_TN = 7 * MXU_WIDTH
GROUP = NSA_HEADS // NSA_KV
NBLK_PAD = 128
VMEM_LIMIT = 56 * 1024 * 1024

IN_SIZES = (NSA_HEADS * HEAD_DIM,) + (NSA_KV * HEAD_DIM,) * 6 + (NSA_HEADS * 3,) + \
    (SWA_HEADS * HEAD_DIM, SWA_KV * HEAD_DIM, SWA_KV * HEAD_DIM) + \
    (DIFF_HEADS * 2 * HEAD_DIM,) * 3
IN_OFFS = np.concatenate([[0], np.cumsum(IN_SIZES)]).tolist()
GM_OFF = IN_OFFS[-1]


def _slopes(n):
    return [2.0 ** (-8.0 * (h + 1) / n) for h in range(n)]


class _Layout:
    def __init__(self):
        src, scale, bias, pa, pb = [], [], [], [], []

        def slot(cols, sc=1.0, consts=(), pos=False, width=LANE):
            s = [-1] * width
            c = [0.0] * width
            a = [0.0] * width
            b = [0.0] * width
            s[:len(cols)] = cols
            for off, val in consts:
                c[off] = val
            if pos:
                a[HEAD_DIM] = 1.0
                b[HEAD_DIM + 1] = 1.0
            start = len(src)
            src.extend(s)
            scale.extend([sc] * width)
            bias.extend(c)
            pa.extend(a)
            pb.extend(b)
            return start

        def rng(base, n):
            return list(range(base, base + n))

        qs = HEAD_DIM ** -0.5
        o = IN_OFFS
        sl8 = _slopes(NSA_HEADS)
        sl4 = _slopes(DIFF_HEADS)
        self.qa = len(src)
        for h in range(NSA_HEADS):
            slot(rng(o[0] + h * HEAD_DIM, HEAD_DIM), qs,
                 [(HEAD_DIM, sl8[h] * LANE), (HEAD_DIM + 1, sl8[h])])
        self.qb = len(src)
        for h in range(SWA_HEADS):
            slot(rng(o[8] + h * HEAD_DIM, HEAD_DIM), qs,
                 [(HEAD_DIM, sl8[h] * LANE), (HEAD_DIM + 1, sl8[h])])
        self.qd = len(src)
        for h in range(DIFF_HEADS):
            for r in range(2):
                slot(rng(o[11] + (h * 2 + r) * HEAD_DIM, HEAD_DIM), qs,
                     [(HEAD_DIM, sl4[h] * LANE), (HEAD_DIM + 1, sl4[h])])
        self.kd = len(src)
        for h in range(DIFF_HEADS):
            for r in range(2):
                slot(rng(o[12] + (h * 2 + r) * HEAD_DIM, HEAD_DIM), pos=True)
        self.vd = len(src)
        for h in range(DIFF_HEADS):
            slot(rng(o[13] + h * 2 * HEAD_DIM, 2 * HEAD_DIM),
                 consts=[(2 * HEAD_DIM, 1.0)], width=2 * LANE)

        def kv_slots(k_src, v_src):
            k_off = len(src)
            for k in range(NSA_KV):
                slot(rng(k_src + k * HEAD_DIM, HEAD_DIM), pos=True)
            v_off = len(src)
            for k in range(NSA_KV):
                slot(rng(v_src + k * HEAD_DIM, HEAD_DIM), consts=[(HEAD_DIM, 1.0)])
            return k_off, v_off

        self.ks, self.vs = kv_slots(o[3], o[4])
        self.kw, self.vw = kv_slots(o[5], o[6])
        self.kb, self.vb = kv_slots(o[9], o[10])
        self.kc = slot(rng(o[1], NSA_KV * HEAD_DIM))
        self.vc = slot(rng(o[2], NSA_KV * HEAD_DIM))
        self.ga = slot([o[7] + h * 3 + c for c in range(3) for h in range(NSA_HEADS)])
        while len(src) % PROJ_TN:
            slot([])
        self.width = len(src)
        aux = np.zeros((SUBLANE, self.width), np.float32)
        aux[0] = bias
        aux[1] = pa
        aux[2] = pb
        self.aux = aux


_LAYOUT = _Layout()


def _cparams(*sem):
    return pltpu.CompilerParams(dimension_semantics=sem, vmem_limit_bytes=VMEM_LIMIT)


def _tile(n, pref):
    t = min(n, pref)
    assert n % t == 0, (n, pref)
    return t


def _iota(shape, dim):
    return lax.broadcasted_iota(jnp.int32, shape, dim)


def _dot(a, b):
    return jnp.dot(a, b, preferred_element_type=F32)


def _dot_nt(a, b):
    return lax.dot_general(a, b, (((1,), (1,)), ((), ())), preferred_element_type=F32)


def _proj_kernel(x_ref, w_ref, aux_ref, o_ref, *, seq, tm):
    acc = _dot(x_ref[...], w_ref[...])
    pos = (pl.program_id(0) * tm) % seq + _iota((tm, 1), 0)
    a = (pos >> LANE_SHIFT).astype(F32)
    b = (pos & (LANE - 1)).astype(F32)
    aux = aux_ref[...]
    o_ref[...] = (acc + aux[0:1] + a * aux[1:2] + b * aux[2:3]).astype(o_ref.dtype)


def _project(xb, w, aux, seq, out_dtype, tn_pref):
    t, d = xb.shape
    n = w.shape[1]
    tm = _tile(t, 1024)
    tn = _tile(n, tn_pref)
    return pl.pallas_call(
        functools.partial(_proj_kernel, seq=seq, tm=tm),
        grid=(t // tm, n // tn),
        in_specs=[pl.BlockSpec((tm, d), lambda i, j: (i, 0)),
                  pl.BlockSpec((d, tn), lambda i, j: (0, j)),
                  pl.BlockSpec((SUBLANE, tn), lambda i, j: (0, j))],
        out_specs=pl.BlockSpec((tm, tn), lambda i, j: (i, j)),
        out_shape=jax.ShapeDtypeStruct((t, n), out_dtype),
        compiler_params=_cparams("parallel", "parallel"),
        name="proj",
    )(xb, w, aux)


def _compress_kernel(x_ref, pe_ref, w1_ref, w2_ref, o_ref):
    xb = (x_ref[0].astype(F32) + pe_ref[0]).astype(BF16)
    h = jax.nn.gelu(_dot(xb, w1_ref[0]))
    o_ref[0] = _dot(h.astype(BF16), w2_ref[0])


def _compress(xs, pes, w1s, w2s):
    _, r, kdim = xs.shape
    hid = w1s.shape[2]
    dh = w2s.shape[2]
    return pl.pallas_call(
        _compress_kernel,
        grid=(2,),
        in_specs=[pl.BlockSpec((1, r, kdim), lambda i: (i, 0, 0)),
                  pl.BlockSpec((1, 1, kdim), lambda i: (i, 0, 0)),
                  pl.BlockSpec((1, kdim, hid), lambda i: (i, 0, 0)),
                  pl.BlockSpec((1, hid, dh), lambda i: (i, 0, 0))],
        out_specs=pl.BlockSpec((1, r, dh), lambda i: (i, 0, 0)),
        out_shape=jax.ShapeDtypeStruct((2, r, dh), F32),
        compiler_params=_cparams("parallel"),
        name="nsa_compress",
    )(xs, pes, w1s, w2s)


def _stack_heads(q, n):
    return jnp.concatenate([q[:, g * LANE:(g + 1) * LANE] for g in range(n)], axis=0)


def _unstack_heads(o, n, tq, width):
    return jnp.concatenate([o[g * tq:(g + 1) * tq, :width] for g in range(n)], axis=1)


def _cmp_kernel(q_ref, kc_ref, vc_ref, oc_ref, sel_ref, touch_ref, *, tq, n_sel):
    i = pl.program_id(2)
    qs = _stack_heads(q_ref[0], GROUP)
    s = _dot_nt(qs, kc_ref[0, 0])
    t1 = i * tq + _iota((tq, 1), 0)
    t = jnp.concatenate([t1] * GROUP, axis=0)
    n = _iota((1, NBLK_PAD), 1)
    valid = (n * NSA_BLOCK + (NSA_BLOCK - 1)) <= t
    s = jnp.where(valid, s, NEG)
    mx = jnp.max(s, axis=-1, keepdims=True)
    e = jnp.where(valid, jnp.exp(s - mx), 0.0)
    p = e / jnp.maximum(jnp.sum(e, axis=-1, keepdims=True), 1e-30)
    o = _dot(p.astype(BF16), vc_ref[0, 0])
    oc_ref[0] = _unstack_heads(o, GROUP, tq, HEAD_DIM)

    imp = p[0:tq]
    for g in range(1, GROUP):
        imp = imp + p[g * tq:(g + 1) * tq]
    nb = _iota((NBLK_PAD, 1), 0)
    rel = ((i * tq + _iota((1, tq), 1)) >> BLOCK_SHIFT) - nb
    forced = (nb == 0) | ((rel >= 0) & (rel < NSA_LOCAL))
    val = jnp.where(forced, FORCE, jnp.where(rel >= 0, imp.T, -1.0))
    sel = jnp.zeros((NBLK_PAD, tq), F32)
    nf = nb.astype(F32)
    for _ in range(n_sel):
        mx = jnp.max(val, axis=0, keepdims=True)
        idx = jnp.min(jnp.where(val == mx, nf, float(NBLK_PAD)), axis=0, keepdims=True)
        pick = nf == idx
        sel = jnp.where(pick & (mx >= 0.0), 1.0, sel)
        val = jnp.where(pick, -jnp.inf, val)
    sel = sel.T
    sel_ref[0, 0] = (1.0 - sel).astype(BF16)
    touch_ref[0, 0, 0] = jnp.broadcast_to(jnp.max(sel, axis=0, keepdims=True), (SUBLANE, NBLK_PAD))


def _nsa_compressed(proj, kca, vca, seq, n_sel):
    b = proj.shape[0]
    tq = _tile(seq, 256)
    lay = _LAYOUT
    gw = GROUP * LANE
    return pl.pallas_call(
        functools.partial(_cmp_kernel, tq=tq, n_sel=n_sel),
        grid=(b, NSA_KV, seq // tq),
        in_specs=[pl.BlockSpec((1, tq, gw), lambda b_, k, i: (b_, i, lay.qa // gw + k)),
                  pl.BlockSpec((1, 1, NBLK_PAD, LANE), lambda b_, k, i: (b_, k, 0, 0)),
                  pl.BlockSpec((1, 1, NBLK_PAD, LANE), lambda b_, k, i: (b_, k, 0, 0))],
        out_specs=[pl.BlockSpec((1, tq, GROUP * HEAD_DIM), lambda b_, k, i: (b_, i, k)),
                   pl.BlockSpec((1, 1, tq, NBLK_PAD), lambda b_, k, i: (b_, k, i, 0)),
                   pl.BlockSpec((1, 1, 1, SUBLANE, NBLK_PAD), lambda b_, k, i: (b_, k, i, 0, 0))],
        out_shape=[jax.ShapeDtypeStruct((b, seq, NSA_HEADS * HEAD_DIM), F32),
                   jax.ShapeDtypeStruct((b, NSA_KV, seq, NBLK_PAD), BF16),
                   jax.ShapeDtypeStruct((b, NSA_KV, seq // tq, SUBLANE, NBLK_PAD), F32)],
        compiler_params=_cparams("parallel", "parallel", "parallel"),
        name="nsa_compressed",
    )(proj, kca, vca)


def _flash(chunk_fns, v_ref, i, tq, tk, m_ref, acc_ref, s_refs, n_active=None, tile_at=None):
    m_ref[...] = jnp.full(m_ref.shape, NEG, F32)
    acc_ref[...] = jnp.zeros(acc_ref.shape, F32)
    acc_reps = acc_ref.shape[1] // LANE
    n_full = (i * tq) // tk
    if n_active is None:
        n_active = n_full
        tile_at = lambda n: n

    def nth_tile(n):
        return jnp.where(n < n_active, tile_at(n), n_full)

    def issue(n, s_ref):
        start = pl.multiple_of(nth_tile(n) * tk, tk)
        for c, fn in enumerate(chunk_fns):
            s_ref[c * tq:(c + 1) * tq, :] = fn(start)

    def accumulate(j, s_ref, visible):
        start = pl.multiple_of(j * tk, tk)
        vt = v_ref[pl.ds(start, tk), :]
        for c in range(len(chunk_fns)):
            rows = slice(c * tq, (c + 1) * tq)
            s = s_ref[rows, :]
            if visible is not None:
                s = jnp.where(visible, s, NEG)
            m_old = m_ref[rows, :]
            m_new = jnp.maximum(m_old, jnp.max(s, axis=-1, keepdims=True))
            p = jnp.exp((s - jnp.concatenate([m_new] * (tk // LANE), axis=1)).astype(BF16))
            alpha = jnp.concatenate([jnp.exp(m_old - m_new)] * acc_reps, axis=1)
            acc_ref[rows, :] = alpha * acc_ref[rows, :] + _dot(p, vt)
            m_ref[rows, :] = m_new

    s0, s1 = s_refs
    issue(0, s0)

    def body(pair, carry):
        n = 2 * pair
        issue(n + 1, s1)
        accumulate(nth_tile(n), s0, None)
        issue(n + 2, s0)
        accumulate(nth_tile(n + 1), s1, None)
        return carry

    pairs = n_active // 2
    lax.fori_loop(0, pairs, body, 0)
    visible = (_iota((tq, tk), 1) - _iota((tq, tk), 0)) <= i * tq - n_full * tk
    leftover = n_active - 2 * pairs

    @pl.when(leftover == 1)
    def _():
        issue(n_active, s1)
        accumulate(nth_tile(n_active - 1), s0, None)
        accumulate(n_full, s1, visible)

    @pl.when(leftover == 0)
    def _():
        accumulate(n_full, s0, visible)


def _sel_kernel(order_ref, count_ref, q_ref, sel_ref, k_ref, v_ref, o_ref, qa_ref, m_ref, acc_ref,
                s0_ref, s1_ref, *, tq, tk, n_tiles):
    i = pl.program_id(2)
    tile_id = (pl.program_id(0) * pl.num_programs(1) + pl.program_id(1)) * pl.num_programs(2) + i
    q = q_ref[0]
    sc = sel_ref[0, 0]
    for g in range(GROUP):
        qa_ref[g * tq:(g + 1) * tq, 0:LANE] = q[:, g * LANE:(g + 1) * LANE]
        qa_ref[g * tq:(g + 1) * tq, LANE:2 * LANE] = sc
    kv = k_ref.at[0, 0]

    def head_scores(g):
        return lambda start: _dot_nt(qa_ref[g * tq:(g + 1) * tq, :], kv[pl.ds(start, tk), :])

    _flash([head_scores(g) for g in range(GROUP)], v_ref.at[0], i, tq, tk, m_ref, acc_ref, (s0_ref, s1_ref),
           n_active=count_ref[tile_id], tile_at=lambda n: order_ref[tile_id * n_tiles + n])
    acc = acc_ref[...]
    o = acc / acc[:, HEAD_DIM:HEAD_DIM + 1]
    o_ref[0] = _unstack_heads(o, GROUP, tq, HEAD_DIM)


def _nsa_selected(proj, selc, touch, ksel, seq):
    b = proj.shape[0]
    tq = _tile(seq, 256)
    tk = _tile(seq, 256)
    n_tiles = seq // tk
    lay = _LAYOUT
    gw = GROUP * LANE
    per_tile = touch[:, :, :, 0, :seq // NSA_BLOCK].reshape(b, NSA_KV, seq // tq, n_tiles, tk // NSA_BLOCK)
    n_full = (np.arange(seq // tq) * tq) // tk
    active = (jnp.max(per_tile, axis=-1) > 0.0) & jnp.asarray(np.arange(n_tiles)[None, :] < n_full[:, None])
    order = jnp.argsort(jnp.where(active, 0, 1), axis=-1, stable=True).astype(jnp.int32).reshape(-1)
    count = jnp.sum(active, axis=-1).astype(jnp.int32).reshape(-1)
    grid_spec = pltpu.PrefetchScalarGridSpec(
        num_scalar_prefetch=2,
        grid=(b, NSA_KV, seq // tq),
        in_specs=[pl.BlockSpec((1, tq, gw), lambda b_, k, i, o, c: (b_, i, lay.qa // gw + k)),
                  pl.BlockSpec((1, 1, tq, NBLK_PAD), lambda b_, k, i, o, c: (b_, k, i, 0)),
                  pl.BlockSpec((1, 1, seq, 2 * LANE), lambda b_, k, i, o, c: (b_, k, 0, 0)),
                  pl.BlockSpec((1, seq, LANE), lambda b_, k, i, o, c: (b_, 0, lay.vs // LANE + k))],
        out_specs=pl.BlockSpec((1, tq, GROUP * HEAD_DIM), lambda b_, k, i, o, c: (b_, i, k)),
        scratch_shapes=[pltpu.VMEM((GROUP * tq, 2 * LANE), BF16),
                        pltpu.VMEM((GROUP * tq, LANE), F32),
                        pltpu.VMEM((GROUP * tq, LANE), F32),
                        pltpu.VMEM((GROUP * tq, tk), F32),
                        pltpu.VMEM((GROUP * tq, tk), F32)])
    return pl.pallas_call(
        functools.partial(_sel_kernel, tq=tq, tk=tk, n_tiles=n_tiles),
        grid_spec=grid_spec,
        out_shape=jax.ShapeDtypeStruct((b, seq, NSA_HEADS * HEAD_DIM), F32),
        compiler_params=_cparams("parallel", "parallel", "arbitrary"),
        name="nsa_selected",
    )(order, count, proj, selc, ksel, proj)


def _diff_kernel(lam_ref, q_ref, k_ref, v_ref, g_ref, o_ref, m_ref, acc_ref, s0_ref, s1_ref,
                 *, tq, tk, lambda_init):
    i = pl.program_id(2)
    kv = k_ref.at[0]

    def map_scores(r):
        cols = slice(r * LANE, (r + 1) * LANE)
        return lambda start: _dot_nt(q_ref[0, :, cols], kv[pl.ds(start, tk), cols])

    _flash([map_scores(0), map_scores(1)], v_ref.at[0], i, tq, tk, m_ref, acc_ref, (s0_ref, s1_ref))
    acc = acc_ref[...]
    dv = 2 * HEAD_DIM
    a = acc[:, :dv] / acc[:, dv:dv + 1]
    lv = lam_ref[...]
    lam = (jnp.exp(jnp.sum(lv[0:1] * lv[1:2], axis=-1, keepdims=True))
           - jnp.exp(jnp.sum(lv[2:3] * lv[3:4], axis=-1, keepdims=True)) + lambda_init)
    o = a[0:tq] - lam * a[tq:2 * tq]
    o = o * lax.rsqrt(jnp.mean(o * o, axis=-1, keepdims=True) + LN_EPS) * g_ref[...]
    o_ref[0] = (o * (1.0 - lambda_init)).astype(o_ref.dtype)


def _diff_attention(proj, lam_vecs, norm_g, seq, lambda_init):
    b = proj.shape[0]
    tq = _tile(seq, 1024)
    tk = _tile(seq, 1024)
    lay = _LAYOUT
    w2 = 2 * LANE
    return pl.pallas_call(
        functools.partial(_diff_kernel, tq=tq, tk=tk, lambda_init=lambda_init),
        grid=(b, DIFF_HEADS, seq // tq),
        in_specs=[pl.BlockSpec((4, HEAD_DIM), lambda b_, h, i: (0, 0)),
                  pl.BlockSpec((1, tq, w2), lambda b_, h, i: (b_, i, lay.qd // w2 + h)),
                  pl.BlockSpec((1, seq, w2), lambda b_, h, i: (b_, 0, lay.kd // w2 + h)),
                  pl.BlockSpec((1, seq, w2), lambda b_, h, i: (b_, 0, lay.vd // w2 + h)),
                  pl.BlockSpec((1, 2 * HEAD_DIM), lambda b_, h, i: (0, 0))],
        out_specs=pl.BlockSpec((1, tq, 2 * HEAD_DIM), lambda b_, h, i: (b_, i, h)),
        out_shape=jax.ShapeDtypeStruct((b, seq, DIFF_HEADS * 2 * HEAD_DIM), BF16),
        scratch_shapes=[pltpu.VMEM((2 * tq, LANE), F32),
                        pltpu.VMEM((2 * tq, w2), F32),
                        pltpu.VMEM((2 * tq, tk), F32),
                        pltpu.VMEM((2 * tq, tk), F32)],
        compiler_params=_cparams("parallel", "parallel", "arbitrary"),
        name="diff_attention",
    )(lam_vecs, proj, proj, proj, norm_g)


def _band_kernel(hp_ref, q_ref, k_ref, v_ref, o_ref, *, tq, window, band, use_sinks):
    kvh = pl.program_id(1)
    i = pl.program_id(2)
    start = pl.multiple_of(jnp.maximum(i * tq - window, 0), LANE)
    kb = k_ref[0, pl.ds(start, band), :]
    vb = v_ref[0, pl.ds(start, band), :]
    t1 = i * tq + _iota((tq, 1), 0)
    dist = t1 - (start + _iota((1, band), 1))
    visible = (dist >= 0) & (dist < window)
    outs = []
    scores = [_dot_nt(q_ref[0, :, g * LANE:(g + 1) * LANE], kb) for g in range(GROUP)]
    for g in range(GROUP):
        s = jnp.where(visible, scores[g], NEG)
        mx = jnp.max(s, axis=-1, keepdims=True)
        if use_sinks:
            h = kvh * GROUP + g
            sink = hp_ref[1, h] + hp_ref[0, h] * t1.astype(F32)
            mx = jnp.maximum(mx, sink)
        acc = _dot(jnp.exp((s - mx).astype(BF16)), vb)
        den = acc[:, HEAD_DIM:HEAD_DIM + 1]
        if use_sinks:
            den = den + jnp.exp(sink - mx)
        outs.append((acc / den)[:, :HEAD_DIM])
    o_ref[0] = jnp.concatenate(outs, axis=1).astype(o_ref.dtype)


def _banded(proj, head_params, seq, window, q_off, k_off, v_off, use_sinks, out_dtype, name):
    b = proj.shape[0]
    tq = _tile(seq, 256)
    band = min(window + tq, seq)
    gw = GROUP * LANE
    return pl.pallas_call(
        functools.partial(_band_kernel, tq=tq, window=window, band=band, use_sinks=use_sinks),
        grid=(b, NSA_KV, seq // tq),
        in_specs=[pl.BlockSpec(memory_space=pltpu.SMEM),
                  pl.BlockSpec((1, tq, gw), lambda b_, k, i: (b_, i, q_off // gw + k)),
                  pl.BlockSpec((1, seq, LANE), lambda b_, k, i: (b_, 0, k_off // LANE + k)),
                  pl.BlockSpec((1, seq, LANE), lambda b_, k, i: (b_, 0, v_off // LANE + k))],
        out_specs=pl.BlockSpec((1, tq, GROUP * HEAD_DIM), lambda b_, k, i: (b_, i, k)),
        out_shape=jax.ShapeDtypeStruct((b, seq, NSA_HEADS * HEAD_DIM), out_dtype),
        compiler_params=_cparams("parallel", "parallel", "parallel"),
        name=name,
    )(head_params, proj, proj, proj)


def _layer_norm(z, g, b):
    mu = jnp.mean(z, axis=-1, keepdims=True)
    zc = z - mu
    var = jnp.mean(zc * zc, axis=-1, keepdims=True)
    return zc * lax.rsqrt(var + LN_EPS) * g + b


def _merge_kernel(x_ref, xin_ref, oc_ref, os_ref, ow_ref, ga_ref, ob_ref, od_ref, wgm_ref, ex_ref,
                  wa_ref, wb_ref, wc_ref, wo_ref, lng_ref, lnb_ref, xo_ref, xb_ref, *, alpha):
    d = x_ref.shape[1]
    ga = jax.nn.sigmoid(ga_ref[...].astype(F32)).astype(BF16)
    oa = (_dot(ga, ex_ref[0]) * oc_ref[...] + _dot(ga, ex_ref[1]) * os_ref[...]
          + _dot(ga, ex_ref[2]) * ow_ref[...])
    gm = jax.nn.sigmoid(_dot(xin_ref[...], wgm_ref[...]))
    merged = (gm[:, 0:d] * _dot(oa.astype(BF16), wa_ref[...])
              + gm[:, d:2 * d] * _dot(ob_ref[...], wb_ref[...])
              + gm[:, 2 * d:3 * d] * _dot(od_ref[...], wc_ref[...]))
    z = alpha * x_ref[...] + _dot(merged.astype(BF16), wo_ref[...])
    y = _layer_norm(z, lng_ref[...], lnb_ref[...])
    xo_ref[...] = y
    xb_ref[...] = y.astype(BF16)


def _merge(x, xb, oc, os_, ow, proj2d, ob, od, wgm, expand, wa, wb, wc, wo, lng, lnb, alpha):
    t, d = x.shape
    tm = _tile(t, 512)
    ha = NSA_HEADS * HEAD_DIM
    row = lambda w: pl.BlockSpec((tm, w), lambda i: (i, 0))
    full = lambda a: pl.BlockSpec(a.shape, lambda i: (0,) * a.ndim)
    return pl.pallas_call(
        functools.partial(_merge_kernel, alpha=alpha),
        grid=(t // tm,),
        in_specs=[row(d), row(d), row(ha), row(ha), row(ha),
                  pl.BlockSpec((tm, LANE), lambda i: (i, _LAYOUT.ga // LANE)),
                  row(ha), row(ha), full(wgm), full(expand),
                  full(wa), full(wb), full(wc), full(wo), full(lng), full(lnb)],
        out_specs=[row(d), row(d)],
        out_shape=[jax.ShapeDtypeStruct((t, d), F32), jax.ShapeDtypeStruct((t, d), BF16)],
        compiler_params=_cparams("parallel"),
        name="merge_norm",
    )(x, xb, oc, os_, ow, proj2d, ob, od, wgm, expand, wa, wb, wc, wo, lng, lnb)


def _first_max(v, idx, n):
    mx = jnp.max(v, axis=0, keepdims=True)
    first = jnp.min(jnp.where(v == mx, idx, n), axis=0, keepdims=True)
    return mx, idx == first


def _router_kernel(x_ref, rw_ref, rb_ref, cw_ref):
    gsz = N_EXPERTS // N_GROUPS
    logits = lax.dot_general(rw_ref[...], x_ref[...], (((1,), (1,)), ((), ())),
                             precision=lax.Precision.HIGHEST,
                             preferred_element_type=F32)
    tm = logits.shape[1]
    scores = jax.nn.sigmoid(logits)
    biased = scores + rb_ref[...]
    iw = _iota((gsz, tm), 0)
    gs = []
    for g in range(N_GROUPS):
        blk = biased[g * gsz:(g + 1) * gsz]
        m1, pick = _first_max(blk, iw, gsz)
        m2 = jnp.max(jnp.where(pick, -jnp.inf, blk), axis=0, keepdims=True)
        gs.append(m1 + m2)
    gs = jnp.concatenate(gs, axis=0)
    ig = _iota((N_GROUPS, tm), 0)
    gmask = jnp.zeros((N_GROUPS, tm), F32)
    for _ in range(TOPK_GROUPS):
        _, pick = _first_max(gs, ig, N_GROUPS)
        gmask = jnp.where(pick, 1.0, gmask)
        gs = jnp.where(pick, -jnp.inf, gs)
    emask = jnp.concatenate(
        [jnp.broadcast_to(gmask[g:g + 1], (gsz, tm)) for g in range(N_GROUPS)], axis=0)
    cand = jnp.where(emask > 0.5, biased, NEG)
    ie = _iota((N_EXPERTS, tm), 0)
    chosen = jnp.zeros((N_EXPERTS, tm), F32)
    for _ in range(TOP_K):
        _, pick = _first_max(cand, ie, N_EXPERTS)
        chosen = jnp.where(pick, 1.0, chosen)
        cand = jnp.where(pick, -jnp.inf, cand)
    w = scores * chosen
    cw_ref[...] = w / jnp.sum(w, axis=0, keepdims=True) * ROUTED_SCALE


def _router(x, rw_t, rb):
    t, d = x.shape
    tm = _tile(t, 512)
    return pl.pallas_call(
        _router_kernel,
        grid=(t // tm,),
        in_specs=[pl.BlockSpec((tm, d), lambda i: (i, 0)),
                  pl.BlockSpec((N_EXPERTS, d), lambda i: (0, 0)),
                  pl.BlockSpec((N_EXPERTS, 1), lambda i: (0, 0))],
        out_specs=pl.BlockSpec((N_EXPERTS, tm), lambda i: (0, i)),
        out_shape=jax.ShapeDtypeStruct((N_EXPERTS, t), F32),
        compiler_params=_cparams("parallel"),
        name="router",
    )(x, rw_t, rb)


def _moe_kernel(x_ref, cw_ref, wg_ref, wu_ref, wd_ref, y_ref, *, ne):
    c = pl.program_id(1)
    x = x_ref[...]
    edim = wg_ref.shape[2]
    cw = cw_ref[...]
    hs = []
    for e in range(ne):
        h = jax.nn.silu(_dot(x, wg_ref[e])) * _dot(x, wu_ref[e]) * cw[:, e:e + 1]
        hs.append(h.astype(BF16))
    y = _dot(jnp.concatenate(hs, axis=1), wd_ref[...].reshape(ne * edim, wd_ref.shape[2]))

    @pl.when(c == 0)
    def _():
        y_ref[...] = y

    @pl.when(c != 0)
    def _():
        y_ref[...] += y


def _moe_dense(xb, cw_t, wg, wu, wd, layer):
    t, d = xb.shape
    edim = wg.shape[3]
    tm = _tile(t, 1024)
    ne = 8
    cw = cw_t.reshape(N_EXPERTS // ne, ne, t).transpose(0, 2, 1)
    return pl.pallas_call(
        functools.partial(_moe_kernel, ne=ne),
        grid=(t // tm, N_EXPERTS // ne),
        in_specs=[pl.BlockSpec((tm, d), lambda i, c: (i, 0)),
                  pl.BlockSpec((None, tm, ne), lambda i, c: (c, i, 0)),
                  pl.BlockSpec((None, ne, d, edim), lambda i, c: (layer, c, 0, 0)),
                  pl.BlockSpec((None, ne, d, edim), lambda i, c: (layer, c, 0, 0)),
                  pl.BlockSpec((None, ne, edim, d), lambda i, c: (layer, c, 0, 0))],
        out_specs=pl.BlockSpec((tm, d), lambda i, c: (i, 0)),
        out_shape=jax.ShapeDtypeStruct((t, d), F32),
        compiler_params=_cparams("parallel", "arbitrary"),
        name="moe_experts",
    )(xb, cw, wg, wu, wd)


def _tail_kernel(x_ref, xb_ref, y_ref, p_ref, sg_ref, su_ref, sd_ref, pg_ref, pp_ref,
                 lng_ref, lnb_ref, xo_ref, xbo_ref, *, alpha):
    xb = xb_ref[...]
    h = jax.nn.silu(_dot(xb, sg_ref[...])) * _dot(xb, su_ref[...])
    shared = _dot(h.astype(BF16), sd_ref[...])
    ple = jax.nn.sigmoid(_dot(xb, pg_ref[...])) * _dot(p_ref[...].astype(BF16), pp_ref[...])
    z = alpha * x_ref[...] + y_ref[...] + shared + ple
    y = _layer_norm(z, lng_ref[...], lnb_ref[...])
    xo_ref[...] = y
    xbo_ref[...] = y.astype(BF16)


def _tail(x, xb, y, p, sg, su, sd, pg, pp, lng, lnb, alpha):
    t, d = x.shape
    tm = _tile(t, 512)
    row = lambda w: pl.BlockSpec((tm, w), lambda i: (i, 0))
    full = lambda a: pl.BlockSpec(a.shape, lambda i: (0,) * a.ndim)
    return pl.pallas_call(
        functools.partial(_tail_kernel, alpha=alpha),
        grid=(t // tm,),
        in_specs=[row(d), row(d), row(d), row(p.shape[1]), full(sg), full(su), full(sd),
                  full(pg), full(pp), full(lng), full(lnb)],
        out_specs=[row(d), row(d)],
        out_shape=[jax.ShapeDtypeStruct((t, d), F32), jax.ShapeDtypeStruct((t, d), BF16)],
        compiler_params=_cparams("parallel"),
        name="ffn_tail_norm",
    )(x, xb, y, p, sg, su, sd, pg, pp, lng, lnb)


def _slot_weights(w):
    d = w.shape[0]
    o = IN_OFFS
    qs = HEAD_DIM ** -0.5

    def heads(off, n, width=HEAD_DIM, slot=LANE, scale=1.0):
        blk = w[:, off:off + n * width].reshape(d, n, width) * scale
        return jnp.pad(blk, ((0, 0), (0, 0), (0, slot - width))).reshape(d, n * slot)

    gates = w[:, o[7]:o[8]].reshape(d, NSA_HEADS, 3).transpose(0, 2, 1).reshape(d, 3 * NSA_HEADS)
    parts = [heads(o[0], NSA_HEADS, scale=qs), heads(o[8], SWA_HEADS, scale=qs),
             heads(o[11], 2 * DIFF_HEADS, scale=qs), heads(o[12], 2 * DIFF_HEADS),
             heads(o[13], DIFF_HEADS, 2 * HEAD_DIM, 2 * LANE),
             heads(o[3], NSA_KV), heads(o[4], NSA_KV), heads(o[5], NSA_KV), heads(o[6], NSA_KV),
             heads(o[9], SWA_KV), heads(o[10], SWA_KV),
             w[:, o[1]:o[2]], w[:, o[2]:o[3]],
             jnp.pad(gates, ((0, 0), (0, LANE - 3 * NSA_HEADS)))]
    used = sum(part.shape[1] for part in parts)
    parts.append(jnp.zeros((d, _LAYOUT.width - used), F32))
    out = jnp.concatenate(parts, axis=1)
    assert out.shape[1] == _LAYOUT.width
    return out.astype(BF16)


def _compressed_slots(c, pos_cols):
    b, nb, hkv, dh = c.shape
    c = c.transpose(0, 2, 1, 3)
    extra = jnp.zeros((b, hkv, nb, LANE - dh), F32)
    if pos_cols:
        cpos = np.arange(nb) * NSA_BLOCK + NSA_BLOCK - 1
        cols = np.zeros((nb, LANE - dh), np.float32)
        cols[:, 0] = cpos // LANE
        cols[:, 1] = cpos % LANE
        extra = extra + jnp.asarray(cols)
    c = jnp.concatenate([c, extra], axis=-1)
    c = jnp.pad(c, ((0, 0), (0, 0), (0, NBLK_PAD - nb), (0, 0)))
    return c.astype(BF16)


def kernel(x, p, w_in, nsa_pe_k, nsa_w1_k, nsa_w2_k, nsa_pe_v, nsa_w1_v, nsa_w2_v, swa_sinks,
           diff_lq1, diff_lk1, diff_lq2, diff_lk2, diff_norm_g, w_branch_a, w_branch_b, w_branch_c,
           w_out, ln1_g, ln1_b, router_w, router_bias, exp_w_gate, exp_w_up, exp_w_down,
           sh_w_gate, sh_w_up, sh_w_down, ple_w_proj, ple_w_gate, ln2_g, ln2_b):
    bsz, seq, d = x.shape
    depth = w_in.shape[0]
    t = bsz * seq
    alpha = (2 * depth) ** 0.25
    lay = _LAYOUT
    nb = seq // NSA_BLOCK
    assert nb <= NBLK_PAD and seq % LANE == 0
    n_sel = min(NSA_TOPN, nb)

    aux = jnp.asarray(lay.aux)
    blk_mask = jnp.asarray(
        (np.arange(seq)[:, None] // NSA_BLOCK == np.arange(NBLK_PAD)[None, :]) * MASK_NEG, BF16)
    ex = np.zeros((3, LANE, NSA_HEADS * HEAD_DIM), np.float32)
    for c in range(3):
        for h in range(NSA_HEADS):
            ex[c, c * NSA_HEADS + h, h * HEAD_DIM:(h + 1) * HEAD_DIM] = 1.0
    expand = jnp.asarray(ex, BF16)
    slopes8 = jnp.asarray(_slopes(NSA_HEADS), F32)

    wg_all = exp_w_gate.astype(BF16)
    wu_all = exp_w_up.astype(BF16)
    wd_all = exp_w_down.astype(BF16)
    xf = x.reshape(t, d)
    xb = xf.astype(BF16)
    for i in range(depth):
        w_slots = _slot_weights(w_in[i])
        proj2d = _project(xb, w_slots, aux, seq, BF16, PROJ_TN)
        proj = proj2d.reshape(bsz, seq, lay.width)

        def blocks(off):
            c = proj[:, :, off:off + LANE].reshape(bsz, nb, NSA_BLOCK, NSA_KV, HEAD_DIM)
            return c.transpose(0, 1, 3, 2, 4).reshape(bsz * nb * NSA_KV, NSA_BLOCK * HEAD_DIM)
        cmp = _compress(
            jnp.stack([blocks(lay.kc), blocks(lay.vc)]),
            jnp.stack([nsa_pe_k[i].reshape(1, -1), nsa_pe_v[i].reshape(1, -1)]),
            jnp.stack([nsa_w1_k[i], nsa_w1_v[i]]).astype(BF16),
            jnp.stack([nsa_w2_k[i], nsa_w2_v[i]]).astype(BF16))
        cmp = cmp.reshape(2, bsz, nb, NSA_KV, HEAD_DIM)
        kca = _compressed_slots(cmp[0], True)
        vca = _compressed_slots(cmp[1], False)

        o_cmp, selc, touch = _nsa_compressed(proj, kca, vca, seq, n_sel)
        ks = proj[:, :, lay.ks:lay.ks + NSA_KV * LANE].reshape(bsz, seq, NSA_KV, LANE).transpose(0, 2, 1, 3)
        ksel = jnp.concatenate(
            [ks, jnp.broadcast_to(blk_mask[None, None], (bsz, NSA_KV, seq, NBLK_PAD))], axis=-1)
        o_sel = _nsa_selected(proj, selc, touch, ksel, seq)
        hp_a = jnp.stack([slopes8, jnp.zeros_like(slopes8)])
        o_win = _banded(proj, hp_a, seq, NSA_WINDOW, lay.qa, lay.kw, lay.vw, False, F32, "nsa_window")
        hp_b = jnp.stack([slopes8, swa_sinks[i].astype(F32)])
        o_b = _banded(proj, hp_b, seq, SWA_WINDOW, lay.qb, lay.kb, lay.vb, True, BF16, "swa_sinks")

        lambda_init = 0.8 - 0.6 * math.exp(-0.3 * i)
        lam_vecs = jnp.stack([diff_lq1[i], diff_lk1[i], diff_lq2[i], diff_lk2[i]]).astype(F32)
        o_d = _diff_attention(proj, lam_vecs, diff_norm_g[i].reshape(1, -1), seq, lambda_init)

        ha = NSA_HEADS * HEAD_DIM
        xf, xb = _merge(
            xf, xb, o_cmp.reshape(t, ha), o_sel.reshape(t, ha), o_win.reshape(t, ha), proj2d,
            o_b.reshape(t, ha), o_d.reshape(t, ha), w_in[i][:, GM_OFF:].astype(BF16), expand,
            w_branch_a[i].astype(BF16), w_branch_b[i].astype(BF16), w_branch_c[i].astype(BF16),
            w_out[i].astype(BF16), ln1_g[i].reshape(1, d), ln1_b[i].reshape(1, d), alpha)

        cw_t = _router(xf, router_w[i].T, router_bias[i].reshape(-1, 1))
        y = _moe_dense(xb, cw_t, wg_all, wu_all, wd_all, i)
        xf, xb = _tail(xf, xb, y, p[i].reshape(t, -1),
                       sh_w_gate[i].astype(BF16), sh_w_up[i].astype(BF16), sh_w_down[i].astype(BF16),
                       ple_w_gate[i].astype(BF16), ple_w_proj[i].astype(BF16),
                       ln2_g[i].reshape(1, d), ln2_b[i].reshape(1, d), alpha)
    return xf.reshape(bsz, seq, d)
```

```python
import functools
import math

import numpy as np
import jax
import jax.numpy as jnp
from jax import lax
from jax.experimental import pallas as pl
from jax.experimental.pallas import tpu as pltpu

F32 = jnp.float32
BF16 = jnp.bfloat16

HEAD_DIM = 64
NSA_HEADS = 8
NSA_KV = 2
NSA_BLOCK = 64
NSA_TOPN = 16
NSA_LOCAL = 2
NSA_WINDOW = 512
SWA_HEADS = 8
SWA_KV = 2
SWA_WINDOW = 128
DIFF_HEADS = 4
N_EXPERTS = 64
TOP_K = 8
N_GROUPS = 8
TOPK_GROUPS = 4
ROUTED_SCALE = 2.5
LN_EPS = 1e-5
NEG = -1e30
FORCE = 1e4
MASK_NEG = -(2.0 ** 100)

LANE = 128
SUBLANE = 8
LANE_SHIFT = LANE.bit_length() - 1
BLOCK_SHIFT = NSA_BLOCK.bit_length() - 1
MXU_WIDTH = 256
PROJ_TN = 7 * MXU_WIDTH
GROUP = NSA_HEADS // NSA_KV
NBLK_PAD = 128
VMEM_LIMIT = 56 * 1024 * 1024

IN_SIZES = (NSA_HEADS * HEAD_DIM,) + (NSA_KV * HEAD_DIM,) * 6 + (NSA_HEADS * 3,) + \
    (SWA_HEADS * HEAD_DIM, SWA_KV * HEAD_DIM, SWA_KV * HEAD_DIM) + \
    (DIFF_HEADS * 2 * HEAD_DIM,) * 3
IN_OFFS = np.concatenate([[0], np.cumsum(IN_SIZES)]).tolist()
GM_OFF = IN_OFFS[-1]


def _slopes(n):
    return [2.0 ** (-8.0 * (h + 1) / n) for h in range(n)]


class _Layout:
    def __init__(self):
        src, scale, bias, pa, pb = [], [], [], [], []

        def slot(cols, sc=1.0, consts=(), pos=False, width=LANE):
            s = [-1] * width
            c = [0.0] * width
            a = [0.0] * width
            b = [0.0] * width
            s[:len(cols)] = cols
            for off, val in consts:
                c[off] = val
            if pos:
                a[HEAD_DIM] = 1.0
                b[HEAD_DIM + 1] = 1.0
            start = len(src)
            src.extend(s)
            scale.extend([sc] * width)
            bias.extend(c)
            pa.extend(a)
            pb.extend(b)
            return start

        def rng(base, n):
            return list(range(base, base + n))

        qs = HEAD_DIM ** -0.5
        o = IN_OFFS
        sl8 = _slopes(NSA_HEADS)
        sl4 = _slopes(DIFF_HEADS)
        self.qa = len(src)
        for h in range(NSA_HEADS):
            slot(rng(o[0] + h * HEAD_DIM, HEAD_DIM), qs,
                 [(HEAD_DIM, sl8[h] * LANE), (HEAD_DIM + 1, sl8[h])])
        self.qb = len(src)
        for h in range(SWA_HEADS):
            slot(rng(o[8] + h * HEAD_DIM, HEAD_DIM), qs,
                 [(HEAD_DIM, sl8[h] * LANE), (HEAD_DIM + 1, sl8[h])])
        self.qd = len(src)
        for h in range(DIFF_HEADS):
            for r in range(2):
                slot(rng(o[11] + (h * 2 + r) * HEAD_DIM, HEAD_DIM), qs,
                     [(HEAD_DIM, sl4[h] * LANE), (HEAD_DIM + 1, sl4[h])])
        self.kd = len(src)
        for h in range(DIFF_HEADS):
            for r in range(2):
                slot(rng(o[12] + (h * 2 + r) * HEAD_DIM, HEAD_DIM), pos=True)
        self.vd = len(src)
        for h in range(DIFF_HEADS):
            slot(rng(o[13] + h * 2 * HEAD_DIM, 2 * HEAD_DIM),
                 consts=[(2 * HEAD_DIM, 1.0)], width=2 * LANE)

        def kv_slots(k_src, v_src):
            k_off = len(src)
            for k in range(NSA_KV):
                slot(rng(k_src + k * HEAD_DIM, HEAD_DIM), pos=True)
            v_off = len(src)
            for k in range(NSA_KV):
                slot(rng(v_src + k * HEAD_DIM, HEAD_DIM), consts=[(HEAD_DIM, 1.0)])
            return k_off, v_off

        self.ks, self.vs = kv_slots(o[3], o[4])
        self.kw, self.vw = kv_slots(o[5], o[6])
        self.kb, self.vb = kv_slots(o[9], o[10])
        self.kc = slot(rng(o[1], NSA_KV * HEAD_DIM))
        self.vc = slot(rng(o[2], NSA_KV * HEAD_DIM))
        self.ga = slot([o[7] + h * 3 + c for c in range(3) for h in range(NSA_HEADS)])
        while len(src) % PROJ_TN:
            slot([])
        self.width = len(src)
        aux = np.zeros((SUBLANE, self.width), np.float32)
        aux[0] = bias
        aux[1] = pa
        aux[2] = pb
        self.aux = aux


_LAYOUT = _Layout()


def _cparams(*sem):
    return pltpu.CompilerParams(dimension_semantics=sem, vmem_limit_bytes=VMEM_LIMIT)


def _tile(n, pref):
    t = min(n, pref)
    assert n % t == 0, (n, pref)
    return t


def _iota(shape, dim):
    return lax.broadcasted_iota(jnp.int32, shape, dim)


def _dot(a, b):
    return jnp.dot(a, b, preferred_element_type=F32)


def _dot_nt(a, b):
    return lax.dot_general(a, b, (((1,), (1,)), ((), ())), preferred_element_type=F32)


def _proj_kernel(x_ref, w_ref, aux_ref, o_ref, *, seq, tm):
    acc = _dot(x_ref[...], w_ref[...])
    pos = (pl.program_id(0) * tm) % seq + _iota((tm, 1), 0)
    a = (pos >> LANE_SHIFT).astype(F32)
    b = (pos & (LANE - 1)).astype(F32)
    aux = aux_ref[...]
    o_ref[...] = (acc + aux[0:1] + a * aux[1:2] + b * aux[2:3]).astype(o_ref.dtype)


def _project(xb, w, aux, seq, out_dtype, tn_pref):
    t, d = xb.shape
    n = w.shape[1]
    tm = _tile(t, 1024)
    tn = _tile(n, tn_pref)
    return pl.pallas_call(
        functools.partial(_proj_kernel, seq=seq, tm=tm),
        grid=(t // tm, n // tn),
        in_specs=[pl.BlockSpec((tm, d), lambda i, j: (i, 0)),
                  pl.BlockSpec((d, tn), lambda i, j: (0, j)),
                  pl.BlockSpec((SUBLANE, tn), lambda i, j: (0, j))],
        out_specs=pl.BlockSpec((tm, tn), lambda i, j: (i, j)),
        out_shape=jax.ShapeDtypeStruct((t, n), out_dtype),
        compiler_params=_cparams("parallel", "parallel"),
        name="proj",
    )(xb, w, aux)


def _compress_kernel(x_ref, pe_ref, w1_ref, w2_ref, o_ref):
    xb = (x_ref[0].astype(F32) + pe_ref[0]).astype(BF16)
    h = jax.nn.gelu(_dot(xb, w1_ref[0]))
    o_ref[0] = _dot(h.astype(BF16), w2_ref[0])


def _compress(xs, pes, w1s, w2s):
    _, r, kdim = xs.shape
    hid = w1s.shape[2]
    dh = w2s.shape[2]
    return pl.pallas_call(
        _compress_kernel,
        grid=(2,),
        in_specs=[pl.BlockSpec((1, r, kdim), lambda i: (i, 0, 0)),
                  pl.BlockSpec((1, 1, kdim), lambda i: (i, 0, 0)),
                  pl.BlockSpec((1, kdim, hid), lambda i: (i, 0, 0)),
                  pl.BlockSpec((1, hid, dh), lambda i: (i, 0, 0))],
        out_specs=pl.BlockSpec((1, r, dh), lambda i: (i, 0, 0)),
        out_shape=jax.ShapeDtypeStruct((2, r, dh), F32),
        compiler_params=_cparams("parallel"),
        name="nsa_compress",
    )(xs, pes, w1s, w2s)


def _stack_heads(q, n):
    return jnp.concatenate([q[:, g * LANE:(g + 1) * LANE] for g in range(n)], axis=0)


def _unstack_heads(o, n, tq, width):
    return jnp.concatenate([o[g * tq:(g + 1) * tq, :width] for g in range(n)], axis=1)


def _cmp_kernel(q_ref, kc_ref, vc_ref, oc_ref, sel_ref, touch_ref, *, tq, n_sel):
    i = pl.program_id(2)
    qs = _stack_heads(q_ref[0], GROUP)
    s = _dot_nt(qs, kc_ref[0, 0])
    t1 = i * tq + _iota((tq, 1), 0)
    t = jnp.concatenate([t1] * GROUP, axis=0)
    n = _iota((1, NBLK_PAD), 1)
    valid = (n * NSA_BLOCK + (NSA_BLOCK - 1)) <= t
    s = jnp.where(valid, s, NEG)
    mx = jnp.max(s, axis=-1, keepdims=True)
    e = jnp.where(valid, jnp.exp(s - mx), 0.0)
    p = e / jnp.maximum(jnp.sum(e, axis=-1, keepdims=True), 1e-30)
    o = _dot(p.astype(BF16), vc_ref[0, 0])
    oc_ref[0] = _unstack_heads(o, GROUP, tq, HEAD_DIM)

    imp = p[0:tq]
    for g in range(1, GROUP):
        imp = imp + p[g * tq:(g + 1) * tq]
    nb = _iota((NBLK_PAD, 1), 0)
    rel = ((i * tq + _iota((1, tq), 1)) >> BLOCK_SHIFT) - nb
    forced = (nb == 0) | ((rel >= 0) & (rel < NSA_LOCAL))
    val = jnp.where(forced, FORCE, jnp.where(rel >= 0, imp.T, -1.0))
    sel = jnp.zeros((NBLK_PAD, tq), F32)
    nf = nb.astype(F32)
    for _ in range(n_sel):
        mx = jnp.max(val, axis=0, keepdims=True)
        idx = jnp.min(jnp.where(val == mx, nf, float(NBLK_PAD)), axis=0, keepdims=True)
        pick = nf == idx
        sel = jnp.where(pick & (mx >= 0.0), 1.0, sel)
        val = jnp.where(pick, -jnp.inf, val)
    sel = sel.T
    sel_ref[0, 0] = (1.0 - sel).astype(BF16)
    touch_ref[0, 0, 0] = jnp.broadcast_to(jnp.max(sel, axis=0, keepdims=True), (SUBLANE, NBLK_PAD))


def _nsa_compressed(proj, kca, vca, seq, n_sel):
    b = proj.shape[0]
    tq = _tile(seq, 256)
    lay = _LAYOUT
    gw = GROUP * LANE
    return pl.pallas_call(
        functools.partial(_cmp_kernel, tq=tq, n_sel=n_sel),
        grid=(b, NSA_KV, seq // tq),
        in_specs=[pl.BlockSpec((1, tq, gw), lambda b_, k, i: (b_, i, lay.qa // gw + k)),
                  pl.BlockSpec((1, 1, NBLK_PAD, LANE), lambda b_, k, i: (b_, k, 0, 0)),
                  pl.BlockSpec((1, 1, NBLK_PAD, LANE), lambda b_, k, i: (b_, k, 0, 0))],
        out_specs=[pl.BlockSpec((1, tq, GROUP * HEAD_DIM), lambda b_, k, i: (b_, i, k)),
                   pl.BlockSpec((1, 1, tq, NBLK_PAD), lambda b_, k, i: (b_, k, i, 0)),
                   pl.BlockSpec((1, 1, 1, SUBLANE, NBLK_PAD), lambda b_, k, i: (b_, k, i, 0, 0))],
        out_shape=[jax.ShapeDtypeStruct((b, seq, NSA_HEADS * HEAD_DIM), F32),
                   jax.ShapeDtypeStruct((b, NSA_KV, seq, NBLK_PAD), BF16),
                   jax.ShapeDtypeStruct((b, NSA_KV, seq // tq, SUBLANE, NBLK_PAD), F32)],
        compiler_params=_cparams("parallel", "parallel", "parallel"),
        name="nsa_compressed",
    )(proj, kca, vca)


def _flash(chunk_fns, v_ref, i, tq, tk, m_ref, acc_ref, s_refs, n_active=None, tile_at=None):
    m_ref[...] = jnp.full(m_ref.shape, NEG, F32)
    acc_ref[...] = jnp.zeros(acc_ref.shape, F32)
    acc_reps = acc_ref.shape[1] // LANE
    n_full = (i * tq) // tk
    if n_active is None:
        n_active = n_full
        tile_at = lambda n: n

    def nth_tile(n):
        return jnp.where(n < n_active, tile_at(n), n_full)

    def issue(n, s_ref):
        start = pl.multiple_of(nth_tile(n) * tk, tk)
        for c, fn in enumerate(chunk_fns):
            s_ref[c * tq:(c + 1) * tq, :] = fn(start)

    def accumulate(j, s_ref, visible):
        start = pl.multiple_of(j * tk, tk)
        vt = v_ref[pl.ds(start, tk), :]
        for c in range(len(chunk_fns)):
            rows = slice(c * tq, (c + 1) * tq)
            s = s_ref[rows, :]
            if visible is not None:
                s = jnp.where(visible, s, NEG)
            m_old = m_ref[rows, :]
            m_new = jnp.maximum(m_old, jnp.max(s, axis=-1, keepdims=True))
            p = jnp.exp((s - jnp.concatenate([m_new] * (tk // LANE), axis=1)).astype(BF16))
            alpha = jnp.concatenate([jnp.exp(m_old - m_new)] * acc_reps, axis=1)
            acc_ref[rows, :] = alpha * acc_ref[rows, :] + _dot(p, vt)
            m_ref[rows, :] = m_new

    s0, s1 = s_refs
    issue(0, s0)

    def body(pair, carry):
        n = 2 * pair
        issue(n + 1, s1)
        accumulate(nth_tile(n), s0, None)
        issue(n + 2, s0)
        accumulate(nth_tile(n + 1), s1, None)
        return carry

    pairs = n_active // 2
    lax.fori_loop(0, pairs, body, 0)
    visible = (_iota((tq, tk), 1) - _iota((tq, tk), 0)) <= i * tq - n_full * tk
    leftover = n_active - 2 * pairs

    @pl.when(leftover == 1)
    def _():
        issue(n_active, s1)
        accumulate(nth_tile(n_active - 1), s0, None)
        accumulate(n_full, s1, visible)

    @pl.when(leftover == 0)
    def _():
        accumulate(n_full, s0, visible)


def _sel_kernel(order_ref, count_ref, q_ref, sel_ref, k_ref, e_ref, v_ref, o_ref, qa_ref, m_ref, acc_ref,
                s0_ref, s1_ref, ke_ref, *, tq, tk, n_tiles):
    i = pl.program_id(2)

    @pl.when(i == 0)
    def _():
        ke_ref[:, 0:LANE] = k_ref[0]
        ke_ref[:, LANE:2 * LANE] = e_ref[...]

    tile_id = (pl.program_id(0) * pl.num_programs(1) + pl.program_id(1)) * pl.num_programs(2) + i
    q = q_ref[0]
    sc = sel_ref[0, 0]
    for g in range(GROUP):
        qa_ref[g * tq:(g + 1) * tq, 0:LANE] = q[:, g * LANE:(g + 1) * LANE]
        qa_ref[g * tq:(g + 1) * tq, LANE:2 * LANE] = sc
    kv = ke_ref

    def head_scores(g):
        return lambda start: _dot_nt(qa_ref[g * tq:(g + 1) * tq, :], kv[pl.ds(start, tk), :])

    _flash([head_scores(g) for g in range(GROUP)], v_ref.at[0], i, tq, tk, m_ref, acc_ref, (s0_ref, s1_ref),
           n_active=count_ref[tile_id], tile_at=lambda n: order_ref[tile_id * n_tiles + n])
    acc = acc_ref[...]
    o = acc / acc[:, HEAD_DIM:HEAD_DIM + 1]
    o_ref[0] = _unstack_heads(o, GROUP, tq, HEAD_DIM)


def _nsa_selected(proj, selc, touch, blk_mask, seq):
    b = proj.shape[0]
    tq = _tile(seq, 256)
    tk = _tile(seq, 256)
    n_tiles = seq // tk
    lay = _LAYOUT
    gw = GROUP * LANE
    per_tile = touch[:, :, :, 0, :seq // NSA_BLOCK].reshape(b, NSA_KV, seq // tq, n_tiles, tk // NSA_BLOCK)
    n_full = (np.arange(seq // tq) * tq) // tk
    active = (jnp.max(per_tile, axis=-1) > 0.0) & jnp.asarray(np.arange(n_tiles)[None, :] < n_full[:, None])
    order = jnp.argsort(jnp.where(active, 0, 1), axis=-1, stable=True).astype(jnp.int32).reshape(-1)
    count = jnp.sum(active, axis=-1).astype(jnp.int32).reshape(-1)
    grid_spec = pltpu.PrefetchScalarGridSpec(
        num_scalar_prefetch=2,
        grid=(b, NSA_KV, seq // tq),
        in_specs=[pl.BlockSpec((1, tq, gw), lambda b_, k, i, o, c: (b_, i, lay.qa // gw + k)),
                  pl.BlockSpec((1, 1, tq, NBLK_PAD), lambda b_, k, i, o, c: (b_, k, i, 0)),
                  pl.BlockSpec((1, seq, LANE), lambda b_, k, i, o, c: (b_, 0, lay.ks // LANE + k)),
                  pl.BlockSpec((seq, NBLK_PAD), lambda b_, k, i, o, c: (0, 0)),
                  pl.BlockSpec((1, seq, LANE), lambda b_, k, i, o, c: (b_, 0, lay.vs // LANE + k))],
        out_specs=pl.BlockSpec((1, tq, GROUP * HEAD_DIM), lambda b_, k, i, o, c: (b_, i, k)),
        scratch_shapes=[pltpu.VMEM((GROUP * tq, 2 * LANE), BF16),
                        pltpu.VMEM((GROUP * tq, LANE), F32),
                        pltpu.VMEM((GROUP * tq, LANE), F32),
                        pltpu.VMEM((GROUP * tq, tk), F32),
                        pltpu.VMEM((GROUP * tq, tk), F32),
                        pltpu.VMEM((seq, 2 * LANE), BF16)])
    return pl.pallas_call(
        functools.partial(_sel_kernel, tq=tq, tk=tk, n_tiles=n_tiles),
        grid_spec=grid_spec,
        out_shape=jax.ShapeDtypeStruct((b, seq, NSA_HEADS * HEAD_DIM), F32),
        compiler_params=_cparams("parallel", "parallel", "arbitrary"),
        name="nsa_selected",
    )(order, count, proj, selc, proj, blk_mask, proj)


def _diff_kernel(lam_ref, q_ref, k_ref, v_ref, g_ref, o_ref, m_ref, acc_ref, s0_ref, s1_ref,
                 *, tq, tk, lambda_init):
    i = pl.program_id(2)
    kv = k_ref.at[0]

    def map_scores(r):
        cols = slice(r * LANE, (r + 1) * LANE)
        return lambda start: _dot_nt(q_ref[0, :, cols], kv[pl.ds(start, tk), cols])

    _flash([map_scores(0), map_scores(1)], v_ref.at[0], i, tq, tk, m_ref, acc_ref, (s0_ref, s1_ref))
    acc = acc_ref[...]
    dv = 2 * HEAD_DIM
    a = acc[:, :dv] / acc[:, dv:dv + 1]
    lv = lam_ref[...]
    lam = (jnp.exp(jnp.sum(lv[0:1] * lv[1:2], axis=-1, keepdims=True))
           - jnp.exp(jnp.sum(lv[2:3] * lv[3:4], axis=-1, keepdims=True)) + lambda_init)
    o = a[0:tq] - lam * a[tq:2 * tq]
    o = o * lax.rsqrt(jnp.mean(o * o, axis=-1, keepdims=True) + LN_EPS) * g_ref[...]
    o_ref[0] = (o * (1.0 - lambda_init)).astype(o_ref.dtype)


def _diff_attention(proj, lam_vecs, norm_g, seq, lambda_init):
    b = proj.shape[0]
    tq = _tile(seq, 1024)
    tk = _tile(seq, 1024)
    lay = _LAYOUT
    w2 = 2 * LANE
    return pl.pallas_call(
        functools.partial(_diff_kernel, tq=tq, tk=tk, lambda_init=lambda_init),
        grid=(b, DIFF_HEADS, seq // tq),
        in_specs=[pl.BlockSpec((4, HEAD_DIM), lambda b_, h, i: (0, 0)),
                  pl.BlockSpec((1, tq, w2), lambda b_, h, i: (b_, i, lay.qd // w2 + h)),
                  pl.BlockSpec((1, seq, w2), lambda b_, h, i: (b_, 0, lay.kd // w2 + h)),
                  pl.BlockSpec((1, seq, w2), lambda b_, h, i: (b_, 0, lay.vd // w2 + h)),
                  pl.BlockSpec((1, 2 * HEAD_DIM), lambda b_, h, i: (0, 0))],
        out_specs=pl.BlockSpec((1, tq, 2 * HEAD_DIM), lambda b_, h, i: (b_, i, h)),
        out_shape=jax.ShapeDtypeStruct((b, seq, DIFF_HEADS * 2 * HEAD_DIM), BF16),
        scratch_shapes=[pltpu.VMEM((2 * tq, LANE), F32),
                        pltpu.VMEM((2 * tq, w2), F32),
                        pltpu.VMEM((2 * tq, tk), F32),
                        pltpu.VMEM((2 * tq, tk), F32)],
        compiler_params=_cparams("parallel", "parallel", "arbitrary"),
        name="diff_attention",
    )(lam_vecs, proj, proj, proj, norm_g)


def _band_kernel(hp_ref, q_ref, k_ref, v_ref, o_ref, *, tq, window, band, use_sinks):
    kvh = pl.program_id(1)
    i = pl.program_id(2)
    start = pl.multiple_of(jnp.maximum(i * tq - window, 0), LANE)
    kb = k_ref[0, pl.ds(start, band), :]
    vb = v_ref[0, pl.ds(start, band), :]
    t1 = i * tq + _iota((tq, 1), 0)
    dist = t1 - (start + _iota((1, band), 1))
    visible = (dist >= 0) & (dist < window)
    outs = []
    scores = [_dot_nt(q_ref[0, :, g * LANE:(g + 1) * LANE], kb) for g in range(GROUP)]
    for g in range(GROUP):
        s = jnp.where(visible, scores[g], NEG)
        mx = jnp.max(s, axis=-1, keepdims=True)
        if use_sinks:
            h = kvh * GROUP + g
            sink = hp_ref[1, h] + hp_ref[0, h] * t1.astype(F32)
            mx = jnp.maximum(mx, sink)
        acc = _dot(jnp.exp((s - mx).astype(BF16)), vb)
        den = acc[:, HEAD_DIM:HEAD_DIM + 1]
        if use_sinks:
            den = den + jnp.exp(sink - mx)
        outs.append((acc / den)[:, :HEAD_DIM])
    o_ref[0] = jnp.concatenate(outs, axis=1).astype(o_ref.dtype)


def _banded(proj, head_params, seq, window, q_off, k_off, v_off, use_sinks, out_dtype, name):
    b = proj.shape[0]
    tq = _tile(seq, 256)
    band = min(window + tq, seq)
    gw = GROUP * LANE
    return pl.pallas_call(
        functools.partial(_band_kernel, tq=tq, window=window, band=band, use_sinks=use_sinks),
        grid=(b, NSA_KV, seq // tq),
        in_specs=[pl.BlockSpec(memory_space=pltpu.SMEM),
                  pl.BlockSpec((1, tq, gw), lambda b_, k, i: (b_, i, q_off // gw + k)),
                  pl.BlockSpec((1, seq, LANE), lambda b_, k, i: (b_, 0, k_off // LANE + k)),
                  pl.BlockSpec((1, seq, LANE), lambda b_, k, i: (b_, 0, v_off // LANE + k))],
        out_specs=pl.BlockSpec((1, tq, GROUP * HEAD_DIM), lambda b_, k, i: (b_, i, k)),
        out_shape=jax.ShapeDtypeStruct((b, seq, NSA_HEADS * HEAD_DIM), out_dtype),
        compiler_params=_cparams("parallel", "parallel", "parallel"),
        name=name,
    )(head_params, proj, proj, proj)


def _layer_norm(z, g, b):
    mu = jnp.mean(z, axis=-1, keepdims=True)
    zc = z - mu
    var = jnp.mean(zc * zc, axis=-1, keepdims=True)
    return zc * lax.rsqrt(var + LN_EPS) * g + b


def _merge_kernel(x_ref, xin_ref, oc_ref, os_ref, ow_ref, ga_ref, ob_ref, od_ref, wgm_ref, ex_ref,
                  wa_ref, wb_ref, wc_ref, wo_ref, lng_ref, lnb_ref, xo_ref, xb_ref, *, alpha):
    d = x_ref.shape[1]
    ga = jax.nn.sigmoid(ga_ref[...].astype(F32)).astype(BF16)
    oa = (_dot(ga, ex_ref[0]) * oc_ref[...] + _dot(ga, ex_ref[1]) * os_ref[...]
          + _dot(ga, ex_ref[2]) * ow_ref[...])
    gm = jax.nn.sigmoid(_dot(xin_ref[...], wgm_ref[...]))
    merged = (gm[:, 0:d] * _dot(oa.astype(BF16), wa_ref[...])
              + gm[:, d:2 * d] * _dot(ob_ref[...], wb_ref[...])
              + gm[:, 2 * d:3 * d] * _dot(od_ref[...], wc_ref[...]))
    z = alpha * x_ref[...] + _dot(merged.astype(BF16), wo_ref[...])
    y = _layer_norm(z, lng_ref[...], lnb_ref[...])
    xo_ref[...] = y
    xb_ref[...] = y.astype(BF16)


def _merge(x, xb, oc, os_, ow, proj2d, ob, od, wgm, expand, wa, wb, wc, wo, lng, lnb, alpha):
    t, d = x.shape
    tm = _tile(t, 512)
    ha = NSA_HEADS * HEAD_DIM
    row = lambda w: pl.BlockSpec((tm, w), lambda i: (i, 0))
    full = lambda a: pl.BlockSpec(a.shape, lambda i: (0,) * a.ndim)
    return pl.pallas_call(
        functools.partial(_merge_kernel, alpha=alpha),
        grid=(t // tm,),
        in_specs=[row(d), row(d), row(ha), row(ha), row(ha),
                  pl.BlockSpec((tm, LANE), lambda i: (i, _LAYOUT.ga // LANE)),
                  row(ha), row(ha), full(wgm), full(expand),
                  full(wa), full(wb), full(wc), full(wo), full(lng), full(lnb)],
        out_specs=[row(d), row(d)],
        out_shape=[jax.ShapeDtypeStruct((t, d), F32), jax.ShapeDtypeStruct((t, d), BF16)],
        compiler_params=_cparams("parallel"),
        name="merge_norm",
    )(x, xb, oc, os_, ow, proj2d, ob, od, wgm, expand, wa, wb, wc, wo, lng, lnb)


def _first_max(v, idx, n):
    mx = jnp.max(v, axis=0, keepdims=True)
    first = jnp.min(jnp.where(v == mx, idx, n), axis=0, keepdims=True)
    return mx, idx == first


def _router_kernel(x_ref, rw_ref, rb_ref, cw_ref):
    gsz = N_EXPERTS // N_GROUPS
    logits = lax.dot_general(rw_ref[...], x_ref[...], (((1,), (1,)), ((), ())),
                             precision=lax.Precision.HIGHEST,
                             preferred_element_type=F32)
    tm = logits.shape[1]
    scores = jax.nn.sigmoid(logits)
    biased = scores + rb_ref[...]
    iw = _iota((gsz, tm), 0)
    gs = []
    for g in range(N_GROUPS):
        blk = biased[g * gsz:(g + 1) * gsz]
        m1, pick = _first_max(blk, iw, gsz)
        m2 = jnp.max(jnp.where(pick, -jnp.inf, blk), axis=0, keepdims=True)
        gs.append(m1 + m2)
    gs = jnp.concatenate(gs, axis=0)
    ig = _iota((N_GROUPS, tm), 0)
    gmask = jnp.zeros((N_GROUPS, tm), F32)
    for _ in range(TOPK_GROUPS):
        _, pick = _first_max(gs, ig, N_GROUPS)
        gmask = jnp.where(pick, 1.0, gmask)
        gs = jnp.where(pick, -jnp.inf, gs)
    emask = jnp.concatenate(
        [jnp.broadcast_to(gmask[g:g + 1], (gsz, tm)) for g in range(N_GROUPS)], axis=0)
    cand = jnp.where(emask > 0.5, biased, NEG)
    ie = _iota((N_EXPERTS, tm), 0)
    chosen = jnp.zeros((N_EXPERTS, tm), F32)
    for _ in range(TOP_K):
        _, pick = _first_max(cand, ie, N_EXPERTS)
        chosen = jnp.where(pick, 1.0, chosen)
        cand = jnp.where(pick, -jnp.inf, cand)
    w = scores * chosen
    cw_ref[...] = w / jnp.sum(w, axis=0, keepdims=True) * ROUTED_SCALE


def _router(x, rw_t, rb):
    t, d = x.shape
    tm = _tile(t, 512)
    return pl.pallas_call(
        _router_kernel,
        grid=(t // tm,),
        in_specs=[pl.BlockSpec((tm, d), lambda i: (i, 0)),
                  pl.BlockSpec((N_EXPERTS, d), lambda i: (0, 0)),
                  pl.BlockSpec((N_EXPERTS, 1), lambda i: (0, 0))],
        out_specs=pl.BlockSpec((N_EXPERTS, tm), lambda i: (0, i)),
        out_shape=jax.ShapeDtypeStruct((N_EXPERTS, t), F32),
        compiler_params=_cparams("parallel"),
        name="router",
    )(x, rw_t, rb)


def _moe_kernel(x_ref, cw_ref, wg_ref, wu_ref, wd_ref, y_ref, *, ne):
    c = pl.program_id(1)
    x = x_ref[...]
    edim = wg_ref.shape[2]
    cw = cw_ref[...]
    hs = []
    for e in range(ne):
        h = jax.nn.silu(_dot(x, wg_ref[e])) * _dot(x, wu_ref[e]) * cw[:, e:e + 1]
        hs.append(h.astype(BF16))
    y = _dot(jnp.concatenate(hs, axis=1), wd_ref[...].reshape(ne * edim, wd_ref.shape[2]))

    @pl.when(c == 0)
    def _():
        y_ref[...] = y

    @pl.when(c != 0)
    def _():
        y_ref[...] += y


def _moe_dense(xb, cw_t, wg, wu, wd, layer):
    t, d = xb.shape
    edim = wg.shape[3]
    tm = _tile(t, 1024)
    ne = 8
    cw = cw_t.reshape(N_EXPERTS // ne, ne, t).transpose(0, 2, 1)
    return pl.pallas_call(
        functools.partial(_moe_kernel, ne=ne),
        grid=(t // tm, N_EXPERTS // ne),
        in_specs=[pl.BlockSpec((tm, d), lambda i, c: (i, 0)),
                  pl.BlockSpec((None, tm, ne), lambda i, c: (c, i, 0)),
                  pl.BlockSpec((None, ne, d, edim), lambda i, c: (layer, c, 0, 0)),
                  pl.BlockSpec((None, ne, d, edim), lambda i, c: (layer, c, 0, 0)),
                  pl.BlockSpec((None, ne, edim, d), lambda i, c: (layer, c, 0, 0))],
        out_specs=pl.BlockSpec((tm, d), lambda i, c: (i, 0)),
        out_shape=jax.ShapeDtypeStruct((t, d), F32),
        compiler_params=_cparams("parallel", "arbitrary"),
        name="moe_experts",
    )(xb, cw, wg, wu, wd)


def _tail_kernel(x_ref, xb_ref, y_ref, p_ref, sg_ref, su_ref, sd_ref, pg_ref, pp_ref,
                 lng_ref, lnb_ref, xo_ref, xbo_ref, *, alpha):
    xb = xb_ref[...]
    h = jax.nn.silu(_dot(xb, sg_ref[...])) * _dot(xb, su_ref[...])
    shared = _dot(h.astype(BF16), sd_ref[...])
    ple = jax.nn.sigmoid(_dot(xb, pg_ref[...])) * _dot(p_ref[...].astype(BF16), pp_ref[...])
    z = alpha * x_ref[...] + y_ref[...] + shared + ple
    y = _layer_norm(z, lng_ref[...], lnb_ref[...])
    xo_ref[...] = y
    xbo_ref[...] = y.astype(BF16)


def _tail(x, xb, y, p, sg, su, sd, pg, pp, lng, lnb, alpha):
    t, d = x.shape
    tm = _tile(t, 512)
    row = lambda w: pl.BlockSpec((tm, w), lambda i: (i, 0))
    full = lambda a: pl.BlockSpec(a.shape, lambda i: (0,) * a.ndim)
    return pl.pallas_call(
        functools.partial(_tail_kernel, alpha=alpha),
        grid=(t // tm,),
        in_specs=[row(d), row(d), row(d), row(p.shape[1]), full(sg), full(su), full(sd),
                  full(pg), full(pp), full(lng), full(lnb)],
        out_specs=[row(d), row(d)],
        out_shape=[jax.ShapeDtypeStruct((t, d), F32), jax.ShapeDtypeStruct((t, d), BF16)],
        compiler_params=_cparams("parallel"),
        name="ffn_tail_norm",
    )(x, xb, y, p, sg, su, sd, pg, pp, lng, lnb)


def _slot_weights(w):
    d = w.shape[0]
    o = IN_OFFS
    qs = HEAD_DIM ** -0.5

    def heads(off, n, width=HEAD_DIM, slot=LANE, scale=1.0):
        blk = w[:, off:off + n * width].reshape(d, n, width) * scale
        return jnp.pad(blk, ((0, 0), (0, 0), (0, slot - width))).reshape(d, n * slot)

    gates = w[:, o[7]:o[8]].reshape(d, NSA_HEADS, 3).transpose(0, 2, 1).reshape(d, 3 * NSA_HEADS)
    parts = [heads(o[0], NSA_HEADS, scale=qs), heads(o[8], SWA_HEADS, scale=qs),
             heads(o[11], 2 * DIFF_HEADS, scale=qs), heads(o[12], 2 * DIFF_HEADS),
             heads(o[13], DIFF_HEADS, 2 * HEAD_DIM, 2 * LANE),
             heads(o[3], NSA_KV), heads(o[4], NSA_KV), heads(o[5], NSA_KV), heads(o[6], NSA_KV),
             heads(o[9], SWA_KV), heads(o[10], SWA_KV),
             w[:, o[1]:o[2]], w[:, o[2]:o[3]],
             jnp.pad(gates, ((0, 0), (0, LANE - 3 * NSA_HEADS)))]
    used = sum(part.shape[1] for part in parts)
    parts.append(jnp.zeros((d, _LAYOUT.width - used), F32))
    out = jnp.concatenate(parts, axis=1)
    assert out.shape[1] == _LAYOUT.width
    return out.astype(BF16)


def _compressed_slots(c, pos_cols):
    b, nb, hkv, dh = c.shape
    c = c.transpose(0, 2, 1, 3)
    extra = jnp.zeros((b, hkv, nb, LANE - dh), F32)
    if pos_cols:
        cpos = np.arange(nb) * NSA_BLOCK + NSA_BLOCK - 1
        cols = np.zeros((nb, LANE - dh), np.float32)
        cols[:, 0] = cpos // LANE
        cols[:, 1] = cpos % LANE
        extra = extra + jnp.asarray(cols)
    c = jnp.concatenate([c, extra], axis=-1)
    c = jnp.pad(c, ((0, 0), (0, 0), (0, NBLK_PAD - nb), (0, 0)))
    return c.astype(BF16)


def kernel(x, p, w_in, nsa_pe_k, nsa_w1_k, nsa_w2_k, nsa_pe_v, nsa_w1_v, nsa_w2_v, swa_sinks,
           diff_lq1, diff_lk1, diff_lq2, diff_lk2, diff_norm_g, w_branch_a, w_branch_b, w_branch_c,
           w_out, ln1_g, ln1_b, router_w, router_bias, exp_w_gate, exp_w_up, exp_w_down,
           sh_w_gate, sh_w_up, sh_w_down, ple_w_proj, ple_w_gate, ln2_g, ln2_b):
    bsz, seq, d = x.shape
    depth = w_in.shape[0]
    t = bsz * seq
    alpha = (2 * depth) ** 0.25
    lay = _LAYOUT
    nb = seq // NSA_BLOCK
    assert nb <= NBLK_PAD and seq % LANE == 0
    n_sel = min(NSA_TOPN, nb)

    aux = jnp.asarray(lay.aux)
    blk_mask = jnp.asarray(
        (np.arange(seq)[:, None] // NSA_BLOCK == np.arange(NBLK_PAD)[None, :]) * MASK_NEG, BF16)
    ex = np.zeros((3, LANE, NSA_HEADS * HEAD_DIM), np.float32)
    for c in range(3):
        for h in range(NSA_HEADS):
            ex[c, c * NSA_HEADS + h, h * HEAD_DIM:(h + 1) * HEAD_DIM] = 1.0
    expand = jnp.asarray(ex, BF16)
    slopes8 = jnp.asarray(_slopes(NSA_HEADS), F32)

    wg_all = exp_w_gate.astype(BF16)
    wu_all = exp_w_up.astype(BF16)
    wd_all = exp_w_down.astype(BF16)
    xf = x.reshape(t, d)
    xb = xf.astype(BF16)
    for i in range(depth):
        w_slots = _slot_weights(w_in[i])
        proj2d = _project(xb, w_slots, aux, seq, BF16, PROJ_TN)
        proj = proj2d.reshape(bsz, seq, lay.width)

        def blocks(off):
            c = proj[:, :, off:off + LANE].reshape(bsz, nb, NSA_BLOCK, NSA_KV, HEAD_DIM)
            return c.transpose(0, 1, 3, 2, 4).reshape(bsz * nb * NSA_KV, NSA_BLOCK * HEAD_DIM)
        cmp = _compress(
            jnp.stack([blocks(lay.kc), blocks(lay.vc)]),
            jnp.stack([nsa_pe_k[i].reshape(1, -1), nsa_pe_v[i].reshape(1, -1)]),
            jnp.stack([nsa_w1_k[i], nsa_w1_v[i]]).astype(BF16),
            jnp.stack([nsa_w2_k[i], nsa_w2_v[i]]).astype(BF16))
        cmp = cmp.reshape(2, bsz, nb, NSA_KV, HEAD_DIM)
        kca = _compressed_slots(cmp[0], True)
        vca = _compressed_slots(cmp[1], False)

        o_cmp, selc, touch = _nsa_compressed(proj, kca, vca, seq, n_sel)
        o_sel = _nsa_selected(proj, selc, touch, blk_mask, seq)
        hp_a = jnp.stack([slopes8, jnp.zeros_like(slopes8)])
        o_win = _banded(proj, hp_a, seq, NSA_WINDOW, lay.qa, lay.kw, lay.vw, False, F32, "nsa_window")
        hp_b = jnp.stack([slopes8, swa_sinks[i].astype(F32)])
        o_b = _banded(proj, hp_b, seq, SWA_WINDOW, lay.qb, lay.kb, lay.vb, True, BF16, "swa_sinks")

        lambda_init = 0.8 - 0.6 * math.exp(-0.3 * i)
        lam_vecs = jnp.stack([diff_lq1[i], diff_lk1[i], diff_lq2[i], diff_lk2[i]]).astype(F32)
        o_d = _diff_attention(proj, lam_vecs, diff_norm_g[i].reshape(1, -1), seq, lambda_init)

        ha = NSA_HEADS * HEAD_DIM
        xf, xb = _merge(
            xf, xb, o_cmp.reshape(t, ha), o_sel.reshape(t, ha), o_win.reshape(t, ha), proj2d,
            o_b.reshape(t, ha), o_d.reshape(t, ha), w_in[i][:, GM_OFF:].astype(BF16), expand,
            w_branch_a[i].astype(BF16), w_branch_b[i].astype(BF16), w_branch_c[i].astype(BF16),
            w_out[i].astype(BF16), ln1_g[i].reshape(1, d), ln1_b[i].reshape(1, d), alpha)

        cw_t = _router(xf, router_w[i].T, router_bias[i].reshape(-1, 1))
        y = _moe_dense(xb, cw_t, wg_all, wu_all, wd_all, i)
        xf, xb = _tail(xf, xb, y, p[i].reshape(t, -1),
                       sh_w_gate[i].astype(BF16), sh_w_up[i].astype(BF16), sh_w_down[i].astype(BF16),
                       ple_w_gate[i].astype(BF16), ple_w_proj[i].astype(BF16),
                       ln2_g[i].reshape(1, d), ln2_b[i].reshape(1, d), alpha)
    return xf.reshape(bsz, seq, d)
```

```python
import functools
import math

import numpy as np
import jax
import jax.numpy as jnp
from jax import lax
from jax.experimental import pallas as pl
from jax.experimental.pallas import tpu as pltpu

F32 = jnp.float32
BF16 = jnp.bfloat16

HEAD_DIM = 64
NSA_HEADS = 8
NSA_KV = 2
NSA_BLOCK = 64
NSA_TOPN = 16
NSA_LOCAL = 2
NSA_WINDOW = 512
SWA_HEADS = 8
SWA_KV = 2
SWA_WINDOW = 128
DIFF_HEADS = 4
N_EXPERTS = 64
TOP_K = 8
N_GROUPS = 8
TOPK_GROUPS = 4
ROUTED_SCALE = 2.5
LN_EPS = 1e-5
NEG = -1e30
FORCE = 1e4
MASK_NEG = -(2.0 ** 100)

LANE = 128
SUBLANE = 8
LANE_SHIFT = LANE.bit_length() - 1
BLOCK_SHIFT = NSA_BLOCK.bit_length() - 1
MXU_WIDTH = 256
PROJ_TN = 7 * MXU_WIDTH
GROUP = NSA_HEADS // NSA_KV
NBLK_PAD = 128
VMEM_LIMIT = 56 * 1024 * 1024

IN_SIZES = (NSA_HEADS * HEAD_DIM,) + (NSA_KV * HEAD_DIM,) * 6 + (NSA_HEADS * 3,) + \
    (SWA_HEADS * HEAD_DIM, SWA_KV * HEAD_DIM, SWA_KV * HEAD_DIM) + \
    (DIFF_HEADS * 2 * HEAD_DIM,) * 3
IN_OFFS = np.concatenate([[0], np.cumsum(IN_SIZES)]).tolist()
GM_OFF = IN_OFFS[-1]


def _slopes(n):
    return [2.0 ** (-8.0 * (h + 1) / n) for h in range(n)]


class _Layout:
    def __init__(self):
        src, scale, bias, pa, pb = [], [], [], [], []

        def slot(cols, sc=1.0, consts=(), pos=False, width=LANE):
            s = [-1] * width
            c = [0.0] * width
            a = [0.0] * width
            b = [0.0] * width
            s[:len(cols)] = cols
            for off, val in consts:
                c[off] = val
            if pos:
                a[HEAD_DIM] = 1.0
                b[HEAD_DIM + 1] = 1.0
            start = len(src)
            src.extend(s)
            scale.extend([sc] * width)
            bias.extend(c)
            pa.extend(a)
            pb.extend(b)
            return start

        def rng(base, n):
            return list(range(base, base + n))

        qs = HEAD_DIM ** -0.5
        o = IN_OFFS
        sl8 = _slopes(NSA_HEADS)
        sl4 = _slopes(DIFF_HEADS)
        self.qa = len(src)
        for h in range(NSA_HEADS):
            slot(rng(o[0] + h * HEAD_DIM, HEAD_DIM), qs,
                 [(HEAD_DIM, sl8[h] * LANE), (HEAD_DIM + 1, sl8[h])])
        self.qb = len(src)
        for h in range(SWA_HEADS):
            slot(rng(o[8] + h * HEAD_DIM, HEAD_DIM), qs,
                 [(HEAD_DIM, sl8[h] * LANE), (HEAD_DIM + 1, sl8[h])])
        self.qd = len(src)
        for h in range(DIFF_HEADS):
            for r in range(2):
                slot(rng(o[11] + (h * 2 + r) * HEAD_DIM, HEAD_DIM), qs,
                     [(HEAD_DIM, sl4[h] * LANE), (HEAD_DIM + 1, sl4[h])])
        self.kd = len(src)
        for h in range(DIFF_HEADS):
            for r in range(2):
                slot(rng(o[12] + (h * 2 + r) * HEAD_DIM, HEAD_DIM), pos=True)
        self.vd = len(src)
        for h in range(DIFF_HEADS):
            slot(rng(o[13] + h * 2 * HEAD_DIM, 2 * HEAD_DIM),
                 consts=[(2 * HEAD_DIM, 1.0)], width=2 * LANE)

        def kv_slots(k_src, v_src):
            k_off = len(src)
            for k in range(NSA_KV):
                slot(rng(k_src + k * HEAD_DIM, HEAD_DIM), pos=True)
            v_off = len(src)
            for k in range(NSA_KV):
                slot(rng(v_src + k * HEAD_DIM, HEAD_DIM), consts=[(HEAD_DIM, 1.0)])
            return k_off, v_off

        self.ks, self.vs = kv_slots(o[3], o[4])
        self.kw, self.vw = kv_slots(o[5], o[6])
        self.kb, self.vb = kv_slots(o[9], o[10])
        self.kc = slot(rng(o[1], NSA_KV * HEAD_DIM))
        self.vc = slot(rng(o[2], NSA_KV * HEAD_DIM))
        self.ga = slot([o[7] + h * 3 + c for c in range(3) for h in range(NSA_HEADS)])
        while len(src) % PROJ_TN:
            slot([])
        self.width = len(src)
        aux = np.zeros((SUBLANE, self.width), np.float32)
        aux[0] = bias
        aux[1] = pa
        aux[2] = pb
        self.aux = aux


_LAYOUT = _Layout()


def _cparams(*sem):
    return pltpu.CompilerParams(dimension_semantics=sem, vmem_limit_bytes=VMEM_LIMIT)


def _tile(n, pref):
    t = min(n, pref)
    assert n % t == 0, (n, pref)
    return t


def _iota(shape, dim):
    return lax.broadcasted_iota(jnp.int32, shape, dim)


def _dot(a, b):
    return jnp.dot(a, b, preferred_element_type=F32)


def _dot_nt(a, b):
    return lax.dot_general(a, b, (((1,), (1,)), ((), ())), preferred_element_type=F32)


def _proj_kernel(x_ref, w_ref, aux_ref, o_ref, *, seq, tm):
    acc = _dot(x_ref[...], w_ref[...])
    pos = (pl.program_id(0) * tm) % seq + _iota((tm, 1), 0)
    a = (pos >> LANE_SHIFT).astype(F32)
    b = (pos & (LANE - 1)).astype(F32)
    aux = aux_ref[...]
    o_ref[...] = (acc + aux[0:1] + a * aux[1:2] + b * aux[2:3]).astype(o_ref.dtype)


def _project(xb, w, aux, seq, out_dtype, tn_pref):
    t, d = xb.shape
    n = w.shape[1]
    tm = _tile(t, 1024)
    tn = _tile(n, tn_pref)
    return pl.pallas_call(
        functools.partial(_proj_kernel, seq=seq, tm=tm),
        grid=(t // tm, n // tn),
        in_specs=[pl.BlockSpec((tm, d), lambda i, j: (i, 0)),
                  pl.BlockSpec((d, tn), lambda i, j: (0, j)),
                  pl.BlockSpec((SUBLANE, tn), lambda i, j: (0, j))],
        out_specs=pl.BlockSpec((tm, tn), lambda i, j: (i, j)),
        out_shape=jax.ShapeDtypeStruct((t, n), out_dtype),
        compiler_params=_cparams("parallel", "parallel"),
        name="proj",
    )(xb, w, aux)


def _compress_kernel(x_ref, pe_ref, w1_ref, w2_ref, o_ref):
    xb = (x_ref[0].astype(F32) + pe_ref[0]).astype(BF16)
    h = jax.nn.gelu(_dot(xb, w1_ref[0]))
    o_ref[0] = _dot(h.astype(BF16), w2_ref[0])


def _compress(xs, pes, w1s, w2s):
    _, r, kdim = xs.shape
    hid = w1s.shape[2]
    dh = w2s.shape[2]
    return pl.pallas_call(
        _compress_kernel,
        grid=(2,),
        in_specs=[pl.BlockSpec((1, r, kdim), lambda i: (i, 0, 0)),
                  pl.BlockSpec((1, 1, kdim), lambda i: (i, 0, 0)),
                  pl.BlockSpec((1, kdim, hid), lambda i: (i, 0, 0)),
                  pl.BlockSpec((1, hid, dh), lambda i: (i, 0, 0))],
        out_specs=pl.BlockSpec((1, r, dh), lambda i: (i, 0, 0)),
        out_shape=jax.ShapeDtypeStruct((2, r, dh), F32),
        compiler_params=_cparams("parallel"),
        name="nsa_compress",
    )(xs, pes, w1s, w2s)


def _stack_heads(q, n):
    return jnp.concatenate([q[:, g * LANE:(g + 1) * LANE] for g in range(n)], axis=0)


def _unstack_heads(o, n, tq, width):
    return jnp.concatenate([o[g * tq:(g + 1) * tq, :width] for g in range(n)], axis=1)


def _cmp_kernel(q_ref, kc_ref, vc_ref, oc_ref, sel_ref, touch_ref, *, tq, n_sel):
    i = pl.program_id(2)
    qs = _stack_heads(q_ref[0], GROUP)
    s = _dot_nt(qs, kc_ref[0, 0])
    t1 = i * tq + _iota((tq, 1), 0)
    t = jnp.concatenate([t1] * GROUP, axis=0)
    n = _iota((1, NBLK_PAD), 1)
    valid = (n * NSA_BLOCK + (NSA_BLOCK - 1)) <= t
    s = jnp.where(valid, s, NEG)
    mx = jnp.max(s, axis=-1, keepdims=True)
    e = jnp.where(valid, jnp.exp(s - mx), 0.0)
    p = e / jnp.maximum(jnp.sum(e, axis=-1, keepdims=True), 1e-30)
    o = _dot(p.astype(BF16), vc_ref[0, 0])
    oc_ref[0] = _unstack_heads(o, GROUP, tq, HEAD_DIM)

    imp = p[0:tq]
    for g in range(1, GROUP):
        imp = imp + p[g * tq:(g + 1) * tq]
    nb = _iota((NBLK_PAD, 1), 0)
    rel = ((i * tq + _iota((1, tq), 1)) >> BLOCK_SHIFT) - nb
    forced = (nb == 0) | ((rel >= 0) & (rel < NSA_LOCAL))
    val = jnp.where(forced, FORCE, jnp.where(rel >= 0, imp.T, -1.0))
    sel = jnp.zeros((NBLK_PAD, tq), F32)
    nf = nb.astype(F32)
    for _ in range(n_sel):
        mx = jnp.max(val, axis=0, keepdims=True)
        idx = jnp.min(jnp.where(val == mx, nf, float(NBLK_PAD)), axis=0, keepdims=True)
        pick = nf == idx
        sel = jnp.where(pick & (mx >= 0.0), 1.0, sel)
        val = jnp.where(pick, -jnp.inf, val)
    sel = sel.T
    sel_ref[0, 0] = (1.0 - sel).astype(BF16)
    touch_ref[0, 0, 0] = jnp.broadcast_to(jnp.max(sel, axis=0, keepdims=True), (SUBLANE, NBLK_PAD))


def _nsa_compressed(proj, kca, vca, seq, n_sel):
    b = proj.shape[0]
    tq = _tile(seq, 256)
    lay = _LAYOUT
    gw = GROUP * LANE
    return pl.pallas_call(
        functools.partial(_cmp_kernel, tq=tq, n_sel=n_sel),
        grid=(b, NSA_KV, seq // tq),
        in_specs=[pl.BlockSpec((1, tq, gw), lambda b_, k, i: (b_, i, lay.qa // gw + k)),
                  pl.BlockSpec((1, 1, NBLK_PAD, LANE), lambda b_, k, i: (b_, k, 0, 0)),
                  pl.BlockSpec((1, 1, NBLK_PAD, LANE), lambda b_, k, i: (b_, k, 0, 0))],
        out_specs=[pl.BlockSpec((1, tq, GROUP * HEAD_DIM), lambda b_, k, i: (b_, i, k)),
                   pl.BlockSpec((1, 1, tq, NBLK_PAD), lambda b_, k, i: (b_, k, i, 0)),
                   pl.BlockSpec((1, 1, 1, SUBLANE, NBLK_PAD), lambda b_, k, i: (b_, k, i, 0, 0))],
        out_shape=[jax.ShapeDtypeStruct((b, seq, NSA_HEADS * HEAD_DIM), F32),
                   jax.ShapeDtypeStruct((b, NSA_KV, seq, NBLK_PAD), BF16),
                   jax.ShapeDtypeStruct((b, NSA_KV, seq // tq, SUBLANE, NBLK_PAD), F32)],
        compiler_params=_cparams("parallel", "parallel", "parallel"),
        name="nsa_compressed",
    )(proj, kca, vca)


def _flash(chunk_fns, v_ref, i, tq, tk, m_ref, acc_ref, s_refs, n_active=None, tile_at=None):
    m_ref[...] = jnp.full(m_ref.shape, NEG, F32)
    acc_ref[...] = jnp.zeros(acc_ref.shape, F32)
    acc_reps = acc_ref.shape[1] // LANE
    n_full = (i * tq) // tk
    if n_active is None:
        n_active = n_full
        tile_at = lambda n: n

    def nth_tile(n):
        return jnp.where(n < n_active, tile_at(n), n_full)

    def issue(n, s_ref):
        start = pl.multiple_of(nth_tile(n) * tk, tk)
        for c, fn in enumerate(chunk_fns):
            s_ref[c * tq:(c + 1) * tq, :] = fn(start)

    def accumulate(j, s_ref, visible):
        start = pl.multiple_of(j * tk, tk)
        vt = v_ref[pl.ds(start, tk), :]
        for c in range(len(chunk_fns)):
            rows = slice(c * tq, (c + 1) * tq)
            s = s_ref[rows, :]
            if visible is not None:
                s = jnp.where(visible, s, NEG)
            m_old = m_ref[rows, :]
            m_new = jnp.maximum(m_old, jnp.max(s, axis=-1, keepdims=True))
            p = jnp.exp((s - jnp.concatenate([m_new] * (tk // LANE), axis=1)).astype(BF16))
            alpha = jnp.concatenate([jnp.exp(m_old - m_new)] * acc_reps, axis=1)
            acc_ref[rows, :] = alpha * acc_ref[rows, :] + _dot(p, vt)
            m_ref[rows, :] = m_new

    s0, s1 = s_refs
    issue(0, s0)

    def body(pair, carry):
        n = 2 * pair
        issue(n + 1, s1)
        accumulate(nth_tile(n), s0, None)
        issue(n + 2, s0)
        accumulate(nth_tile(n + 1), s1, None)
        return carry

    pairs = n_active // 2
    lax.fori_loop(0, pairs, body, 0)
    visible = (_iota((tq, tk), 1) - _iota((tq, tk), 0)) <= i * tq - n_full * tk
    leftover = n_active - 2 * pairs

    @pl.when(leftover == 1)
    def _():
        issue(n_active, s1)
        accumulate(nth_tile(n_active - 1), s0, None)
        accumulate(n_full, s1, visible)

    @pl.when(leftover == 0)
    def _():
        accumulate(n_full, s0, visible)


def _sel_kernel(order_ref, count_ref, q_ref, sel_ref, k_ref, v_ref, o_ref, qa_ref, m_ref, acc_ref,
                s0_ref, s1_ref, *, tq, tk, n_tiles):
    i = pl.program_id(2)
    tile_id = (pl.program_id(0) * pl.num_programs(1) + pl.program_id(1)) * pl.num_programs(2) + i
    q = q_ref[0]
    sc = sel_ref[0, 0]
    for g in range(GROUP):
        qa_ref[g * tq:(g + 1) * tq, 0:LANE] = q[:, g * LANE:(g + 1) * LANE]
        qa_ref[g * tq:(g + 1) * tq, LANE:2 * LANE] = sc
    kv = k_ref.at[0, 0]

    def head_scores(g):
        return lambda start: _dot_nt(qa_ref[g * tq:(g + 1) * tq, :], kv[pl.ds(start, tk), :])

    _flash([head_scores(g) for g in range(GROUP)], v_ref.at[0], i, tq, tk, m_ref, acc_ref, (s0_ref, s1_ref),
           n_active=count_ref[tile_id], tile_at=lambda n: order_ref[tile_id * n_tiles + n])
    acc = acc_ref[...]
    o = acc / acc[:, HEAD_DIM:HEAD_DIM + 1]
    o_ref[0] = _unstack_heads(o, GROUP, tq, HEAD_DIM)


def _nsa_selected(proj, selc, touch, ksel, seq):
    b = proj.shape[0]
    tq = _tile(seq, 256)
    tk = _tile(seq, 256)
    n_tiles = seq // tk
    lay = _LAYOUT
    gw = GROUP * LANE
    per_tile = touch[:, :, :, 0, :seq // NSA_BLOCK].reshape(b, NSA_KV, seq // tq, n_tiles, tk // NSA_BLOCK)
    n_full = (np.arange(seq // tq) * tq) // tk
    active = (jnp.max(per_tile, axis=-1) > 0.0) & jnp.asarray(np.arange(n_tiles)[None, :] < n_full[:, None])
    order = jnp.argsort(jnp.where(active, 0, 1), axis=-1, stable=True).astype(jnp.int32).reshape(-1)
    count = jnp.sum(active, axis=-1).astype(jnp.int32).reshape(-1)
    grid_spec = pltpu.PrefetchScalarGridSpec(
        num_scalar_prefetch=2,
        grid=(b, NSA_KV, seq // tq),
        in_specs=[pl.BlockSpec((1, tq, gw), lambda b_, k, i, o, c: (b_, i, lay.qa // gw + k)),
                  pl.BlockSpec((1, 1, tq, NBLK_PAD), lambda b_, k, i, o, c: (b_, k, i, 0)),
                  pl.BlockSpec((1, 1, seq, 2 * LANE), lambda b_, k, i, o, c: (b_, k, 0, 0)),
                  pl.BlockSpec((1, seq, LANE), lambda b_, k, i, o, c: (b_, 0, lay.vs // LANE + k))],
        out_specs=pl.BlockSpec((1, tq, GROUP * HEAD_DIM), lambda b_, k, i, o, c: (b_, i, k)),
        scratch_shapes=[pltpu.VMEM((GROUP * tq, 2 * LANE), BF16),
                        pltpu.VMEM((GROUP * tq, LANE), F32),
                        pltpu.VMEM((GROUP * tq, LANE), F32),
                        pltpu.VMEM((GROUP * tq, tk), F32),
                        pltpu.VMEM((GROUP * tq, tk), F32)])
    return pl.pallas_call(
        functools.partial(_sel_kernel, tq=tq, tk=tk, n_tiles=n_tiles),
        grid_spec=grid_spec,
        out_shape=jax.ShapeDtypeStruct((b, seq, NSA_HEADS * HEAD_DIM), F32),
        compiler_params=_cparams("parallel", "parallel", "arbitrary"),
        name="nsa_selected",
    )(order, count, proj, selc, ksel, proj)


def _diff_kernel(lam_ref, q_ref, k_ref, v_ref, g_ref, o_ref, m_ref, acc_ref, s0_ref, s1_ref,
                 *, tq, tk, lambda_init):
    i = pl.program_id(2)
    kv = k_ref.at[0]

    def map_scores(r):
        cols = slice(r * LANE, (r + 1) * LANE)
        return lambda start: _dot_nt(q_ref[0, :, cols], kv[pl.ds(start, tk), cols])

    _flash([map_scores(0), map_scores(1)], v_ref.at[0], i, tq, tk, m_ref, acc_ref, (s0_ref, s1_ref))
    acc = acc_ref[...]
    dv = 2 * HEAD_DIM
    a = acc[:, :dv] / acc[:, dv:dv + 1]
    lv = lam_ref[...]
    lam = (jnp.exp(jnp.sum(lv[0:1] * lv[1:2], axis=-1, keepdims=True))
           - jnp.exp(jnp.sum(lv[2:3] * lv[3:4], axis=-1, keepdims=True)) + lambda_init)
    o = a[0:tq] - lam * a[tq:2 * tq]
    o = o * lax.rsqrt(jnp.mean(o * o, axis=-1, keepdims=True) + LN_EPS) * g_ref[...]
    o_ref[0] = (o * (1.0 - lambda_init)).astype(o_ref.dtype)


def _diff_attention(proj, lam_vecs, norm_g, seq, lambda_init):
    b = proj.shape[0]
    tq = _tile(seq, 1024)
    tk = _tile(seq, 1024)
    lay = _LAYOUT
    w2 = 2 * LANE
    return pl.pallas_call(
        functools.partial(_diff_kernel, tq=tq, tk=tk, lambda_init=lambda_init),
        grid=(b, DIFF_HEADS, seq // tq),
        in_specs=[pl.BlockSpec((4, HEAD_DIM), lambda b_, h, i: (0, 0)),
                  pl.BlockSpec((1, tq, w2), lambda b_, h, i: (b_, i, lay.qd // w2 + h)),
                  pl.BlockSpec((1, seq, w2), lambda b_, h, i: (b_, 0, lay.kd // w2 + h)),
                  pl.BlockSpec((1, seq, w2), lambda b_, h, i: (b_, 0, lay.vd // w2 + h)),
                  pl.BlockSpec((1, 2 * HEAD_DIM), lambda b_, h, i: (0, 0))],
        out_specs=pl.BlockSpec((1, tq, 2 * HEAD_DIM), lambda b_, h, i: (b_, i, h)),
        out_shape=jax.ShapeDtypeStruct((b, seq, DIFF_HEADS * 2 * HEAD_DIM), BF16),
        scratch_shapes=[pltpu.VMEM((2 * tq, LANE), F32),
                        pltpu.VMEM((2 * tq, w2), F32),
                        pltpu.VMEM((2 * tq, tk), F32),
                        pltpu.VMEM((2 * tq, tk), F32)],
        compiler_params=_cparams("parallel", "parallel", "arbitrary"),
        name="diff_attention",
    )(lam_vecs, proj, proj, proj, norm_g)


def _band_kernel(hp_ref, q_ref, k_ref, v_ref, o_ref, *, tq, window, band, use_sinks):
    kvh = pl.program_id(1)
    i = pl.program_id(2)
    start = pl.multiple_of(jnp.maximum(i * tq - window, 0), LANE)
    kb = k_ref[0, pl.ds(start, band), :]
    vb = v_ref[0, pl.ds(start, band), :]
    t1 = i * tq + _iota((tq, 1), 0)
    dist = t1 - (start + _iota((1, band), 1))
    visible = (dist >= 0) & (dist < window)
    outs = []
    scores = [_dot_nt(q_ref[0, :, g * LANE:(g + 1) * LANE], kb) for g in range(GROUP)]
    for g in range(GROUP):
        s = jnp.where(visible, scores[g], NEG)
        mx = jnp.max(s, axis=-1, keepdims=True)
        if use_sinks:
            h = kvh * GROUP + g
            sink = hp_ref[1, h] + hp_ref[0, h] * t1.astype(F32)
            mx = jnp.maximum(mx, sink)
        acc = _dot(jnp.exp((s - mx).astype(BF16)), vb)
        den = acc[:, HEAD_DIM:HEAD_DIM + 1]
        if use_sinks:
            den = den + jnp.exp(sink - mx)
        outs.append((acc / den)[:, :HEAD_DIM])
    o_ref[0] = jnp.concatenate(outs, axis=1).astype(o_ref.dtype)


def _banded(proj, head_params, seq, window, q_off, k_off, v_off, use_sinks, out_dtype, name):
    b = proj.shape[0]
    tq = _tile(seq, 256)
    band = min(window + tq, seq)
    gw = GROUP * LANE
    return pl.pallas_call(
        functools.partial(_band_kernel, tq=tq, window=window, band=band, use_sinks=use_sinks),
        grid=(b, NSA_KV, seq // tq),
        in_specs=[pl.BlockSpec(memory_space=pltpu.SMEM),
                  pl.BlockSpec((1, tq, gw), lambda b_, k, i: (b_, i, q_off // gw + k)),
                  pl.BlockSpec((1, seq, LANE), lambda b_, k, i: (b_, 0, k_off // LANE + k)),
                  pl.BlockSpec((1, seq, LANE), lambda b_, k, i: (b_, 0, v_off // LANE + k))],
        out_specs=pl.BlockSpec((1, tq, GROUP * HEAD_DIM), lambda b_, k, i: (b_, i, k)),
        out_shape=jax.ShapeDtypeStruct((b, seq, NSA_HEADS * HEAD_DIM), out_dtype),
        compiler_params=_cparams("parallel", "parallel", "parallel"),
        name=name,
    )(head_params, proj, proj, proj)


def _layer_norm(z, g, b):
    mu = jnp.mean(z, axis=-1, keepdims=True)
    zc = z - mu
    var = jnp.mean(zc * zc, axis=-1, keepdims=True)
    return zc * lax.rsqrt(var + LN_EPS) * g + b


def _merge_kernel(x_ref, xin_ref, oc_ref, os_ref, ow_ref, ga_ref, ob_ref, od_ref, wgm_ref, ex_ref,
                  wa_ref, wb_ref, wc_ref, wo_ref, lng_ref, lnb_ref, rw_ref, rb_ref,
                  xo_ref, xb_ref, cw_ref, *, alpha):
    d = x_ref.shape[1]
    ga = jax.nn.sigmoid(ga_ref[...].astype(F32)).astype(BF16)
    oa = (_dot(ga, ex_ref[0]) * oc_ref[...] + _dot(ga, ex_ref[1]) * os_ref[...]
          + _dot(ga, ex_ref[2]) * ow_ref[...])
    gm = jax.nn.sigmoid(_dot(xin_ref[...], wgm_ref[...]))
    merged = (gm[:, 0:d] * _dot(oa.astype(BF16), wa_ref[...])
              + gm[:, d:2 * d] * _dot(ob_ref[...], wb_ref[...])
              + gm[:, 2 * d:3 * d] * _dot(od_ref[...], wc_ref[...]))
    z = alpha * x_ref[...] + _dot(merged.astype(BF16), wo_ref[...])
    y = _layer_norm(z, lng_ref[...], lnb_ref[...])
    xo_ref[...] = y
    xb_ref[...] = y.astype(BF16)
    cw_ref[...] = _route(y, rw_ref[...], rb_ref[...])


def _merge(x, xb, oc, os_, ow, proj2d, ob, od, wgm, expand, wa, wb, wc, wo, lng, lnb, rw_t, rb, alpha):
    t, d = x.shape
    tm = _tile(t, 512)
    ha = NSA_HEADS * HEAD_DIM
    row = lambda w: pl.BlockSpec((tm, w), lambda i: (i, 0))
    full = lambda a: pl.BlockSpec(a.shape, lambda i: (0,) * a.ndim)
    return pl.pallas_call(
        functools.partial(_merge_kernel, alpha=alpha),
        grid=(t // tm,),
        in_specs=[row(d), row(d), row(ha), row(ha), row(ha),
                  pl.BlockSpec((tm, LANE), lambda i: (i, _LAYOUT.ga // LANE)),
                  row(ha), row(ha), full(wgm), full(expand),
                  full(wa), full(wb), full(wc), full(wo), full(lng), full(lnb), full(rw_t), full(rb)],
        out_specs=[row(d), row(d), pl.BlockSpec((N_EXPERTS, tm), lambda i: (0, i))],
        out_shape=[jax.ShapeDtypeStruct((t, d), F32), jax.ShapeDtypeStruct((t, d), BF16),
                   jax.ShapeDtypeStruct((N_EXPERTS, t), F32)],
        compiler_params=_cparams("parallel"),
        name="merge_norm_route",
    )(x, xb, oc, os_, ow, proj2d, ob, od, wgm, expand, wa, wb, wc, wo, lng, lnb, rw_t, rb)


def _first_max(v, idx, n):
    mx = jnp.max(v, axis=0, keepdims=True)
    first = jnp.min(jnp.where(v == mx, idx, n), axis=0, keepdims=True)
    return mx, idx == first


def _route(x, rw, rb):
    gsz = N_EXPERTS // N_GROUPS
    logits = lax.dot_general(rw, x, (((1,), (1,)), ((), ())),
                             precision=lax.Precision.HIGHEST,
                             preferred_element_type=F32)
    tm = logits.shape[1]
    scores = jax.nn.sigmoid(logits)
    biased = scores + rb
    iw = _iota((gsz, tm), 0)
    gs = []
    for g in range(N_GROUPS):
        blk = biased[g * gsz:(g + 1) * gsz]
        m1, pick = _first_max(blk, iw, gsz)
        m2 = jnp.max(jnp.where(pick, -jnp.inf, blk), axis=0, keepdims=True)
        gs.append(m1 + m2)
    gs = jnp.concatenate(gs, axis=0)
    ig = _iota((N_GROUPS, tm), 0)
    gmask = jnp.zeros((N_GROUPS, tm), F32)
    for _ in range(TOPK_GROUPS):
        _, pick = _first_max(gs, ig, N_GROUPS)
        gmask = jnp.where(pick, 1.0, gmask)
        gs = jnp.where(pick, -jnp.inf, gs)
    emask = jnp.concatenate(
        [jnp.broadcast_to(gmask[g:g + 1], (gsz, tm)) for g in range(N_GROUPS)], axis=0)
    cand = jnp.where(emask > 0.5, biased, NEG)
    ie = _iota((N_EXPERTS, tm), 0)
    chosen = jnp.zeros((N_EXPERTS, tm), F32)
    for _ in range(TOP_K):
        _, pick = _first_max(cand, ie, N_EXPERTS)
        chosen = jnp.where(pick, 1.0, chosen)
        cand = jnp.where(pick, -jnp.inf, cand)
    w = scores * chosen
    return w / jnp.sum(w, axis=0, keepdims=True) * ROUTED_SCALE


def _moe_kernel(x_ref, cw_ref, wg_ref, wu_ref, wd_ref, y_ref, *, ne):
    c = pl.program_id(1)
    x = x_ref[...]
    edim = wg_ref.shape[2]
    cw = cw_ref[...]
    hs = []
    for e in range(ne):
        h = jax.nn.silu(_dot(x, wg_ref[e])) * _dot(x, wu_ref[e]) * cw[:, e:e + 1]
        hs.append(h.astype(BF16))
    y = _dot(jnp.concatenate(hs, axis=1), wd_ref[...].reshape(ne * edim, wd_ref.shape[2]))

    @pl.when(c == 0)
    def _():
        y_ref[...] = y

    @pl.when(c != 0)
    def _():
        y_ref[...] += y


def _moe_dense(xb, cw_t, wg, wu, wd, layer):
    t, d = xb.shape
    edim = wg.shape[3]
    tm = _tile(t, 1024)
    ne = 8
    cw = cw_t.reshape(N_EXPERTS // ne, ne, t).transpose(0, 2, 1)
    return pl.pallas_call(
        functools.partial(_moe_kernel, ne=ne),
        grid=(t // tm, N_EXPERTS // ne),
        in_specs=[pl.BlockSpec((tm, d), lambda i, c: (i, 0)),
                  pl.BlockSpec((None, tm, ne), lambda i, c: (c, i, 0)),
                  pl.BlockSpec((None, ne, d, edim), lambda i, c: (layer, c, 0, 0)),
                  pl.BlockSpec((None, ne, d, edim), lambda i, c: (layer, c, 0, 0)),
                  pl.BlockSpec((None, ne, edim, d), lambda i, c: (layer, c, 0, 0))],
        out_specs=pl.BlockSpec((tm, d), lambda i, c: (i, 0)),
        out_shape=jax.ShapeDtypeStruct((t, d), F32),
        compiler_params=_cparams("parallel", "arbitrary"),
        name="moe_experts",
    )(xb, cw, wg, wu, wd)


def _tail_kernel(x_ref, xb_ref, y_ref, p_ref, sg_ref, su_ref, sd_ref, pg_ref, pp_ref,
                 lng_ref, lnb_ref, xo_ref, xbo_ref, *, alpha):
    xb = xb_ref[...]
    h = jax.nn.silu(_dot(xb, sg_ref[...])) * _dot(xb, su_ref[...])
    shared = _dot(h.astype(BF16), sd_ref[...])
    ple = jax.nn.sigmoid(_dot(xb, pg_ref[...])) * _dot(p_ref[...].astype(BF16), pp_ref[...])
    z = alpha * x_ref[...] + y_ref[...] + shared + ple
    y = _layer_norm(z, lng_ref[...], lnb_ref[...])
    xo_ref[...] = y
    xbo_ref[...] = y.astype(BF16)


def _tail(x, xb, y, p, sg, su, sd, pg, pp, lng, lnb, alpha):
    t, d = x.shape
    tm = _tile(t, 512)
    row = lambda w: pl.BlockSpec((tm, w), lambda i: (i, 0))
    full = lambda a: pl.BlockSpec(a.shape, lambda i: (0,) * a.ndim)
    return pl.pallas_call(
        functools.partial(_tail_kernel, alpha=alpha),
        grid=(t // tm,),
        in_specs=[row(d), row(d), row(d), row(p.shape[1]), full(sg), full(su), full(sd),
                  full(pg), full(pp), full(lng), full(lnb)],
        out_specs=[row(d), row(d)],
        out_shape=[jax.ShapeDtypeStruct((t, d), F32), jax.ShapeDtypeStruct((t, d), BF16)],
        compiler_params=_cparams("parallel"),
        name="ffn_tail_norm",
    )(x, xb, y, p, sg, su, sd, pg, pp, lng, lnb)


def _slot_weights(w):
    d = w.shape[0]
    o = IN_OFFS
    qs = HEAD_DIM ** -0.5

    def heads(off, n, width=HEAD_DIM, slot=LANE, scale=1.0):
        blk = w[:, off:off + n * width].reshape(d, n, width) * scale
        return jnp.pad(blk, ((0, 0), (0, 0), (0, slot - width))).reshape(d, n * slot)

    gates = w[:, o[7]:o[8]].reshape(d, NSA_HEADS, 3).transpose(0, 2, 1).reshape(d, 3 * NSA_HEADS)
    parts = [heads(o[0], NSA_HEADS, scale=qs), heads(o[8], SWA_HEADS, scale=qs),
             heads(o[11], 2 * DIFF_HEADS, scale=qs), heads(o[12], 2 * DIFF_HEADS),
             heads(o[13], DIFF_HEADS, 2 * HEAD_DIM, 2 * LANE),
             heads(o[3], NSA_KV), heads(o[4], NSA_KV), heads(o[5], NSA_KV), heads(o[6], NSA_KV),
             heads(o[9], SWA_KV), heads(o[10], SWA_KV),
             w[:, o[1]:o[2]], w[:, o[2]:o[3]],
             jnp.pad(gates, ((0, 0), (0, LANE - 3 * NSA_HEADS)))]
    used = sum(part.shape[1] for part in parts)
    parts.append(jnp.zeros((d, _LAYOUT.width - used), F32))
    out = jnp.concatenate(parts, axis=1)
    assert out.shape[1] == _LAYOUT.width
    return out.astype(BF16)


def _compressed_slots(c, pos_cols):
    b, nb, hkv, dh = c.shape
    c = c.transpose(0, 2, 1, 3)
    extra = jnp.zeros((b, hkv, nb, LANE - dh), F32)
    if pos_cols:
        cpos = np.arange(nb) * NSA_BLOCK + NSA_BLOCK - 1
        cols = np.zeros((nb, LANE - dh), np.float32)
        cols[:, 0] = cpos // LANE
        cols[:, 1] = cpos % LANE
        extra = extra + jnp.asarray(cols)
    c = jnp.concatenate([c, extra], axis=-1)
    c = jnp.pad(c, ((0, 0), (0, 0), (0, NBLK_PAD - nb), (0, 0)))
    return c.astype(BF16)


def kernel(x, p, w_in, nsa_pe_k, nsa_w1_k, nsa_w2_k, nsa_pe_v, nsa_w1_v, nsa_w2_v, swa_sinks,
           diff_lq1, diff_lk1, diff_lq2, diff_lk2, diff_norm_g, w_branch_a, w_branch_b, w_branch_c,
           w_out, ln1_g, ln1_b, router_w, router_bias, exp_w_gate, exp_w_up, exp_w_down,
           sh_w_gate, sh_w_up, sh_w_down, ple_w_proj, ple_w_gate, ln2_g, ln2_b):
    bsz, seq, d = x.shape
    depth = w_in.shape[0]
    t = bsz * seq
    alpha = (2 * depth) ** 0.25
    lay = _LAYOUT
    nb = seq // NSA_BLOCK
    assert nb <= NBLK_PAD and seq % LANE == 0
    n_sel = min(NSA_TOPN, nb)

    aux = jnp.asarray(lay.aux)
    blk_mask = jnp.asarray(
        (np.arange(seq)[:, None] // NSA_BLOCK == np.arange(NBLK_PAD)[None, :]) * MASK_NEG, BF16)
    ex = np.zeros((3, LANE, NSA_HEADS * HEAD_DIM), np.float32)
    for c in range(3):
        for h in range(NSA_HEADS):
            ex[c, c * NSA_HEADS + h, h * HEAD_DIM:(h + 1) * HEAD_DIM] = 1.0
    expand = jnp.asarray(ex, BF16)
    slopes8 = jnp.asarray(_slopes(NSA_HEADS), F32)

    wg_all = exp_w_gate.astype(BF16)
    wu_all = exp_w_up.astype(BF16)
    wd_all = exp_w_down.astype(BF16)
    xf = x.reshape(t, d)
    xb = xf.astype(BF16)
    for i in range(depth):
        w_slots = _slot_weights(w_in[i])
        proj2d = _project(xb, w_slots, aux, seq, BF16, PROJ_TN)
        proj = proj2d.reshape(bsz, seq, lay.width)

        def blocks(off):
            c = proj[:, :, off:off + LANE].reshape(bsz, nb, NSA_BLOCK, NSA_KV, HEAD_DIM)
            return c.transpose(0, 1, 3, 2, 4).reshape(bsz * nb * NSA_KV, NSA_BLOCK * HEAD_DIM)
        cmp = _compress(
            jnp.stack([blocks(lay.kc), blocks(lay.vc)]),
            jnp.stack([nsa_pe_k[i].reshape(1, -1), nsa_pe_v[i].reshape(1, -1)]),
            jnp.stack([nsa_w1_k[i], nsa_w1_v[i]]).astype(BF16),
            jnp.stack([nsa_w2_k[i], nsa_w2_v[i]]).astype(BF16))
        cmp = cmp.reshape(2, bsz, nb, NSA_KV, HEAD_DIM)
        kca = _compressed_slots(cmp[0], True)
        vca = _compressed_slots(cmp[1], False)

        o_cmp, selc, touch = _nsa_compressed(proj, kca, vca, seq, n_sel)
        ks = proj[:, :, lay.ks:lay.ks + NSA_KV * LANE].reshape(bsz, seq, NSA_KV, LANE).transpose(0, 2, 1, 3)
        ksel = jnp.concatenate(
            [ks, jnp.broadcast_to(blk_mask[None, None], (bsz, NSA_KV, seq, NBLK_PAD))], axis=-1)
        o_sel = _nsa_selected(proj, selc, touch, ksel, seq)
        hp_a = jnp.stack([slopes8, jnp.zeros_like(slopes8)])
        o_win = _banded(proj, hp_a, seq, NSA_WINDOW, lay.qa, lay.kw, lay.vw, False, F32, "nsa_window")
        hp_b = jnp.stack([slopes8, swa_sinks[i].astype(F32)])
        o_b = _banded(proj, hp_b, seq, SWA_WINDOW, lay.qb, lay.kb, lay.vb, True, BF16, "swa_sinks")

        lambda_init = 0.8 - 0.6 * math.exp(-0.3 * i)
        lam_vecs = jnp.stack([diff_lq1[i], diff_lk1[i], diff_lq2[i], diff_lk2[i]]).astype(F32)
        o_d = _diff_attention(proj, lam_vecs, diff_norm_g[i].reshape(1, -1), seq, lambda_init)

        ha = NSA_HEADS * HEAD_DIM
        xf, xb, cw_t = _merge(
            xf, xb, o_cmp.reshape(t, ha), o_sel.reshape(t, ha), o_win.reshape(t, ha), proj2d,
            o_b.reshape(t, ha), o_d.reshape(t, ha), w_in[i][:, GM_OFF:].astype(BF16), expand,
            w_branch_a[i].astype(BF16), w_branch_b[i].astype(BF16), w_branch_c[i].astype(BF16),
            w_out[i].astype(BF16), ln1_g[i].reshape(1, d), ln1_b[i].reshape(1, d),
            router_w[i].T, router_bias[i].reshape(-1, 1), alpha)
        y = _moe_dense(xb, cw_t, wg_all, wu_all, wd_all, i)
        xf, xb = _tail(xf, xb, y, p[i].reshape(t, -1),
                       sh_w_gate[i].astype(BF16), sh_w_up[i].astype(BF16), sh_w_down[i].astype(BF16),
                       ple_w_gate[i].astype(BF16), ple_w_proj[i].astype(BF16),
                       ln2_g[i].reshape(1, d), ln2_b[i].reshape(1, d), alpha)
    return xf.reshape(bsz, seq, d)
```
